```python
import math
import jax, jax.numpy as jnp
from jax import lax
import numpy as np

D_MODEL = 1024
BATCH = 16
SEQ = 4096
DEPTH = 2

GRID_W = 64
CTX_LEN = 256
N_AB = (DEPTH + 1) // 2
N_C = DEPTH // 2
F32 = jnp.float32
EPS = 1e-6
HY_D = 768
HY_ORDER = 2
HY_EMB = 33
HY_BANDS = (HY_EMB - 1) // 2
HY_FILT = 64
HY_CONV = 3
HY_DECAY_SHORT = 0.3
HY_DECAY_LONG = 1.5
HY_TARGET = 1e-2
S5_D = 256
S5_GROUP = 16
S5_GROUPS = S5_D // S5_GROUP
S5_STATE = 64
S5_DT_MIN = 1e-3
S5_DT_MAX = 1e-1
C_HEADS = 8
C_HEAD_DIM = 128
C_D = C_HEADS * C_HEAD_DIM
C_CHUNK = 64
MOE_GROUPS = 4
MOE_EPG = 8
MOE_TOPK = 2
MOE_HIDDEN = 256

kernel_name = 'hyena_s5_hgrn2_hmoe_prefix_trunk'


def rmsnorm(x, g):
    x32 = x.astype(F32)
    y = x32 * lax.rsqrt(jnp.mean(x32 * x32, axis=-1, keepdims=True) + EPS)
    return (y * g).astype(x.dtype)


def short_conv(u, w, b):
    y = lax.conv_general_dilated(
        u, w.astype(u.dtype)[:, None, :], window_strides=(1,), padding=((1, 1),),
        dimension_numbers=('NWC', 'WIO', 'NWC'), feature_group_count=u.shape[-1])
    return y + b


def hyena_filter_fft(L, fw1, fb1, ff1, fw2, fb2, ff2, fw3):
    pos = jnp.arange(L, dtype=F32)
    t = pos / max(L - 1, 1)
    w = 2.0 * math.pi * pos / L
    bands = jnp.linspace(1e-4, HY_BANDS - 1, HY_BANDS, dtype=F32)
    ang = w[:, None] * bands[None, :]
    z = jnp.concatenate([t[:, None], jnp.cos(ang), -jnp.sin(ang)], axis=-1)
    hdn = jnp.sin(ff1 * (z @ fw1 + fb1))
    hdn = jnp.sin(ff2 * (hdn @ fw2 + fb2))
    h = (hdn @ fw3).astype(F32).reshape(L, HY_ORDER, 2, HY_D)
    deltas = jnp.abs(jnp.linspace(math.log(HY_TARGET) / HY_DECAY_LONG,
                                  math.log(HY_TARGET) / HY_DECAY_SHORT, HY_D, dtype=F32))
    h = h * jnp.exp(-t[:, None] * deltas)[:, None, None, :]
    k = jnp.concatenate([h[:, :, 0], jnp.zeros((1, HY_ORDER, HY_D), F32), h[:0:-1, :, 1]], axis=0)
    k = k * lax.rsqrt(jnp.sum(k * k, axis=0, keepdims=True))
    return jnp.fft.rfft(k, axis=0)


def fft_conv(z, kf, bias):
    L = z.shape[1]
    zf = jnp.fft.rfft(z, n=2 * L, axis=1)
    return jnp.fft.irfft(zf * kf, n=2 * L, axis=1)[:, :L] + z * bias


def hyena_mix(u3, filt, conv_w, conv_b, bias):
    kf = hyena_filter_fft(u3.shape[1], *filt)
    uc = short_conv(u3, conv_w, conv_b).astype(F32)
    x1, x2, v = jnp.split(uc, 3, axis=-1)
    z = x1 * fft_conv(v, kf[:, 0], bias[0])
    return x2 * fft_conv(z, kf[:, 1], bias[1])


def ssm_binop(e_i, e_j):
    a_i, b_i = e_i
    a_j, b_j = e_j
    return a_j * a_i, a_j * b_i + b_j


def s5_discretise(lam_re, lam_im, log_step, b_re, b_im):
    lam = lax.complex(lam_re.astype(F32), lam_im.astype(F32))
    dt = jnp.exp(log_step.astype(F32))[..., None]
    lam_bar = jnp.exp(lam * dt)
    b_bar = ((lam_bar - 1.0) / lam)[..., None] * lax.complex(b_re.astype(F32), b_im.astype(F32))
    return lam_bar, b_bar


def s5_scan(u, lam_bar, b_bar, s0, reverse):
    bu = jnp.einsum('blgn,gpn->blgp', u.astype(jnp.complex64), b_bar)
    if s0 is not None:
        bu = bu.at[:, -1 if reverse else 0].add(lam_bar * s0)
    a = jnp.broadcast_to(lam_bar, (1, u.shape[1]) + lam_bar.shape)
    return lax.associative_scan(ssm_binop, (a, bu), reverse=reverse, axis=1)[1]


def s5_readout(u, st_f, st_b, c_mat, d, glu_w, glu_b):
    Bsz, L = u.shape[:2]
    y = jnp.real(jnp.einsum('blgp,gnp->blgn', st_f, c_mat[0]) + jnp.einsum('blgp,gnp->blgn', st_b, c_mat[1]))
    y = jax.nn.gelu(y.reshape(Bsz, L, S5_D) + d * u.reshape(Bsz, L, S5_D))
    a, g = jnp.split(y @ glu_w + glu_b, 2, axis=-1)
    return a * jax.nn.sigmoid(g)


def ab_mixer(h_lat, h_ctx, w_in, w_out, filt, conv_w, conv_b, hy_bias, s5p, ctx_out):
    lam_re, lam_im, log_step, b_re, b_im, c_re, c_im, d, glu_w, glu_b = s5p
    lam_bar, b_bar = s5_discretise(lam_re, lam_im, log_step, b_re, b_im)
    c_mat = lax.complex(c_re.astype(F32), c_im.astype(F32))

    def s5_in(p):
        return p[..., -S5_D:].astype(F32).reshape(p.shape[0], p.shape[1], S5_GROUPS, S5_GROUP)

    def mix(p, u, st_f, st_b):
        y = jnp.concatenate([hyena_mix(p[..., :3 * HY_D], filt, conv_w, conv_b, hy_bias),
                             s5_readout(u, st_f, st_b, c_mat, d, glu_w, glu_b)], axis=-1)
        return y.astype(p.dtype) @ w_out

    p_ctx = h_ctx @ (w_in if ctx_out else w_in[:, 3 * HY_D:])
    u_ctx = s5_in(p_ctx)
    st_cf = s5_scan(u_ctx, lam_bar[0], b_bar[0], None, False)
    st_cb = s5_scan(u_ctx, lam_bar[1], b_bar[1], None, True)
    p_lat = h_lat @ w_in
    u_lat = s5_in(p_lat)
    st_lf = s5_scan(u_lat, lam_bar[0], b_bar[0], st_cf[:, -1], False)
    st_lb = s5_scan(u_lat, lam_bar[1], b_bar[1], st_cb[:, 0], True)
    y_lat = mix(p_lat, u_lat, st_lf, st_lb)
    y_ctx = mix(p_ctx, u_ctx, st_cf, st_cb) if ctx_out else None
    return y_lat, y_ctx


def gla_chunk_scan(q, k, v, logf, s0):
    with_output = q is not None
    Bsz, H, L, _ = k.shape
    n = L // C_CHUNK

    def chunks(t):
        return t.reshape(Bsz, H, n, C_CHUNK, t.shape[-1]).transpose(2, 0, 1, 3, 4)

    xs = {'k': chunks(k), 'v': chunks(v), 'g': chunks(logf)}
    if with_output:
        xs['q'] = chunks(q)
    lower = jnp.tril(jnp.ones((C_CHUNK, C_CHUNK), bool))[:, :, None]

    def step(S, xc):
        b = jnp.cumsum(xc['g'], axis=2)
        b_end = b[:, :, -1:, :]
        S_new = (jnp.exp(b_end[:, :, 0, :])[..., None] * S
                 + jnp.einsum('bhsd,bhse->bhde', xc['k'] * jnp.exp(b_end - b), xc['v']))
        if not with_output:
            return S_new, None
        qc = xc['q']
        rel = jnp.where(lower, b[:, :, :, None, :] - b[:, :, None, :, :], -jnp.inf)
        att = jnp.einsum('bhtd,bhsd,bhtsd->bhts', qc, xc['k'], jnp.exp(rel))
        o = (jnp.einsum('bhtd,bhde->bhte', qc * jnp.exp(b), S)
             + jnp.einsum('bhts,bhse->bhte', att, xc['v']))
        return S_new, o

    S_fin, o = lax.scan(step, s0, xs)
    if with_output:
        o = o.transpose(1, 2, 0, 3, 4).reshape(Bsz, H, L, -1)
    return o, S_fin


def bidir_gla(q, v, f_fwd, f_bwd, s0_fwd, s0_bwd):
    def flip(t):
        return None if t is None else jnp.flip(t, axis=2)
    o_f, s_f = gla_chunk_scan(q, 1.0 - f_fwd, v, jnp.log(f_fwd), s0_fwd)
    o_b, s_b = gla_chunk_scan(flip(q), flip(1.0 - f_bwd), flip(v), flip(jnp.log(f_bwd)), s0_bwd)
    o = None if q is None else o_f + flip(o_b)
    return o, s_f, s_b


def hgrn2_mixer(h_lat, h_ctx, w_in, w_out, lb, norm_g, ctx_out):
    def heads(t):
        return t.astype(F32).reshape(t.shape[0], t.shape[1], C_HEADS, C_HEAD_DIM).transpose(0, 2, 1, 3)

    def recur_inputs(p):
        i, z_f, z_b = jnp.split(p, 3, axis=-1)
        f_f = lb[0] + (1.0 - lb[0]) * jax.nn.sigmoid(z_f.astype(F32))
        f_b = lb[1] + (1.0 - lb[1]) * jax.nn.sigmoid(z_b.astype(F32))
        return heads(i), heads(f_f), heads(f_b)

    def readout(o, g):
        o = o * lax.rsqrt(jnp.mean(o * o, axis=-1, keepdims=True) + EPS)
        o = o.transpose(0, 2, 1, 3).reshape(g.shape[0], g.shape[1], C_D) * norm_g
        return (o * jax.nn.sigmoid(g.astype(F32))).astype(g.dtype) @ w_out

    zeros = jnp.zeros((h_ctx.shape[0], C_HEADS, C_HEAD_DIM, C_HEAD_DIM), F32)
    if ctx_out:
        p_ctx = h_ctx @ w_in
        q_c = heads(jax.nn.silu(p_ctx[..., :C_D]))
    else:
        p_ctx = h_ctx @ w_in[:, 2 * C_D:]
        q_c = None
    v_c, f_fc, f_bc = recur_inputs(p_ctx[..., -3 * C_D:])
    o_c, s_f, s_b = bidir_gla(q_c, v_c, f_fc, f_bc, zeros, zeros)
    p_lat = h_lat @ w_in
    v_l, f_fl, f_bl = recur_inputs(p_lat[..., 2 * C_D:])
    o_l, _, _ = bidir_gla(heads(jax.nn.silu(p_lat[..., :C_D])), v_l, f_fl, f_bl, s_f, s_b)
    y_lat = readout(o_l, p_lat[..., C_D:2 * C_D])
    y_ctx = readout(o_c, p_ctx[..., C_D:2 * C_D]) if ctx_out else None
    return y_lat, y_ctx


def hier_moe(h, wg, bg, we, be, w_gate, w_up, w_down):
    g_prob = jax.nn.softmax((h @ wg + bg).astype(F32), axis=-1)
    p_top, g_idx = lax.top_k(g_prob, 1)
    e_logits = (h @ we + be).astype(F32).reshape(h.shape[0], h.shape[1], MOE_GROUPS, MOE_EPG)
    e_in = jnp.take_along_axis(e_logits, g_idx[..., None], axis=-2)[..., 0, :]
    e_val, e_idx = lax.top_k(e_in, MOE_TOPK)
    e_w = jax.nn.softmax(e_val, axis=-1) * p_top
    within = jnp.einsum('blk,blke->ble', e_w, jax.nn.one_hot(e_idx, MOE_EPG, dtype=F32))
    gate = jax.nn.one_hot(g_idx[..., 0], MOE_GROUPS, dtype=F32)[..., :, None] * within[..., None, :]
    out = jnp.zeros(h.shape, F32)
    for g in range(MOE_GROUPS):
        a = jnp.einsum('bld,edh->bleh', h, w_gate[g])
        u = jnp.einsum('bld,edh->bleh', h, w_up[g])
        out = out + jnp.einsum('bleh,ehd,ble->bld', jax.nn.silu(a) * u, w_down[g], gate[..., g, :])
    return out.astype(h.dtype)


def setup_inputs(seed: int = 0) -> dict:
    key = jax.random.key(seed)
    keys = iter(jax.random.split(key, 64))

    def nrm(shape, scale):
        return jax.random.normal(next(keys), shape, F32) * scale

    D = D_MODEL
    s5_shape = (N_AB, 2, S5_GROUPS, S5_STATE)
    n_idx = jnp.arange(S5_STATE, dtype=F32)
    return {
        'x': nrm((BATCH, SEQ, D), 1.0),
        'c': nrm((BATCH, D), 1.0),
        'ctx': nrm((BATCH, CTX_LEN, D), 1.0),
        'c_ctx': nrm((D,), 1.0),
        'mod_w': nrm((DEPTH, D, 6 * D), 0.5 * D ** -0.5),
        'mod_b': nrm((DEPTH, 6 * D), 0.02),
        'norm1_g': 1.0 + nrm((DEPTH, D), 0.02),
        'norm2_g': 1.0 + nrm((DEPTH, D), 0.02),
        'final_g': 1.0 + nrm((D,), 0.02),
        'ab_w_in': nrm((N_AB, D, 3 * HY_D + S5_D), D ** -0.5),
        'ab_w_out': nrm((N_AB, HY_D + S5_D, D), (HY_D + S5_D) ** -0.5),
        'hy_conv_w': nrm((N_AB, HY_CONV, 3 * HY_D), HY_CONV ** -0.5),
        'hy_conv_b': nrm((N_AB, 3 * HY_D), 0.02),
        'hy_fw1': nrm((N_AB, HY_EMB, HY_FILT), HY_EMB ** -0.5),
        'hy_fb1': nrm((N_AB, HY_FILT), 0.02),
        'hy_ff1': 1.0 + nrm((N_AB, HY_FILT), 0.02),
        'hy_fw2': nrm((N_AB, HY_FILT, HY_FILT), HY_FILT ** -0.5),
        'hy_fb2': nrm((N_AB, HY_FILT), 0.02),
        'hy_ff2': 1.0 + nrm((N_AB, HY_FILT), 0.02),
        'hy_fw3': nrm((N_AB, HY_FILT, HY_ORDER * 2 * HY_D), HY_FILT ** -0.5),
        'hy_bias': nrm((N_AB, HY_ORDER, HY_D), 0.5),
        's5_lam_re': -0.5 + nrm(s5_shape, 0.01),
        's5_lam_im': math.pi * n_idx + nrm(s5_shape, 0.01),
        's5_log_step': jax.random.uniform(next(keys), (N_AB, 2, S5_GROUPS), F32,
                                          math.log(S5_DT_MIN), math.log(S5_DT_MAX)),
        's5_b_re': nrm(s5_shape + (S5_GROUP,), (2 * S5_GROUP) ** -0.5),
        's5_b_im': nrm(s5_shape + (S5_GROUP,), (2 * S5_GROUP) ** -0.5),
        's5_c_re': nrm((N_AB, 2, S5_GROUPS, S5_GROUP, S5_STATE), S5_STATE ** -0.5),
        's5_c_im': nrm((N_AB, 2, S5_GROUPS, S5_GROUP, S5_STATE), S5_STATE ** -0.5),
        's5_d': nrm((N_AB, S5_D), 1.0),
        's5_glu_w': nrm((N_AB, S5_D, 2 * S5_D), S5_D ** -0.5),
        's5_glu_b': nrm((N_AB, 2 * S5_D), 0.02),
        'c_w_in': nrm((N_C, D, 5 * C_D), D ** -0.5),
        'c_w_out': nrm((N_C, C_D, D), C_D ** -0.5),
        'c_lower_bounds': nrm((2, DEPTH, C_D), 0.1),
        'c_norm_g': 1.0 + nrm((N_C, C_D), 0.02),
        'moe_wg': nrm((DEPTH, D, MOE_GROUPS), D ** -0.5),
        'moe_bg': nrm((DEPTH, MOE_GROUPS), 0.01),
        'moe_we': nrm((DEPTH, D, MOE_GROUPS * MOE_EPG), D ** -0.5),
        'moe_be': nrm((DEPTH, MOE_GROUPS * MOE_EPG), 0.01),
        'moe_w_gate': nrm((DEPTH, MOE_GROUPS, MOE_EPG, D, MOE_HIDDEN), D ** -0.5),
        'moe_w_up': nrm((DEPTH, MOE_GROUPS, MOE_EPG, D, MOE_HIDDEN), D ** -0.5),
        'moe_w_down': nrm((DEPTH, MOE_GROUPS, MOE_EPG, MOE_HIDDEN, D), MOE_HIDDEN ** -0.5),
    }


def reference(x, c, ctx, c_ctx, mod_w, mod_b, norm1_g, norm2_g, final_g,
              ab_w_in, ab_w_out, hy_conv_w, hy_conv_b, hy_fw1, hy_fb1, hy_ff1,
              hy_fw2, hy_fb2, hy_ff2, hy_fw3, hy_bias,
              s5_lam_re, s5_lam_im, s5_log_step, s5_b_re, s5_b_im, s5_c_re, s5_c_im,
              s5_d, s5_glu_w, s5_glu_b,
              c_w_in, c_w_out, c_lower_bounds, c_norm_g,
              moe_wg, moe_bg, moe_we, moe_be, moe_w_gate, moe_w_up, moe_w_down):
    silu_c = jax.nn.silu(c)
    silu_cc = jax.nn.silu(c_ctx)
    sm = jax.nn.softmax(c_lower_bounds.astype(F32), axis=1)
    lower_bounds = jnp.cumsum(sm, axis=1) - sm[:, :1]
    for l in range(DEPTH):
        j = l // 2
        ctx_out = l < DEPTH - 1
        n_mod = 6 if ctx_out else 2
        m = jnp.split((silu_c @ mod_w[l] + mod_b[l])[:, None, :], 6, axis=-1)
        mc = jnp.split(silu_cc @ mod_w[l][:, :n_mod * D_MODEL] + mod_b[l][:n_mod * D_MODEL], n_mod)
        h = rmsnorm(x, norm1_g[l]) * (1.0 + m[1]) + m[0]
        hc = rmsnorm(ctx, norm1_g[l]) * (1.0 + mc[1]) + mc[0]
        if l % 2 == 0:
            filt = (hy_fw1[j], hy_fb1[j], hy_ff1[j], hy_fw2[j], hy_fb2[j], hy_ff2[j], hy_fw3[j])
            s5p = (s5_lam_re[j], s5_lam_im[j], s5_log_step[j], s5_b_re[j], s5_b_im[j],
                   s5_c_re[j], s5_c_im[j], s5_d[j], s5_glu_w[j], s5_glu_b[j])
            y, yc = ab_mixer(h, hc, ab_w_in[j], ab_w_out[j], filt, hy_conv_w[j], hy_conv_b[j],
                             hy_bias[j], s5p, ctx_out)
        else:
            y, yc = hgrn2_mixer(h, hc, c_w_in[j], c_w_out[j], lower_bounds[:, l], c_norm_g[j], ctx_out)
        moe_p = (moe_wg[l], moe_bg[l], moe_we[l], moe_be[l], moe_w_gate[l], moe_w_up[l], moe_w_down[l])
        x = x + m[2] * y
        x = x + m[5] * hier_moe(rmsnorm(x, norm2_g[l]) * (1.0 + m[4]) + m[3], *moe_p)
        if ctx_out:
            ctx = ctx + mc[2] * yc
            ctx = ctx + mc[5] * hier_moe(rmsnorm(ctx, norm2_g[l]) * (1.0 + mc[4]) + mc[3], *moe_p)
    return rmsnorm(x, final_g)
```

```python
import functools
import math

import jax
import jax.numpy as jnp
from jax import lax
from jax.experimental import pallas as pl
from jax.experimental.pallas import tpu as pltpu

F32 = jnp.float32
BF16 = jnp.bfloat16
EPS = 1e-6
HIGHEST = lax.Precision.HIGHEST

D_MODEL = 1024
HY_D = 768
HY_ORDER = 2
HY_EMB = 33
HY_BANDS = (HY_EMB - 1) // 2
HY_DECAY_SHORT = 0.3
HY_DECAY_LONG = 1.5
HY_TARGET = 1e-2
S5_D = 256
S5_GROUP = 16
S5_GROUPS = S5_D // S5_GROUP
S5_STATE = 64
C_HEADS = 8
C_HEAD_DIM = 128
C_D = C_HEADS * C_HEAD_DIM
MOE_GROUPS = 4
MOE_EPG = 8
MOE_EXPERTS = MOE_GROUPS * MOE_EPG
MOE_HIDDEN = 256

LANE = 128
DFT_BLK = 256
S5_CHUNK = 16
GLA_CHUNK = 64
VMEM_LIMIT = 56 * 1024 * 1024


def _cparams(*sem):
    return pltpu.CompilerParams(dimension_semantics=sem, vmem_limit_bytes=VMEM_LIMIT)


def _pick(n, *cands):
    for c in cands:
        if n % c == 0:
            return c
    return n


def _rms_mod(xv, g, sc, sh):
    ms = jnp.mean(xv * xv, axis=-1, keepdims=True)
    return xv * lax.rsqrt(ms + EPS) * g * sc + sh


def _sigmoid(v):
    return 1.0 / (1.0 + jnp.exp(-v))


def _mm_kernel(a_ref, b_ref, o_ref):
    o_ref[...] = jnp.dot(a_ref[...].astype(BF16), b_ref[...],
                         preferred_element_type=F32).astype(o_ref.dtype)


def _mm(a, b, out_dtype=F32, name="mm"):
    M, K = a.shape
    N = b.shape[1]
    tm = _pick(M, 768, 512, 256, 128)
    tn = _pick(N, 512, 256, 128)
    return pl.pallas_call(
        _mm_kernel,
        grid=(M // tm, N // tn),
        in_specs=[pl.BlockSpec((tm, K), lambda i, j: (i, 0)),
                  pl.BlockSpec((K, tn), lambda i, j: (0, j))],
        out_specs=pl.BlockSpec((tm, tn), lambda i, j: (i, j)),
        out_shape=jax.ShapeDtypeStruct((M, N), out_dtype),
        compiler_params=_cparams("parallel", "parallel"),
        name=name,
    )(a, b.astype(BF16))


def _modvec_kernel(c_ref, w_ref, b_ref, o_ref):
    cv = c_ref[...]
    sc = cv * _sigmoid(cv)
    o_ref[...] = jnp.dot(sc, w_ref[...], preferred_element_type=F32, precision=HIGHEST) + b_ref[...]


def _modvec(cvec, w, b):
    R, D = cvec.shape
    N = w.shape[1]
    tn = _pick(N, 512, 256, 128)
    return pl.pallas_call(
        _modvec_kernel,
        grid=(N // tn,),
        in_specs=[pl.BlockSpec((R, D), lambda j: (0, 0)),
                  pl.BlockSpec((D, tn), lambda j: (0, j)),
                  pl.BlockSpec((1, tn), lambda j: (0, j))],
        out_specs=pl.BlockSpec((R, tn), lambda j: (0, j)),
        out_shape=jax.ShapeDtypeStruct((R, N), F32),
        compiler_params=_cparams("parallel"),
        name="modvec",
    )(cvec, w, b.reshape(1, N))


def _inproj0_kernel(x_ref, g_ref, sc_ref, sh_ref, wt_ref, w2_ref, pt_ref, u_ref):
    h = _rms_mod(x_ref[0], g_ref[...], sc_ref[0], sh_ref[0]).astype(BF16)
    pt_ref[0] = lax.dot_general(wt_ref[...], h, (((1,), (1,)), ((), ())),
                                preferred_element_type=F32)
    u_ref[0] = jnp.dot(h, w2_ref[...], preferred_element_type=F32)


def _inproj0(x, g, sc, sh, wt_hy, w_s5):
    B, L, D = x.shape
    C = wt_hy.shape[0]
    tl = _pick(L, 512, 256, 128)
    return pl.pallas_call(
        _inproj0_kernel,
        grid=(B, L // tl),
        in_specs=[pl.BlockSpec((1, tl, D), lambda b, i: (b, i, 0)),
                  pl.BlockSpec((1, D), lambda b, i: (0, 0)),
                  pl.BlockSpec((1, 1, D), lambda b, i: (b, 0, 0)),
                  pl.BlockSpec((1, 1, D), lambda b, i: (b, 0, 0)),
                  pl.BlockSpec((C, D), lambda b, i: (0, 0)),
                  pl.BlockSpec((D, S5_D), lambda b, i: (0, 0))],
        out_specs=[pl.BlockSpec((1, C, tl), lambda b, i: (b, 0, i)),
                   pl.BlockSpec((1, tl, S5_D), lambda b, i: (b, i, 0))],
        out_shape=[jax.ShapeDtypeStruct((B, C, L), F32),
                   jax.ShapeDtypeStruct((B, L, S5_D), F32)],
        compiler_params=_cparams("parallel", "parallel"),
        name="inproj0",
    )(x, g, sc, sh, wt_hy, w_s5)


def _shortconv_kernel(p1_ref, p2_ref, p3_ref, w_ref, b_ref, x1_ref, x2_ref, v_ref):
    L = p1_ref.shape[2]
    lane = lax.broadcasted_iota(jnp.int32, (1, L), 1)
    first = lane == 0
    last = lane == L - 1

    def conv(u, k):
        w = w_ref[k]
        prev = jnp.where(first, 0.0, pltpu.roll(u, 1, 1))
        nxt = jnp.where(last, 0.0, pltpu.roll(u, L - 1, 1))
        return w[:, 0:1] * prev + w[:, 1:2] * u + w[:, 2:3] * nxt + b_ref[k]

    x1_ref[0] = conv(p1_ref[0], 0)
    x2_ref[0] = conv(p2_ref[0], 1)
    v_ref[0] = conv(p3_ref[0], 2).astype(BF16)


def _shortconv(pt, conv_w, conv_b):
    B, C3, L = pt.shape
    tc = 128
    nb = HY_D // tc
    w = conv_w.T.reshape(3, HY_D, 3)
    b = conv_b.reshape(3, HY_D, 1)
    specs = [pl.BlockSpec((1, tc, L), (lambda b_, i, k=k: (b_, k * nb + i, 0))) for k in range(3)]
    o_spec = pl.BlockSpec((1, tc, L), lambda b_, i: (b_, i, 0))
    return pl.pallas_call(
        _shortconv_kernel,
        grid=(B, nb),
        in_specs=specs + [pl.BlockSpec((3, tc, 3), lambda b_, i: (0, i, 0)),
                          pl.BlockSpec((3, tc, 1), lambda b_, i: (0, i, 0))],
        out_specs=[o_spec, o_spec, o_spec],
        out_shape=[jax.ShapeDtypeStruct((B, HY_D, L), F32),
                   jax.ShapeDtypeStruct((B, HY_D, L), F32),
                   jax.ShapeDtypeStruct((B, HY_D, L), BF16)],
        compiler_params=_cparams("parallel", "parallel"),
        name="shortconv",
    )(pt, pt, pt, w, b)


def _filter_kernel(z_ref, w1_ref, b1_ref, f1_ref, w2_ref, b2_ref, f2_ref, w3_ref, dec_ref, o_ref):
    dot = functools.partial(jnp.dot, preferred_element_type=F32, precision=HIGHEST)
    z = z_ref[...]
    hdn = jnp.sin(f1_ref[...] * (dot(z, w1_ref[...]) + b1_ref[...]))
    hdn = jnp.sin(f2_ref[...] * (dot(hdn, w2_ref[...]) + b2_ref[...]))
    t = z[:, 0:1]
    o_ref[...] = dot(hdn, w3_ref[...]) * jnp.exp(-t * dec_ref[...])


def _hyena_filter(L, fw1, fb1, ff1, fw2, fb2, ff2, fw3):
    pos = jnp.arange(L, dtype=F32)
    t = pos / max(L - 1, 1)
    w = 2.0 * math.pi * pos / L
    bands = jnp.linspace(1e-4, HY_BANDS - 1, HY_BANDS, dtype=F32)
    ang = w[:, None] * bands[None, :]
    z = jnp.concatenate([t[:, None], jnp.cos(ang), -jnp.sin(ang)], axis=-1)
    z = jnp.pad(z, ((0, 0), (0, LANE - HY_EMB)))
    w1 = jnp.pad(fw1, ((0, LANE - HY_EMB), (0, 0)))
    deltas = jnp.abs(jnp.linspace(math.log(HY_TARGET) / HY_DECAY_LONG,
                                  math.log(HY_TARGET) / HY_DECAY_SHORT, HY_D, dtype=F32))
    dec = jnp.tile(deltas, HY_ORDER * 2).reshape(1, -1)
    nf = fw1.shape[1]
    No = fw3.shape[1]
    tl = _pick(L, 512, 256)
    full = lambda i: (0, 0)
    return pl.pallas_call(
        _filter_kernel,
        grid=(L // tl,),
        in_specs=[pl.BlockSpec((tl, LANE), lambda i: (i, 0)),
                  pl.BlockSpec((LANE, nf), full), pl.BlockSpec((1, nf), full), pl.BlockSpec((1, nf), full),
                  pl.BlockSpec((nf, nf), full), pl.BlockSpec((1, nf), full), pl.BlockSpec((1, nf), full),
                  pl.BlockSpec((nf, No), full), pl.BlockSpec((1, No), full)],
        out_specs=pl.BlockSpec((tl, No), lambda i: (i, 0)),
        out_shape=jax.ShapeDtypeStruct((L, No), F32),
        compiler_params=_cparams("parallel"),
        name="hyena_filter",
    )(z, w1, fb1.reshape(1, nf), ff1.reshape(1, nf), fw2, fb2.reshape(1, nf), ff2.reshape(1, nf), fw3, dec)


def _dft_mats(L):
    N = 2 * L
    s = jnp.arange(L, dtype=jnp.int32)
    idx = (s[:, None] * s[None, :]) % N
    ang = idx.astype(F32) * (2.0 * math.pi / N)
    c = jnp.cos(ang)
    sn = -jnp.sin(ang)
    alt = jnp.where(s % 2 == 0, 1.0, -1.0).astype(F32)
    sn = sn.at[:, 0].set(alt)
    nb = L // DFT_BLK
    wf = jnp.stack([c.reshape(L, nb, DFT_BLK), sn.reshape(L, nb, DFT_BLK)], axis=2).reshape(L, 2 * L)
    scale = jnp.where(s == 0, 1.0 / N, 2.0 / N).astype(F32)[:, None]
    wi = jnp.stack([(c * scale).reshape(nb, DFT_BLK, L), (sn.T * scale).reshape(nb, DFT_BLK, L)],
                   axis=1).reshape(2 * L, L)
    return wf.astype(BF16), wi.astype(BF16)


def _filter_spectrum(h, wf, L):
    h4 = h.reshape(L, HY_ORDER, 2, HY_D)
    hf = h4[:, :, 0].transpose(1, 2, 0)
    hb = h4[:, :, 1].transpose(1, 2, 0)
    hb = hb * (jnp.arange(L) > 0).astype(F32)
    energy = jnp.sum(hf * hf, axis=-1) + jnp.sum(hb * hb, axis=-1)
    rows = jnp.concatenate([hf.reshape(-1, L), hb.reshape(-1, L)], axis=0)
    spec = _mm(rows, wf, name="filter_dft").reshape(2, HY_ORDER, HY_D, L // DFT_BLK, 2, DFT_BLK)
    re = spec[:, :, :, :, 0].reshape(2, HY_ORDER, HY_D, L)
    im = spec[:, :, :, :, 1].reshape(2, HY_ORDER, HY_D, L)
    nrm = lax.rsqrt(energy)[..., None]
    kr = (re[0] + re[1]) * nrm
    ki = (im[0] - im[1]) * nrm
    nyq = (im[0, :, :, 0] + im[1, :, :, 0]) * nrm[..., 0]
    ki = ki.at[:, :, 0].set(0.0)
    krp = kr.at[:, :, 0].set(nyq)
    return kr, ki, krp


def _dftfwd_kernel(v_ref, wf_ref, kr_ref, ki_ref, krp_ref, y_ref):
    acc = jnp.dot(v_ref[0], wf_ref[...], preferred_element_type=F32)
    vr = acc[:, :DFT_BLK]
    vi = acc[:, DFT_BLK:]
    kr = kr_ref[...]
    ki = ki_ref[...]
    yr = vr * kr - vi * ki
    yi = vr * ki + vi * krp_ref[...]
    y_ref[0] = jnp.concatenate([yr, yi], axis=1).astype(BF16)


def _dftfwd(v, wf, kr, ki, krp):
    B, C, L = v.shape
    nb = L // DFT_BLK
    kspec = pl.BlockSpec((C, DFT_BLK), lambda b, j: (0, j))
    return pl.pallas_call(
        _dftfwd_kernel,
        grid=(B, nb),
        in_specs=[pl.BlockSpec((1, C, L), lambda b, j: (b, 0, 0)),
                  pl.BlockSpec((L, 2 * DFT_BLK), lambda b, j: (0, j)),
                  kspec, kspec, kspec],
        out_specs=pl.BlockSpec((1, C, 2 * DFT_BLK), lambda b, j: (b, 0, j)),
        out_shape=jax.ShapeDtypeStruct((B, C, 2 * L), BF16),
        compiler_params=_cparams("parallel", "arbitrary"),
        name="hyena_dft_fwd",
    )(v, wf, kr, ki, krp)


def _dftinv_kernel(y_ref, wi_ref, xg_ref, v_ref, bias_ref, o_ref):
    conv = jnp.dot(y_ref[0], wi_ref[...], preferred_element_type=F32)
    o_ref[0] = (xg_ref[0] * (conv + v_ref[0].astype(F32) * bias_ref[...])).astype(o_ref.dtype)


def _dftinv(y, wi, xg, v, bias, out_dtype):
    B, C, L2 = y.shape
    L = L2 // 2
    tn = 256
    tspec = pl.BlockSpec((1, C, tn), lambda b, j: (b, 0, j))
    return pl.pallas_call(
        _dftinv_kernel,
        grid=(B, L // tn),
        in_specs=[pl.BlockSpec((1, C, L2), lambda b, j: (b, 0, 0)),
                  pl.BlockSpec((L2, tn), lambda b, j: (0, j)),
                  tspec, tspec,
                  pl.BlockSpec((C, 1), lambda b, j: (0, 0))],
        out_specs=tspec,
        out_shape=jax.ShapeDtypeStruct((B, C, L), out_dtype),
        compiler_params=_cparams("parallel", "arbitrary"),
        name="hyena_dft_inv",
    )(y, wi, xg, v, bias)


def _hyena(pt, conv_w, conv_b, hy_bias, filt):
    L = pt.shape[2]
    wf, wi = _dft_mats(L)
    kr, ki, krp = _filter_spectrum(_hyena_filter(L, *filt), wf, L)
    x1, x2, v = _shortconv(pt, conv_w, conv_b)
    y1 = _dftfwd(v, wf, kr[0], ki[0], krp[0])
    z = _dftinv(y1, wi, x1, v, hy_bias[0].reshape(HY_D, 1), BF16)
    y2 = _dftfwd(z, wf, kr[1], ki[1], krp[1])
    return _dftinv(y2, wi, x2, z, hy_bias[1].reshape(HY_D, 1), BF16)


def _s5_operators(lam_re, lam_im, log_step, b_re, b_im, c_re, c_im):
    T = S5_CHUNK
    lam = lax.complex(lam_re.astype(F32), lam_im.astype(F32))
    dt = jnp.exp(log_step.astype(F32))[..., None]
    lam_bar = jnp.exp(lam * dt)
    b_bar = ((lam_bar - 1.0) / lam)[..., None] * lax.complex(b_re.astype(F32), b_im.astype(F32))
    c_mat = lax.complex(c_re.astype(F32), c_im.astype(F32))
    j = jnp.arange(T + 1, dtype=F32)
    pw = jnp.exp((lam * dt)[:, :, None, :] * j[None, None, :, None])
    kl = jnp.real(jnp.einsum('dgnp,dgjp,dgpm->dgjnm', c_mat, pw[:, :, :T], b_bar))
    lag = jnp.arange(T)[None, :] - jnp.arange(T)[:, None]
    kin = jnp.where((lag >= 0)[None, None, :, :, None, None], kl[:, :, jnp.clip(lag, 0, T - 1)], 0.0)
    kin = kin.transpose(0, 1, 2, 5, 3, 4).reshape(2 * S5_GROUPS, T * S5_GROUP, T * S5_GROUP)
    wst = pw[:, :, T - 1 - jnp.arange(T), :, None] * b_bar[:, :, None]
    wst = wst.transpose(0, 1, 2, 4, 3).reshape(2 * S5_GROUPS, T * S5_GROUP, S5_STATE)
    wst = jnp.concatenate([jnp.real(wst), jnp.imag(wst)], axis=-1)
    cl = c_mat[:, :, None] * pw[:, :, 1:, None, :]
    cl = cl.transpose(0, 1, 4, 2, 3).reshape(2 * S5_GROUPS, S5_STATE, T * S5_GROUP)
    wout = jnp.concatenate([jnp.real(cl), -jnp.imag(cl)], axis=1)
    lt = pw[:, :, T].reshape(2 * S5_GROUPS, 1, S5_STATE)
    a = jnp.concatenate([jnp.real(lt), jnp.real(lt)], axis=-1)
    bc = jnp.concatenate([-jnp.imag(lt), jnp.imag(lt)], axis=-1)
    return kin.astype(BF16), wst.astype(BF16), wout.astype(BF16), a, bc


def _s5_local_kernel(u_ref, wst_ref, o_ref):
    o_ref[0] = jnp.dot(u_ref[0].astype(BF16), wst_ref[0], preferred_element_type=F32)


def _s5_scan_kernel(sloc_ref, s0_ref, a_ref, bc_ref, sinit_ref, sfin_ref):
    nch = sloc_ref.shape[1]
    a = a_ref[0]
    bc = bc_ref[0]

    def body(c, s):
        sinit_ref[0, c] = s
        return a * s + bc * pltpu.roll(s, S5_STATE, 1) + sloc_ref[0, c]

    sfin_ref[0] = lax.fori_loop(0, nch, body, s0_ref[0])


def _s5_out_kernel(u_ref, sinit_ref, kin_ref, wout_ref, o_ref):
    o_ref[0] = (jnp.dot(u_ref[0].astype(BF16), kin_ref[0], preferred_element_type=F32)
                + jnp.dot(sinit_ref[0].astype(BF16), wout_ref[0], preferred_element_type=F32))


def _s5_pass(u, ops, s0):
    kin, wst, wout, a, bc = ops
    B, L, _ = u.shape
    T = S5_CHUNK
    nch = L // T
    GG = 2 * S5_GROUPS
    R = nch * B

    def fold(t):
        return (t.reshape(B, nch, T, S5_GROUPS, S5_GROUP).transpose(3, 1, 0, 2, 4)
                .reshape(S5_GROUPS, R, T * S5_GROUP))

    uu = jnp.concatenate([fold(u), fold(jnp.flip(u, axis=1))], axis=0)
    tr = _pick(R, 1024, 512, 256, 128, 64, 32, 16, 8)
    row = lambda w: pl.BlockSpec((1, tr, w), lambda g, i: (g, i, 0))
    per_g = lambda r, w: pl.BlockSpec((1, r, w), lambda g, i: (g, 0, 0))
    sloc = pl.pallas_call(
        _s5_local_kernel, grid=(GG, R // tr),
        in_specs=[row(256), per_g(256, 128)], out_specs=row(128),
        out_shape=jax.ShapeDtypeStruct((GG, R, 128), F32),
        compiler_params=_cparams("parallel", "parallel"), name="s5_local",
    )(uu, wst)
    g3 = lambda r: pl.BlockSpec((1, r, 128), lambda g: (g, 0, 0))
    g4 = pl.BlockSpec((1, nch, B, 128), lambda g: (g, 0, 0, 0))
    sinit, sfin = pl.pallas_call(
        _s5_scan_kernel, grid=(GG,),
        in_specs=[g4, g3(B), g3(1), g3(1)], out_specs=[g4, g3(B)],
        out_shape=[jax.ShapeDtypeStruct((GG, nch, B, 128), F32),
                   jax.ShapeDtypeStruct((GG, B, 128), F32)],
        compiler_params=_cparams("parallel"), name="s5_scan",
    )(sloc.reshape(GG, nch, B, 128), s0, a, bc)
    y = pl.pallas_call(
        _s5_out_kernel, grid=(GG, R // tr),
        in_specs=[row(256), row(128), per_g(256, 256), per_g(128, 256)], out_specs=row(256),
        out_shape=jax.ShapeDtypeStruct((GG, R, 256), F32),
        compiler_params=_cparams("parallel", "parallel"), name="s5_out",
    )(uu, sinit.reshape(GG, R, 128), kin, wout)
    y = (y.reshape(2, S5_GROUPS, nch, B, T, S5_GROUP).transpose(0, 3, 2, 4, 1, 5)
         .reshape(2, B, L, S5_D))
    return jnp.stack([y[0], jnp.flip(y[1], axis=1)]), sfin


def _gelu_tanh(v):
    return 0.5 * v * (1.0 + jnp.tanh(math.sqrt(2.0 / math.pi) * (v + 0.044715 * v * v * v)))


def _outproj0_kernel(hy_ref, ys_ref, u_ref, d_ref, gw_ref, gb_ref, wa_ref, wb_ref, x_ref, m_ref, o_ref):
    ys = ys_ref[0, 0] + ys_ref[1, 0] + d_ref[...] * u_ref[0]
    glu = jnp.dot(_gelu_tanh(ys).astype(BF16), gw_ref[...], preferred_element_type=F32) + gb_ref[...]
    s5 = glu[:, :S5_D] * _sigmoid(glu[:, S5_D:])
    y = lax.dot_general(hy_ref[0], wa_ref[...], (((0,), (0,)), ((), ())), preferred_element_type=F32)
    y = y + jnp.dot(s5.astype(BF16), wb_ref[...], preferred_element_type=F32)
    o_ref[0] = x_ref[0] + m_ref[0] * y


def _outproj0(hy, ys, u, d, glu_w, glu_b, w_out, x, m2):
    B, L, D = x.shape
    tl = _pick(L, 512, 256, 128)
    full = lambda b, i: (0, 0)
    tok = lambda w: pl.BlockSpec((1, tl, w), lambda b, i: (b, i, 0))
    return pl.pallas_call(
        _outproj0_kernel,
        grid=(B, L // tl),
        in_specs=[pl.BlockSpec((1, HY_D, tl), lambda b, i: (b, 0, i)),
                  pl.BlockSpec((2, 1, tl, S5_D), lambda b, i: (0, b, i, 0)),
                  tok(S5_D),
                  pl.BlockSpec((1, S5_D), full),
                  pl.BlockSpec((S5_D, 2 * S5_D), full),
                  pl.BlockSpec((1, 2 * S5_D), full),
                  pl.BlockSpec((HY_D, D), full),
                  pl.BlockSpec((S5_D, D), full),
                  tok(D),
                  pl.BlockSpec((1, 1, D), lambda b, i: (b, 0, 0))],
        out_specs=tok(D),
        out_shape=jax.ShapeDtypeStruct((B, L, D), F32),
        compiler_params=_cparams("parallel", "parallel"),
        name="outproj0",
    )(hy, ys, u, d.reshape(1, S5_D), glu_w.astype(BF16), glu_b.reshape(1, -1),
      w_out[:HY_D].astype(BF16), w_out[HY_D:].astype(BF16), x, m2)


def _moe_kernel(x_ref, g_ref, sc_ref, sh_ref, m_ref, wr_ref, br_ref, wg_ref, wu_ref, wd_ref, fg_ref,
                o_ref, hn_scr, gate_scr, acc_scr, *, ne, final_norm):
    j = pl.program_id(2)
    nj = pl.num_programs(2)

    @pl.when(j == 0)
    def _():
        hn = _rms_mod(x_ref[0], g_ref[...], sc_ref[0], sh_ref[0])
        hn_scr[...] = hn.astype(BF16)
        logits = jnp.dot(hn, wr_ref[...], preferred_element_type=F32, precision=HIGHEST) + br_ref[...]
        lg = logits[:, :LANE]
        le = logits[:, LANE:]
        lane = lax.broadcasted_iota(jnp.int32, lg.shape, 1)
        neg = -jnp.inf
        lgm = jnp.where(lane < MOE_GROUPS, lg, neg)
        gmax = jnp.max(lgm, axis=1, keepdims=True)
        p_top = 1.0 / jnp.sum(jnp.exp(lgm - gmax), axis=1, keepdims=True)
        gidx = jnp.min(jnp.where(lgm == gmax, lane, LANE), axis=1, keepdims=True)
        lem = jnp.where((lane // MOE_EPG == gidx) & (lane < MOE_EXPERTS), le, neg)
        v1 = jnp.max(lem, axis=1, keepdims=True)
        i1 = jnp.min(jnp.where(lem == v1, lane, LANE), axis=1, keepdims=True)
        lem2 = jnp.where(lane == i1, neg, lem)
        v2 = jnp.max(lem2, axis=1, keepdims=True)
        i2 = jnp.min(jnp.where(lem2 == v2, lane, LANE), axis=1, keepdims=True)
        e2 = jnp.exp(v2 - v1)
        w1 = p_top / (1.0 + e2)
        gate = jnp.where(lane == i1, w1, jnp.where(lane == i2, w1 * e2, 0.0))
        for jj in range(MOE_EXPERTS // ne):
            gate_scr[jj] = gate[:, jj * ne:(jj + 1) * ne]
        acc_scr[...] = jnp.zeros_like(acc_scr)

    hn = hn_scr[...]
    gsl = gate_scr[j]
    for e in range(ne):
        a = jnp.dot(hn, wg_ref[e], preferred_element_type=F32)
        u = jnp.dot(hn, wu_ref[e], preferred_element_type=F32)
        hid = a * _sigmoid(a) * u * gsl[:, e:e + 1]
        acc_scr[...] += jnp.dot(hid.astype(BF16), wd_ref[e], preferred_element_type=F32)

    @pl.when(j == nj - 1)
    def _():
        out = x_ref[0] + m_ref[0] * acc_scr[...]
        if final_norm:
            ms = jnp.mean(out * out, axis=-1, keepdims=True)
            out = out * lax.rsqrt(ms + EPS) * fg_ref[...]
        o_ref[0] = out


def _moe(x, g2, sc, sh, m5, wr, br, wg, wu, wd, final_g=None):
    B, L, D = x.shape
    ne = 4
    tl = _pick(L, 1024, 512, 256, 128)
    final_norm = final_g is not None
    fg = (final_g if final_norm else jnp.ones((D,), F32)).reshape(1, D)
    full = lambda b, i, j: (0, 0)
    tok = pl.BlockSpec((1, tl, D), lambda b, i, j: (b, i, 0))
    vec = pl.BlockSpec((1, 1, D), lambda b, i, j: (b, 0, 0))
    wspec = lambda k, n: pl.BlockSpec((ne, k, n), lambda b, i, j: (j, 0, 0))
    return pl.pallas_call(
        functools.partial(_moe_kernel, ne=ne, final_norm=final_norm),
        grid=(B, L // tl, MOE_EXPERTS // ne),
        in_specs=[tok, pl.BlockSpec((1, D), full), vec, vec, vec,
                  pl.BlockSpec((D, 2 * LANE), full), pl.BlockSpec((1, 2 * LANE), full),
                  wspec(D, MOE_HIDDEN), wspec(D, MOE_HIDDEN), wspec(MOE_HIDDEN, D),
                  pl.BlockSpec((1, D), full)],
        out_specs=tok,
        out_shape=jax.ShapeDtypeStruct((B, L, D), F32),
        scratch_shapes=[pltpu.VMEM((tl, D), BF16),
                        pltpu.VMEM((MOE_EXPERTS // ne, tl, ne), F32),
                        pltpu.VMEM((tl, D), F32)],
        compiler_params=_cparams("parallel", "parallel", "arbitrary"),
        name="moe",
    )(x, g2, sc, sh, m5, wr, br, wg, wu, wd, fg)


def _moe_params(wg, bg, we, be, w_gate, w_up, w_down):
    D = wg.shape[0]
    wr = jnp.concatenate([jnp.pad(wg, ((0, 0), (0, LANE - MOE_GROUPS))),
                          jnp.pad(we, ((0, 0), (0, LANE - MOE_EXPERTS)))], axis=1)
    br = jnp.concatenate([jnp.pad(bg, (0, LANE - MOE_GROUPS)),
                          jnp.pad(be, (0, LANE - MOE_EXPERTS))]).reshape(1, 2 * LANE)
    return (wr, br,
            w_gate.reshape(MOE_EXPERTS, D, MOE_HIDDEN).astype(BF16),
            w_up.reshape(MOE_EXPERTS, D, MOE_HIDDEN).astype(BF16),
            w_down.reshape(MOE_EXPERTS, MOE_HIDDEN, D).astype(BF16))


def _inproj1_kernel(x_ref, g_ref, sc_ref, sh_ref, w_ref, o_ref, h_scr):
    @pl.when(pl.program_id(2) == 0)
    def _():
        h_scr[...] = _rms_mod(x_ref[0], g_ref[...], sc_ref[0], sh_ref[0]).astype(BF16)

    o_ref[0] = jnp.dot(h_scr[...], w_ref[...], preferred_element_type=F32)


def _inproj1(x, g, sc, sh, w):
    B, L, D = x.shape
    N = w.shape[1]
    tl = _pick(L, 512, 256, 128)
    tn = _pick(N, 1280, 1024, 512)
    return pl.pallas_call(
        _inproj1_kernel,
        grid=(B, L // tl, N // tn),
        in_specs=[pl.BlockSpec((1, tl, D), lambda b, i, j: (b, i, 0)),
                  pl.BlockSpec((1, D), lambda b, i, j: (0, 0)),
                  pl.BlockSpec((1, 1, D), lambda b, i, j: (b, 0, 0)),
                  pl.BlockSpec((1, 1, D), lambda b, i, j: (b, 0, 0)),
                  pl.BlockSpec((D, tn), lambda b, i, j: (0, j))],
        out_specs=pl.BlockSpec((1, tl, tn), lambda b, i, j: (b, i, j)),
        out_shape=jax.ShapeDtypeStruct((B, L, N), F32),
        scratch_shapes=[pltpu.VMEM((tl, D), BF16)],
        compiler_params=_cparams("parallel", "parallel", "arbitrary"),
        name="inproj1",
    )(x, g, sc, sh, w)


def _gla_kernel(q_ref, i_ref, z_ref, lb_ref, s0_ref, o_ref, sfin_ref, st_scr, *, rev):
    c = pl.program_id(2)
    C = q_ref.shape[1]

    @pl.when(c == 0)
    def _():
        st_scr[...] = s0_ref[0, 0]

    pq = q_ref[0]
    q = pq * _sigmoid(pq)
    v = i_ref[0]
    lb = lb_ref[...]
    f = lb + (1.0 - lb) * _sigmoid(z_ref[0])
    k = 1.0 - f
    g = jnp.log(f)

    row = lax.broadcasted_iota(jnp.int32, (C, LANE), 0)
    tt = lax.broadcasted_iota(jnp.int32, (C, C), 0)
    ss = lax.broadcasted_iota(jnp.int32, (C, C), 1)
    nt = (((1,), (1,)), ((), ()))
    att = jnp.where(tt == ss, lax.dot_general(q, k, nt, preferred_element_type=F32), 0.0)
    pinc = g
    tot = g
    hs = 1
    while hs < C:
        odd = ((row // hs) % 2) == 1
        if rev:
            aq = tot - pinc + g
            ak = pinc - g
            mask = (((tt // hs) % 2) == 0) & ((ss // hs) == (tt // hs) + 1)
        else:
            aq = pinc
            ak = tot - pinc
            mask = (((tt // hs) % 2) == 1) & ((ss // hs) == (tt // hs) - 1)
        blk = lax.dot_general(q * jnp.exp(aq), k * jnp.exp(ak), nt, preferred_element_type=F32)
        att = att + jnp.where(mask, blk, 0.0)
        tprev = pltpu.roll(tot, hs, 0)
        tnext = pltpu.roll(tot, C - hs, 0)
        pinc = pinc + jnp.where(odd, tprev, 0.0)
        tot = tot + jnp.where(odd, tprev, tnext)
        hs *= 2

    if rev:
        q_dec = tot - pinc + g
        k_dec = pinc - g
    else:
        q_dec = pinc
        k_dec = tot - pinc
    st = st_scr[...]
    o = jnp.dot(att, v, preferred_element_type=F32)
    o = o + lax.dot_general(q * jnp.exp(q_dec), st, nt, preferred_element_type=F32)
    o_ref[0] = o
    upd = lax.dot_general(v, k * jnp.exp(k_dec), (((0,), (0,)), ((), ())), preferred_element_type=F32)
    st_new = st * jnp.exp(tot[0:1, :]) + upd
    st_scr[...] = st_new
    sfin_ref[0, 0] = st_new


def _gla(p, lb, s0, rev):
    B, L, _ = p.shape
    C = GLA_CHUNK
    nch = L // C
    H = C_HEADS
    zoff = 4 * H if rev else 3 * H
    cidx = (lambda c: nch - 1 - c) if rev else (lambda c: c)
    col = lambda off: pl.BlockSpec((1, C, LANE), lambda b, h, c: (b, cidx(c), off + h))
    st_spec = pl.BlockSpec((1, 1, LANE, LANE), lambda b, h, c: (b, h, 0, 0))
    return pl.pallas_call(
        functools.partial(_gla_kernel, rev=rev),
        grid=(B, H, nch),
        in_specs=[col(0), col(2 * H), col(zoff),
                  pl.BlockSpec((1, LANE), lambda b, h, c: (0, h)), st_spec],
        out_specs=[pl.BlockSpec((1, C, LANE), lambda b, h, c: (b, cidx(c), h)), st_spec],
        out_shape=[jax.ShapeDtypeStruct((B, L, C_D), F32),
                   jax.ShapeDtypeStruct((B, H, LANE, LANE), F32)],
        scratch_shapes=[pltpu.VMEM((LANE, LANE), F32)],
        compiler_params=_cparams("parallel", "parallel", "arbitrary"),
        name="gla_bwd" if rev else "gla_fwd",
    )(p, p, p, lb, s0)


def _outproj1_kernel(of_ref, ob_ref, g_ref, ng_ref, w_ref, x_ref, m_ref, o_ref):
    o = of_ref[0] + ob_ref[0]
    parts = []
    for h in range(C_HEADS):
        oh = o[:, h * LANE:(h + 1) * LANE]
        ms = jnp.mean(oh * oh, axis=-1, keepdims=True)
        parts.append(oh * lax.rsqrt(ms + EPS))
    on = jnp.concatenate(parts, axis=1) * ng_ref[...] * _sigmoid(g_ref[0])
    y = jnp.dot(on.astype(BF16), w_ref[...], preferred_element_type=F32)
    o_ref[0] = x_ref[0] + m_ref[0] * y


def _outproj1(o_f, o_b, p, norm_g, w_out, x, m2):
    B, L, D = x.shape
    tl = _pick(L, 512, 256, 128)
    full = lambda b, i: (0, 0)
    tok = pl.BlockSpec((1, tl, C_D), lambda b, i: (b, i, 0))
    return pl.pallas_call(
        _outproj1_kernel,
        grid=(B, L // tl),
        in_specs=[tok, tok,
                  pl.BlockSpec((1, tl, C_D), lambda b, i: (b, i, 1)),
                  pl.BlockSpec((1, C_D), full),
                  pl.BlockSpec((C_D, D), full),
                  pl.BlockSpec((1, tl, D), lambda b, i: (b, i, 0)),
                  pl.BlockSpec((1, 1, D), lambda b, i: (b, 0, 0))],
        out_specs=pl.BlockSpec((1, tl, D), lambda b, i: (b, i, 0)),
        out_shape=jax.ShapeDtypeStruct((B, L, D), F32),
        compiler_params=_cparams("parallel", "parallel"),
        name="outproj1",
    )(o_f, o_b, p, norm_g.reshape(1, C_D), w_out.astype(BF16), x, m2)


def _mods(cmat, w, b, nb):
    R = cmat.shape[0]
    pad = (-R) % 8
    m = _modvec(jnp.pad(cmat, ((0, pad), (0, 0))), w, b)[:R]
    m = jnp.broadcast_to(m, (nb, m.shape[1])) if R == 1 else m
    return [m[:, None, k * D_MODEL:(k + 1) * D_MODEL] for k in range(6)]


def kernel(x, c, ctx, c_ctx, mod_w, mod_b, norm1_g, norm2_g, final_g,
           ab_w_in, ab_w_out, hy_conv_w, hy_conv_b, hy_fw1, hy_fb1, hy_ff1,
           hy_fw2, hy_fb2, hy_ff2, hy_fw3, hy_bias,
           s5_lam_re, s5_lam_im, s5_log_step, s5_b_re, s5_b_im, s5_c_re, s5_c_im,
           s5_d, s5_glu_w, s5_glu_b,
           c_w_in, c_w_out, c_lower_bounds, c_norm_g,
           moe_wg, moe_bg, moe_we, moe_be, moe_w_gate, moe_w_up, moe_w_down):
    B, L, D = x.shape
    row = lambda t: t.reshape(1, -1)

    m = _mods(c, mod_w[0], mod_b[0], B)
    mc = _mods(c_ctx[None, :], mod_w[0], mod_b[0], B)
    w_in = ab_w_in[0]
    wt_hy = w_in[:, :3 * HY_D].T.astype(BF16)
    w_s5 = w_in[:, 3 * HY_D:].astype(BF16)
    g1 = row(norm1_g[0])
    pt_c, u_c = _inproj0(ctx, g1, 1.0 + mc[1], mc[0], wt_hy, w_s5)
    pt_l, u_l = _inproj0(x, g1, 1.0 + m[1], m[0], wt_hy, w_s5)

    ops = _s5_operators(s5_lam_re[0], s5_lam_im[0], s5_log_step[0], s5_b_re[0], s5_b_im[0],
                        s5_c_re[0], s5_c_im[0])
    ys_c, s_fin = _s5_pass(u_c, ops, jnp.zeros((2 * S5_GROUPS, B, 2 * S5_STATE), F32))
    ys_l, _ = _s5_pass(u_l, ops, s_fin)

    filt = (hy_fw1[0], hy_fb1[0], hy_ff1[0], hy_fw2[0], hy_fb2[0], hy_ff2[0], hy_fw3[0])
    hy_c = _hyena(pt_c, hy_conv_w[0], hy_conv_b[0], hy_bias[0], filt)
    hy_l = _hyena(pt_l, hy_conv_w[0], hy_conv_b[0], hy_bias[0], filt)

    x = _outproj0(hy_l, ys_l, u_l, s5_d[0], s5_glu_w[0], s5_glu_b[0], ab_w_out[0], x, m[2])
    ctx = _outproj0(hy_c, ys_c, u_c, s5_d[0], s5_glu_w[0], s5_glu_b[0], ab_w_out[0], ctx, mc[2])

    mp = _moe_params(moe_wg[0], moe_bg[0], moe_we[0], moe_be[0],
                     moe_w_gate[0], moe_w_up[0], moe_w_down[0])
    g2 = row(norm2_g[0])
    x = _moe(x, g2, 1.0 + m[4], m[3], m[5], *mp)
    ctx = _moe(ctx, g2, 1.0 + mc[4], mc[3], mc[5], *mp)

    m = _mods(c, mod_w[1], mod_b[1], B)
    mc = _mods(c_ctx[None, :], mod_w[1], mod_b[1], B)
    sm = jax.nn.softmax(c_lower_bounds.astype(F32), axis=1)
    lower = (jnp.cumsum(sm, axis=1) - sm[:, :1])[:, 1]
    g1 = row(norm1_g[1])
    w1 = c_w_in[0].astype(BF16)
    p_c = _inproj1(ctx, g1, 1.0 + mc[1], mc[0], w1)
    p_l = _inproj1(x, g1, 1.0 + m[1], m[0], w1)
    zeros = jnp.zeros((B, C_HEADS, C_HEAD_DIM, C_HEAD_DIM), F32)
    _, s_f = _gla(p_c, lower[0:1], zeros, False)
    _, s_b = _gla(p_c, lower[1:2], zeros, True)
    o_f, _ = _gla(p_l, lower[0:1], s_f, False)
    o_b, _ = _gla(p_l, lower[1:2], s_b, True)
    x = _outproj1(o_f, o_b, p_l, c_norm_g[0], c_w_out[0], x, m[2])

    mp = _moe_params(moe_wg[1], moe_bg[1], moe_we[1], moe_be[1],
                     moe_w_gate[1], moe_w_up[1], moe_w_down[1])
    return _moe(x, row(norm2_g[1]), 1.0 + m[4], m[3], m[5], *mp, final_g=final_g)
```

```python
import functools
import math

import numpy as np
import jax
import jax.numpy as jnp
from jax import lax
from jax.experimental import pallas as pl
from jax.experimental.pallas import tpu as pltpu

F32 = jnp.float32
BF16 = jnp.bfloat16
EPS = 1e-6
HIGHEST = lax.Precision.HIGHEST

D_MODEL = 1024
HY_D = 768
HY_ORDER = 2
HY_EMB = 33
HY_BANDS = (HY_EMB - 1) // 2
HY_DECAY_SHORT = 0.3
HY_DECAY_LONG = 1.5
HY_TARGET = 1e-2
S5_D = 256
S5_GROUP = 16
S5_GROUPS = S5_D // S5_GROUP
S5_STATE = 64
S5_P = S5_GROUPS * S5_STATE
S5_HALVES = S5_D // 128
S5_PH = S5_P // S5_HALVES
C_HEADS = 8
C_HEAD_DIM = 128
C_D = C_HEADS * C_HEAD_DIM
MOE_GROUPS = 4
MOE_EPG = 8
MOE_EXPERTS = MOE_GROUPS * MOE_EPG
MOE_HIDDEN = 256
MOE_UNIT = 128
MOE_ALIGN = 16

LANE = 128
SUBLANE = 8
DFT_BLK = 256
S5_CHUNK = 16
GLA_CHUNK = 64
VMEM_LIMIT = 56 * 1024 * 1024


def _cparams(*sem):
    return pltpu.CompilerParams(dimension_semantics=sem, vmem_limit_bytes=VMEM_LIMIT)


def _pick(n, *cands):
    for c in cands:
        if n % c == 0:
            return c
    return n


def _rms_mod(xv, g, sc, sh):
    ms = jnp.mean(xv * xv, axis=-1, keepdims=True)
    return xv * lax.rsqrt(ms + EPS) * g * sc + sh


def _sigmoid(v):
    return 1.0 / (1.0 + jnp.exp(-v))


def _bdot(a, b):
    return jnp.dot(a, b, preferred_element_type=F32)


_NT = (((1,), (1,)), ((), ()))
_TN = (((0,), (0,)), ((), ()))


def _mm_kernel(a_ref, b_ref, o_ref):
    o_ref[...] = _bdot(a_ref[...].astype(BF16), b_ref[...]).astype(o_ref.dtype)


def _mm(a, b, out_dtype=F32, name="mm"):
    M, K = a.shape
    N = b.shape[1]
    tm = _pick(M, 768, 512, 256, 128)
    tn = _pick(N, 512, 256, 128)
    return pl.pallas_call(
        _mm_kernel,
        grid=(M // tm, N // tn),
        in_specs=[pl.BlockSpec((tm, K), lambda i, j: (i, 0)),
                  pl.BlockSpec((K, tn), lambda i, j: (0, j))],
        out_specs=pl.BlockSpec((tm, tn), lambda i, j: (i, j)),
        out_shape=jax.ShapeDtypeStruct((M, N), out_dtype),
        compiler_params=_cparams("parallel", "parallel"),
        name=name,
    )(a, b.astype(BF16))


def _modvec_kernel(c_ref, w_ref, b_ref, o_ref):
    cv = c_ref[...]
    sc = cv * _sigmoid(cv)
    o_ref[...] = jnp.dot(sc, w_ref[...], preferred_element_type=F32, precision=HIGHEST) + b_ref[...]


def _modvec(cvec, w, b):
    R, D = cvec.shape
    N = w.shape[1]
    tn = _pick(N, 512, 256, 128)
    return pl.pallas_call(
        _modvec_kernel,
        grid=(N // tn,),
        in_specs=[pl.BlockSpec((R, D), lambda j: (0, 0)),
                  pl.BlockSpec((D, tn), lambda j: (0, j)),
                  pl.BlockSpec((1, tn), lambda j: (0, j))],
        out_specs=pl.BlockSpec((R, tn), lambda j: (0, j)),
        out_shape=jax.ShapeDtypeStruct((R, N), F32),
        compiler_params=_cparams("parallel"),
        name="modvec",
    )(cvec, w, b.reshape(1, N))


def _inproj0_kernel(x_ref, g_ref, sc_ref, sh_ref, wt_ref, w2_ref, pt_ref, ua_ref, ub_ref):
    h = _rms_mod(x_ref[0], g_ref[...], sc_ref[0], sh_ref[0]).astype(BF16)
    pt_ref[0] = lax.dot_general(wt_ref[...], h, _NT, preferred_element_type=F32)
    u = _bdot(h, w2_ref[...])
    ua_ref[0] = u[:, :LANE]
    ub_ref[0] = u[:, LANE:]


def _inproj0(x, g, sc, sh, wt_hy, w_s5):
    B, L, D = x.shape
    C = wt_hy.shape[0]
    tl = _pick(L, 512, 256, 128)
    return pl.pallas_call(
        _inproj0_kernel,
        grid=(B, L // tl),
        in_specs=[pl.BlockSpec((1, tl, D), lambda b, i: (b, i, 0)),
                  pl.BlockSpec((1, D), lambda b, i: (0, 0)),
                  pl.BlockSpec((1, 1, D), lambda b, i: (b, 0, 0)),
                  pl.BlockSpec((1, 1, D), lambda b, i: (b, 0, 0)),
                  pl.BlockSpec((C, D), lambda b, i: (0, 0)),
                  pl.BlockSpec((D, S5_D), lambda b, i: (0, 0))],
        out_specs=[pl.BlockSpec((1, C, tl), lambda b, i: (b, 0, i)),
                   pl.BlockSpec((1, tl, LANE), lambda b, i: (b, i, 0)),
                   pl.BlockSpec((1, tl, LANE), lambda b, i: (b, i, 0))],
        out_shape=[jax.ShapeDtypeStruct((B, C, L), F32),
                   jax.ShapeDtypeStruct((B, L, LANE), F32),
                   jax.ShapeDtypeStruct((B, L, LANE), F32)],
        compiler_params=_cparams("parallel", "parallel"),
        name="inproj0",
    )(x, g, sc, sh, wt_hy, w_s5)


def _shortconv_kernel(p1_ref, p2_ref, p3_ref, w_ref, b_ref, x1_ref, x2_ref, v_ref):
    L = p1_ref.shape[2]
    lane = lax.broadcasted_iota(jnp.int32, (1, L), 1)
    first = lane == 0
    last = lane == L - 1

    def conv(u, k):
        w = w_ref[k]
        prev = jnp.where(first, 0.0, pltpu.roll(u, 1, 1))
        nxt = jnp.where(last, 0.0, pltpu.roll(u, L - 1, 1))
        return w[:, 0:1] * prev + w[:, 1:2] * u + w[:, 2:3] * nxt + b_ref[k]

    x1_ref[0] = conv(p1_ref[0], 0)
    x2_ref[0] = conv(p2_ref[0], 1)
    v_ref[0] = conv(p3_ref[0], 2).astype(BF16)


def _shortconv(pt, conv_w, conv_b):
    B, C3, L = pt.shape
    tc = 128
    nb = HY_D // tc
    w = conv_w.T.reshape(3, HY_D, 3)
    b = conv_b.reshape(3, HY_D, 1)
    specs = [pl.BlockSpec((1, tc, L), (lambda b_, i, k=k: (b_, k * nb + i, 0))) for k in range(3)]
    o_spec = pl.BlockSpec((1, tc, L), lambda b_, i: (b_, i, 0))
    return pl.pallas_call(
        _shortconv_kernel,
        grid=(B, nb),
        in_specs=specs + [pl.BlockSpec((3, tc, 3), lambda b_, i: (0, i, 0)),
                          pl.BlockSpec((3, tc, 1), lambda b_, i: (0, i, 0))],
        out_specs=[o_spec, o_spec, o_spec],
        out_shape=[jax.ShapeDtypeStruct((B, HY_D, L), F32),
                   jax.ShapeDtypeStruct((B, HY_D, L), F32),
                   jax.ShapeDtypeStruct((B, HY_D, L), BF16)],
        compiler_params=_cparams("parallel", "parallel"),
        name="shortconv",
    )(pt, pt, pt, w, b)


def _filter_kernel(z_ref, w1_ref, b1_ref, f1_ref, w2_ref, b2_ref, f2_ref, w3_ref, dec_ref, o_ref):
    dot = functools.partial(jnp.dot, preferred_element_type=F32, precision=HIGHEST)
    z = z_ref[...]
    hdn = jnp.sin(f1_ref[...] * (dot(z, w1_ref[...]) + b1_ref[...]))
    hdn = jnp.sin(f2_ref[...] * (dot(hdn, w2_ref[...]) + b2_ref[...]))
    t = z[:, 0:1]
    o_ref[...] = dot(hdn, w3_ref[...]) * jnp.exp(-t * dec_ref[...])


def _hyena_filter(L, fw1, fb1, ff1, fw2, fb2, ff2, fw3):
    pos = jnp.arange(L, dtype=F32)
    t = pos / max(L - 1, 1)
    w = 2.0 * math.pi * pos / L
    bands = jnp.linspace(1e-4, HY_BANDS - 1, HY_BANDS, dtype=F32)
    ang = w[:, None] * bands[None, :]
    z = jnp.concatenate([t[:, None], jnp.cos(ang), -jnp.sin(ang)], axis=-1)
    z = jnp.pad(z, ((0, 0), (0, LANE - HY_EMB)))
    w1 = jnp.pad(fw1, ((0, LANE - HY_EMB), (0, 0)))
    deltas = jnp.abs(jnp.linspace(math.log(HY_TARGET) / HY_DECAY_LONG,
                                  math.log(HY_TARGET) / HY_DECAY_SHORT, HY_D, dtype=F32))
    dec = jnp.tile(deltas, HY_ORDER * 2).reshape(1, -1)
    nf = fw1.shape[1]
    No = fw3.shape[1]
    tl = _pick(L, 512, 256)
    full = lambda i: (0, 0)
    return pl.pallas_call(
        _filter_kernel,
        grid=(L // tl,),
        in_specs=[pl.BlockSpec((tl, LANE), lambda i: (i, 0)),
                  pl.BlockSpec((LANE, nf), full), pl.BlockSpec((1, nf), full), pl.BlockSpec((1, nf), full),
                  pl.BlockSpec((nf, nf), full), pl.BlockSpec((1, nf), full), pl.BlockSpec((1, nf), full),
                  pl.BlockSpec((nf, No), full), pl.BlockSpec((1, No), full)],
        out_specs=pl.BlockSpec((tl, No), lambda i: (i, 0)),
        out_shape=jax.ShapeDtypeStruct((L, No), F32),
        compiler_params=_cparams("parallel"),
        name="hyena_filter",
    )(z, w1, fb1.reshape(1, nf), ff1.reshape(1, nf), fw2, fb2.reshape(1, nf), ff2.reshape(1, nf), fw3, dec)


def _dft_mats(L):
    N = 2 * L
    W = 64
    j = jnp.arange(2 * L, dtype=jnp.int32)
    kk = (j // (2 * DFT_BLK)) * DFT_BLK + j % DFT_BLK
    is_im = (j // DFT_BLK) % 2 == 1
    nyq = is_im & (kk == 0)
    hi = jnp.arange(L // W, dtype=jnp.int32) * W
    lo = jnp.arange(W, dtype=jnp.int32)
    alt = (1 - 2 * (lo % 2)).astype(F32)
    th = 2.0 * math.pi / N
    ah = ((hi[:, None] * kk[None, :]) % N).astype(F32) * th
    al = ((lo[:, None] * kk[None, :]) % N).astype(F32) * th
    ch, sh, cl, sl = jnp.cos(ah), jnp.sin(ah), jnp.cos(al), jnp.sin(al)
    c = ch[:, None, :] * cl[None] - sh[:, None, :] * sl[None]
    s = sh[:, None, :] * cl[None] + ch[:, None, :] * sl[None]
    wf = jnp.where(nyq[None, None, :], alt[None, :, None], jnp.where(is_im[None, None, :], -s, c))
    ct = ch.T[:, :, None] * cl.T[:, None, :] - sh.T[:, :, None] * sl.T[:, None, :]
    st = sh.T[:, :, None] * cl.T[:, None, :] + ch.T[:, :, None] * sl.T[:, None, :]
    scale = jnp.where(kk == 0, 1.0 / N, 2.0 / N).astype(F32)[:, None, None]
    wi = jnp.where(nyq[:, None, None], alt[None, None, :], jnp.where(is_im[:, None, None], -st, ct)) * scale
    return wf.reshape(L, 2 * L).astype(BF16), wi.reshape(2 * L, L).astype(BF16)


def _filter_spectrum(h, wf, L):
    h4 = h.reshape(L, HY_ORDER, 2, HY_D)
    hf = h4[:, :, 0].transpose(1, 2, 0)
    hb = h4[:, :, 1].transpose(1, 2, 0)
    hb = hb * (jnp.arange(L) > 0).astype(F32)
    energy = jnp.sum(hf * hf, axis=-1) + jnp.sum(hb * hb, axis=-1)
    rows = jnp.concatenate([hf.reshape(-1, L), hb.reshape(-1, L)], axis=0)
    spec = _mm(rows, wf, name="filter_dft").reshape(2, HY_ORDER, HY_D, L // DFT_BLK, 2, DFT_BLK)
    re = spec[:, :, :, :, 0].reshape(2, HY_ORDER, HY_D, L)
    im = spec[:, :, :, :, 1].reshape(2, HY_ORDER, HY_D, L)
    nrm = lax.rsqrt(energy)[..., None]
    kr = (re[0] + re[1]) * nrm
    ki = (im[0] - im[1]) * nrm
    nyq = (im[0, :, :, 0] + im[1, :, :, 0]) * nrm[..., 0]
    ki = ki.at[:, :, 0].set(0.0)
    krp = kr.at[:, :, 0].set(nyq)
    return kr, ki, krp


def _dftfwd_kernel(v_ref, wf_ref, kr_ref, ki_ref, krp_ref, y_ref):
    acc = _bdot(v_ref[0], wf_ref[...])
    vr = acc[:, :DFT_BLK]
    vi = acc[:, DFT_BLK:]
    kr = kr_ref[...]
    ki = ki_ref[...]
    yr = vr * kr - vi * ki
    yi = vr * ki + vi * krp_ref[...]
    y_ref[0] = jnp.concatenate([yr, yi], axis=1).astype(BF16)


def _dftfwd(v, wf, kr, ki, krp):
    B, C, L = v.shape
    nb = L // DFT_BLK
    kspec = pl.BlockSpec((C, DFT_BLK), lambda b, j: (0, j))
    return pl.pallas_call(
        _dftfwd_kernel,
        grid=(B, nb),
        in_specs=[pl.BlockSpec((1, C, L), lambda b, j: (b, 0, 0)),
                  pl.BlockSpec((L, 2 * DFT_BLK), lambda b, j: (0, j)),
                  kspec, kspec, kspec],
        out_specs=pl.BlockSpec((1, C, 2 * DFT_BLK), lambda b, j: (b, 0, j)),
        out_shape=jax.ShapeDtypeStruct((B, C, 2 * L), BF16),
        compiler_params=_cparams("parallel", "arbitrary"),
        name="hyena_dft_fwd",
    )(v, wf, kr, ki, krp)


def _dftinv_kernel(y_ref, wi_ref, xg_ref, v_ref, bias_ref, o_ref):
    conv = _bdot(y_ref[0], wi_ref[...])
    o_ref[0] = (xg_ref[0] * (conv + v_ref[0].astype(F32) * bias_ref[...])).astype(o_ref.dtype)


def _dftinv(y, wi, xg, v, bias, out_dtype):
    B, C, L2 = y.shape
    L = L2 // 2
    tn = 256
    tspec = pl.BlockSpec((1, C, tn), lambda b, j: (b, 0, j))
    return pl.pallas_call(
        _dftinv_kernel,
        grid=(B, L // tn),
        in_specs=[pl.BlockSpec((1, C, L2), lambda b, j: (b, 0, 0)),
                  pl.BlockSpec((L2, tn), lambda b, j: (0, j)),
                  tspec, tspec,
                  pl.BlockSpec((C, 1), lambda b, j: (0, 0))],
        out_specs=tspec,
        out_shape=jax.ShapeDtypeStruct((B, C, L), out_dtype),
        compiler_params=_cparams("parallel", "arbitrary"),
        name="hyena_dft_inv",
    )(y, wi, xg, v, bias)


def _hyena(pt, conv_w, conv_b, hy_bias, filt):
    L = pt.shape[2]
    wf, wi = _dft_mats(L)
    kr, ki, krp = _filter_spectrum(_hyena_filter(L, *filt), wf, L)
    x1, x2, v = _shortconv(pt, conv_w, conv_b)
    y1 = _dftfwd(v, wf, kr[0], ki[0], krp[0])
    z = _dftinv(y1, wi, x1, v, hy_bias[0].reshape(HY_D, 1), BF16)
    y2 = _dftfwd(z, wf, kr[1], ki[1], krp[1])
    return _dftinv(y2, wi, x2, z, hy_bias[1].reshape(HY_D, 1), BF16)


def _s5_operators(lam_re, lam_im, log_step, b_re, b_im, c_re, c_im):
    lam = lax.complex(lam_re.astype(F32), lam_im.astype(F32))
    dt = jnp.exp(log_step.astype(F32))[..., None]
    lam_bar = jnp.exp(lam * dt)
    lam_t = jnp.exp(lam * dt * S5_CHUNK)
    b_bar = ((lam_bar - 1.0) / lam)[..., None] * lax.complex(b_re.astype(F32), b_im.astype(F32))
    c_mat = lax.complex(c_re.astype(F32), c_im.astype(F32))
    gh = S5_GROUPS // S5_HALVES
    eye = jnp.eye(gh, dtype=F32)

    def bd_in(t):
        t = t.reshape(2, S5_HALVES, gh, S5_STATE, S5_GROUP)
        return jnp.einsum('dfgpn,gh->dfgnhp', t, eye).reshape(2, S5_HALVES, LANE, S5_PH)

    def bd_out(t):
        t = t.reshape(2, S5_HALVES, gh, S5_GROUP, S5_STATE)
        return jnp.einsum('dfgnp,gh->dfgphn', t, eye).reshape(2, S5_HALVES, S5_PH, LANE)

    bbd = jnp.stack([bd_in(jnp.real(b_bar)), bd_in(jnp.imag(b_bar))], axis=1).astype(BF16)
    cbd = jnp.stack([bd_out(jnp.real(c_mat)), bd_out(-jnp.imag(c_mat))], axis=1).astype(BF16)
    flat = lambda z: jnp.stack([jnp.real(z), jnp.imag(z)], axis=1).reshape(2, 2, S5_P)
    return bbd, cbd, flat(lam_bar), flat(lam_t)


def _s5_kernel(ua_ref, ub_ref, s0_ref, bbd_ref, cbd_ref, lam_ref, lamt_ref, ya_ref, yb_ref, sfin_ref,
               sloc_scr, sinit_scr):
    T = S5_CHUNK
    PH = S5_PH
    M = ua_ref.shape[1] // T
    rid = lax.broadcasted_iota(jnp.int32, (SUBLANE, PH), 0)

    for hf, (u_ref, y_ref) in enumerate(((ua_ref, ya_ref), (ub_ref, yb_ref))):
        re_sl = slice(hf * PH, (hf + 1) * PH)
        im_sl = slice(S5_P + hf * PH, S5_P + (hf + 1) * PH)
        for d in (0, 1):
            lr = lam_ref[d, 0:1, re_sl]
            li = lam_ref[d, 1:2, re_sl]
            steps = list(range(T)) if d == 0 else list(range(T - 1, -1, -1))

            def drive(s):
                us = u_ref[0, pl.ds(s, M, stride=T), :].astype(BF16)
                return _bdot(us, bbd_ref[d, 0, hf]), _bdot(us, bbd_ref[d, 1, hf])

            sr, si = drive(steps[0])
            for s in steps[1:]:
                br, bi = drive(s)
                sr, si = lr * sr - li * si + br, lr * si + li * sr + bi
            sloc_scr[:, :PH] = sr
            sloc_scr[:, PH:] = si

            ar = lamt_ref[d, 0:1, re_sl]
            ai = lamt_ref[d, 1:2, re_sl]
            nt = M // SUBLANE

            def scan_tile(n, carry):
                cr, ci = carry
                base = pl.multiple_of((n if d == 0 else nt - 1 - n) * SUBLANE, SUBLANE)
                lr_t = sloc_scr[pl.ds(base, SUBLANE), :PH]
                li_t = sloc_scr[pl.ds(base, SUBLANE), PH:]
                out_r = jnp.zeros((SUBLANE, PH), F32)
                out_i = jnp.zeros((SUBLANE, PH), F32)
                for r in (range(SUBLANE) if d == 0 else range(SUBLANE - 1, -1, -1)):
                    out_r = jnp.where(rid == r, cr, out_r)
                    out_i = jnp.where(rid == r, ci, out_i)
                    cr, ci = (ar * cr - ai * ci + lr_t[r:r + 1], ar * ci + ai * cr + li_t[r:r + 1])
                sinit_scr[pl.ds(base, SUBLANE), :PH] = out_r
                sinit_scr[pl.ds(base, SUBLANE), PH:] = out_i
                return cr, ci

            cr, ci = lax.fori_loop(0, nt, scan_tile, (s0_ref[0, d:d + 1, re_sl], s0_ref[0, d:d + 1, im_sl]))
            sfin_ref[0, d:d + 1, re_sl] = cr
            sfin_ref[0, d:d + 1, im_sl] = ci

            sr = sinit_scr[:, :PH]
            si = sinit_scr[:, PH:]
            for s in steps:
                br, bi = drive(s)
                sr, si = lr * sr - li * si + br, lr * si + li * sr + bi
                ys = (_bdot(sr.astype(BF16), cbd_ref[d, 0, hf]) + _bdot(si.astype(BF16), cbd_ref[d, 1, hf]))
                if d == 0:
                    y_ref[0, pl.ds(s, M, stride=T), :] = ys
                else:
                    y_ref[0, pl.ds(s, M, stride=T), :] += ys


def _s5_pass(ua, ub, ops, s0):
    bbd, cbd, lam, lam_t = ops
    B, L, _ = ua.shape
    M = L // S5_CHUNK
    full = lambda nd: (lambda b: (0,) * nd)
    tok = pl.BlockSpec((1, L, LANE), lambda b: (b, 0, 0))
    st = pl.BlockSpec((1, 2, 2 * S5_P), lambda b: (b, 0, 0))
    return pl.pallas_call(
        _s5_kernel,
        grid=(B,),
        in_specs=[tok, tok, st,
                  pl.BlockSpec(bbd.shape, full(5)),
                  pl.BlockSpec(cbd.shape, full(5)),
                  pl.BlockSpec((2, 2, S5_P), full(3)),
                  pl.BlockSpec((2, 2, S5_P), full(3))],
        out_specs=[tok, tok, st],
        out_shape=[jax.ShapeDtypeStruct((B, L, LANE), F32),
                   jax.ShapeDtypeStruct((B, L, LANE), F32),
                   jax.ShapeDtypeStruct((B, 2, 2 * S5_P), F32)],
        scratch_shapes=[pltpu.VMEM((M, 2 * S5_PH), F32), pltpu.VMEM((M, 2 * S5_PH), F32)],
        compiler_params=_cparams("parallel"),
        name="s5",
    )(ua, ub, s0, bbd, cbd, lam, lam_t)


def _gelu_tanh(v):
    return 0.5 * v * (1.0 + jnp.tanh(math.sqrt(2.0 / math.pi) * (v + 0.044715 * v * v * v)))


def _outproj0_kernel(hy_ref, ya_ref, yb_ref, ua_ref, ub_ref, d_ref, gw_ref, gb_ref, wa_ref, wb_ref,
                     x_ref, m_ref, o_ref):
    ys = (jnp.concatenate([ya_ref[0], yb_ref[0]], axis=1)
          + d_ref[...] * jnp.concatenate([ua_ref[0], ub_ref[0]], axis=1))
    glu = _bdot(_gelu_tanh(ys).astype(BF16), gw_ref[...]) + gb_ref[...]
    s5 = glu[:, :S5_D] * _sigmoid(glu[:, S5_D:])
    y = lax.dot_general(hy_ref[0], wa_ref[...], _TN, preferred_element_type=F32)
    y = y + _bdot(s5.astype(BF16), wb_ref[...])
    o_ref[0] = x_ref[0] + m_ref[0] * y


def _outproj0(hy, ys, u, d, glu_w, glu_b, w_out, x, m2):
    B, L, D = x.shape
    tl = _pick(L, 512, 256, 128)
    full = lambda b, i: (0, 0)
    tok = lambda w: pl.BlockSpec((1, tl, w), lambda b, i: (b, i, 0))
    return pl.pallas_call(
        _outproj0_kernel,
        grid=(B, L // tl),
        in_specs=[pl.BlockSpec((1, HY_D, tl), lambda b, i: (b, 0, i)),
                  tok(LANE), tok(LANE), tok(LANE), tok(LANE),
                  pl.BlockSpec((1, S5_D), full),
                  pl.BlockSpec((S5_D, 2 * S5_D), full),
                  pl.BlockSpec((1, 2 * S5_D), full),
                  pl.BlockSpec((HY_D, D), full),
                  pl.BlockSpec((S5_D, D), full),
                  tok(D),
                  pl.BlockSpec((1, 1, D), lambda b, i: (b, 0, 0))],
        out_specs=tok(D),
        out_shape=jax.ShapeDtypeStruct((B, L, D), F32),
        compiler_params=_cparams("parallel", "parallel"),
        name="outproj0",
    )(hy, ys[0], ys[1], u[0], u[1], d.reshape(1, S5_D), glu_w.astype(BF16), glu_b.reshape(1, -1),
      w_out[:HY_D].astype(BF16), w_out[HY_D:].astype(BF16), x, m2)


def _moe_kernel(x_ref, g_ref, sc_ref, sh_ref, m_ref, wrh_ref, wrl_ref, br_ref, tril_ref,
                wg_ref, wu_ref, wd_ref, fg_ref, o_ref, hs_scr, gs_scr, pt_scr, ys_scr, seg_smem,
                *, ne, final_norm):
    j = pl.program_id(2)
    nj = pl.num_programs(2)
    tl = x_ref.shape[1]
    S = hs_scr.shape[0]

    @pl.when(j == 0)
    def _():
        hn = _rms_mod(x_ref[0], g_ref[...], sc_ref[0], sh_ref[0])
        hi = hn.astype(BF16)
        lo = (hn - hi.astype(F32)).astype(BF16)
        wrh = wrh_ref[...]
        logits = _bdot(hi, wrh) + _bdot(lo, wrh) + _bdot(hi, wrl_ref[...]) + br_ref[...]
        lane = lax.broadcasted_iota(jnp.int32, logits.shape, 1)
        neg = -jnp.inf
        lgm = jnp.where(lane < MOE_GROUPS, logits, neg)
        gmax = jnp.max(lgm, axis=1, keepdims=True)
        p_top = 1.0 / jnp.sum(jnp.exp(lgm - gmax), axis=1, keepdims=True)
        gidx = jnp.min(jnp.where(lgm == gmax, lane, LANE), axis=1, keepdims=True)
        elane = lane - MOE_GROUPS
        in_group = (elane >= gidx * MOE_EPG) & (elane < (gidx + 1) * MOE_EPG)
        lem = jnp.where(in_group, logits, neg)
        v1 = jnp.max(lem, axis=1, keepdims=True)
        i1 = jnp.min(jnp.where(lem == v1, lane, LANE), axis=1, keepdims=True)
        lem2 = jnp.where(lane == i1, neg, lem)
        v2 = jnp.max(lem2, axis=1, keepdims=True)
        i2 = jnp.min(jnp.where(lem2 == v2, lane, LANE), axis=1, keepdims=True)
        e2 = jnp.exp(v2 - v1)
        w1 = p_top / (1.0 + e2)
        gate = jnp.where(lane == i1, w1, jnp.where(lane == i2, w1 * e2, 0.0))
        onehot = (lane == gidx).astype(F32)
        rank = _bdot(tril_ref[...], onehot.astype(BF16))
        off = jnp.int32(0)
        dest = jnp.zeros((tl, 1), F32)
        for g in range(MOE_GROUPS):
            col = onehot[:, g:g + 1]
            n_g = jnp.sum(col).astype(jnp.int32)
            seg_smem[g] = off
            seg_smem[MOE_GROUPS + g] = n_g
            dest = dest + col * (off.astype(F32) + rank[:, g:g + 1])
            off = off + ((n_g + MOE_ALIGN - 1) // MOE_ALIGN) * MOE_ALIGN
        slot = lax.broadcasted_iota(jnp.int32, (tl, S), 1)
        pt = (slot == dest.astype(jnp.int32)).astype(BF16)
        pt_scr[...] = pt
        hs_scr[...] = lax.dot_general(pt, hi, _TN, preferred_element_type=F32).astype(BF16)
        ghi = gate.astype(BF16)
        glo = (gate - ghi.astype(F32)).astype(BF16)
        gs_scr[...] = (lax.dot_general(pt, ghi, _TN, preferred_element_type=F32)
                       + lax.dot_general(pt, glo, _TN, preferred_element_type=F32))
        ys_scr[...] = jnp.zeros_like(ys_scr)

    grp = j // (MOE_EPG // ne)
    start = pl.multiple_of(seg_smem[grp], MOE_ALIGN)
    units = (seg_smem[MOE_GROUPS + grp] + MOE_UNIT - 1) // MOE_UNIT
    lane1 = lax.broadcasted_iota(jnp.int32, (1, LANE), 1)

    def block(r0, rows):
        hs = hs_scr[pl.ds(r0, rows), :]
        gsb = gs_scr[pl.ds(r0, rows), :]
        acc = None
        for e in range(ne):
            a = _bdot(hs, wg_ref[e])
            u = _bdot(hs, wu_ref[e])
            gcol = jnp.sum(jnp.where(lane1 == MOE_GROUPS + j * ne + e, gsb, 0.0), axis=1, keepdims=True)
            y = _bdot((a * _sigmoid(a) * u * gcol).astype(BF16), wd_ref[e])
            acc = y if acc is None else acc + y
        ys_scr[pl.ds(r0, rows), :] += acc

    @pl.when(units <= 2)
    def _():
        block(start, 2 * MOE_UNIT)

    @pl.when(units == 3)
    def _():
        block(start, 3 * MOE_UNIT)

    @pl.when(units >= 4)
    def _():
        def body(i, carry):
            block(pl.multiple_of(start + i * (2 * MOE_UNIT), MOE_ALIGN), 2 * MOE_UNIT)
            return carry

        lax.fori_loop(0, (units + 1) // 2, body, 0)

    @pl.when(j == nj - 1)
    def _():
        moe = _bdot(pt_scr[...], ys_scr[...].astype(BF16))
        out = x_ref[0] + m_ref[0] * moe
        if final_norm:
            ms = jnp.mean(out * out, axis=-1, keepdims=True)
            out = out * lax.rsqrt(ms + EPS) * fg_ref[...]
        o_ref[0] = out


def _moe(x, g2, sc, sh, m5, wrh, wrl, br, wg, wu, wd, final_g=None):
    B, L, D = x.shape
    ne = 4
    tl = _pick(L, 1024, 512, 256, 128)
    S = tl + 3 * MOE_UNIT
    final_norm = final_g is not None
    fg = (final_g if final_norm else jnp.ones((D,), F32)).reshape(1, D)
    idx = jnp.arange(tl, dtype=jnp.int32)
    tril = (idx[None, :] < idx[:, None]).astype(BF16)
    full = lambda b, i, j: (0, 0)
    tok = pl.BlockSpec((1, tl, D), lambda b, i, j: (b, i, 0))
    vec = pl.BlockSpec((1, 1, D), lambda b, i, j: (b, 0, 0))
    wspec = lambda k, n: pl.BlockSpec((ne, k, n), lambda b, i, j: (j, 0, 0))
    return pl.pallas_call(
        functools.partial(_moe_kernel, ne=ne, final_norm=final_norm),
        grid=(B, L // tl, MOE_EXPERTS // ne),
        in_specs=[tok, pl.BlockSpec((1, D), full), vec, vec, vec,
                  pl.BlockSpec((D, LANE), full), pl.BlockSpec((D, LANE), full), pl.BlockSpec((1, LANE), full),
                  pl.BlockSpec((tl, tl), full),
                  wspec(D, MOE_HIDDEN), wspec(D, MOE_HIDDEN), wspec(MOE_HIDDEN, D),
                  pl.BlockSpec((1, D), full)],
        out_specs=tok,
        out_shape=jax.ShapeDtypeStruct((B, L, D), F32),
        scratch_shapes=[pltpu.VMEM((S, D), BF16),
                        pltpu.VMEM((S, LANE), F32),
                        pltpu.VMEM((tl, S), BF16),
                        pltpu.VMEM((S, D), F32),
                        pltpu.SMEM((2 * MOE_GROUPS,), jnp.int32)],
        compiler_params=_cparams("parallel", "parallel", "arbitrary"),
        name="moe",
    )(x, g2, sc, sh, m5, wrh, wrl, br, tril, wg, wu, wd, fg)


def _moe_params(wg, bg, we, be, w_gate, w_up, w_down):
    D = wg.shape[0]
    pad = LANE - MOE_GROUPS - MOE_EXPERTS
    wr = jnp.pad(jnp.concatenate([wg, we], axis=1), ((0, 0), (0, pad)))
    br = jnp.pad(jnp.concatenate([bg, be]), (0, pad)).reshape(1, LANE)
    wrh = wr.astype(BF16)
    wrl = (wr - wrh.astype(F32)).astype(BF16)
    return (wrh, wrl, br,
            w_gate.reshape(MOE_EXPERTS, D, MOE_HIDDEN).astype(BF16),
            w_up.reshape(MOE_EXPERTS, D, MOE_HIDDEN).astype(BF16),
            w_down.reshape(MOE_EXPERTS, MOE_HIDDEN, D).astype(BF16))


def _inproj1_kernel(x_ref, g_ref, sc_ref, sh_ref, w_ref, o_ref, h_scr):
    @pl.when(pl.program_id(2) == 0)
    def _():
        h_scr[...] = _rms_mod(x_ref[0], g_ref[...], sc_ref[0], sh_ref[0]).astype(BF16)

    o_ref[0] = _bdot(h_scr[...], w_ref[...]).astype(o_ref.dtype)


def _inproj1(x, g, sc, sh, w):
    B, L, D = x.shape
    N = w.shape[1]
    tl = _pick(L, 512, 256, 128)
    tn = _pick(N, 1280, 1024, 512)
    return pl.pallas_call(
        _inproj1_kernel,
        grid=(B, L // tl, N // tn),
        in_specs=[pl.BlockSpec((1, tl, D), lambda b, i, j: (b, i, 0)),
                  pl.BlockSpec((1, D), lambda b, i, j: (0, 0)),
                  pl.BlockSpec((1, 1, D), lambda b, i, j: (b, 0, 0)),
                  pl.BlockSpec((1, 1, D), lambda b, i, j: (b, 0, 0)),
                  pl.BlockSpec((D, tn), lambda b, i, j: (0, j))],
        out_specs=pl.BlockSpec((1, tl, tn), lambda b, i, j: (b, i, j)),
        out_shape=jax.ShapeDtypeStruct((B, L, N), BF16),
        scratch_shapes=[pltpu.VMEM((tl, D), BF16)],
        compiler_params=_cparams("parallel", "parallel", "arbitrary"),
        name="inproj1",
    )(x, g, sc, sh, w)


def _gla_masks(C):
    t = np.arange(C)[:, None]
    s = np.arange(C)[None, :]
    fwd = []
    hs = 1
    while hs < C:
        fwd.append(((t // (2 * hs)) == (s // (2 * hs))) & ((t & hs) != 0) & ((s & hs) == 0))
        hs *= 2
    fwd = np.stack(fwd).astype(np.float32)
    return jnp.asarray(np.stack([fwd, fwd.transpose(0, 2, 1)])), jnp.asarray(np.eye(C, dtype=np.float32))


def _gla_chain(pq, v, z, lb, masks, eye, st, rev):
    C = pq.shape[0]
    q = pq * _sigmoid(pq)
    f = lb + (1.0 - lb) * _sigmoid(z)
    k = 1.0 - f
    g = jnp.log(f)
    row = lax.broadcasted_iota(jnp.int32, (C, LANE), 0)
    att = eye * lax.dot_general(q.astype(BF16), k.astype(BF16), _NT, preferred_element_type=F32)
    pinc = g
    tot = g
    hs = 1
    lvl = 0
    while hs < C:
        if rev:
            aq = tot - pinc + g
            ak = pinc - g
        else:
            aq = pinc
            ak = tot - pinc
        blk = lax.dot_general((q * jnp.exp(aq)).astype(BF16), (k * jnp.exp(ak)).astype(BF16), _NT,
                              preferred_element_type=F32)
        att = att + masks[lvl] * blk
        odd = (row & hs) != 0
        tprev = pltpu.roll(tot, hs, 0)
        tnext = pltpu.roll(tot, C - hs, 0)
        pinc = pinc + jnp.where(odd, tprev, 0.0)
        tot = tot + jnp.where(odd, tprev, tnext)
        hs *= 2
        lvl += 1
    if rev:
        q_dec = tot - pinc + g
        k_dec = pinc - g
    else:
        q_dec = pinc
        k_dec = tot - pinc
    vb = v.astype(BF16)
    o = _bdot(att.astype(BF16), vb)
    o = o + lax.dot_general((q * jnp.exp(q_dec)).astype(BF16), st.astype(BF16), _NT,
                            preferred_element_type=F32)
    upd = lax.dot_general(vb, (k * jnp.exp(k_dec)).astype(BF16), _TN, preferred_element_type=F32)
    return o, st * jnp.exp(tot[0:1, :]) + upd


def _gla_kernel(qf_ref, if_ref, zf_ref, qb_ref, ib_ref, zb_ref, lb_ref, mk_ref, eye_ref, s0_ref,
                of_ref, ob_ref, sfin_ref, st_scr):
    c = pl.program_id(1)

    @pl.when(c == 0)
    def _():
        st_scr[...] = s0_ref[0]

    eye = eye_ref[...]
    nlev = mk_ref.shape[1]
    for d, (q_ref, i_ref, z_ref, o_ref) in enumerate(((qf_ref, if_ref, zf_ref, of_ref),
                                                      (qb_ref, ib_ref, zb_ref, ob_ref))):
        masks = [mk_ref[d, lv] for lv in range(nlev)]
        for h in range(C_HEADS):
            sl = slice(h * LANE, (h + 1) * LANE)
            o, st = _gla_chain(q_ref[0, :, sl].astype(F32), i_ref[0, :, sl].astype(F32),
                               z_ref[0, :, sl].astype(F32), lb_ref[d:d + 1, sl], masks, eye,
                               st_scr[d, h], rev=(d == 1))
            o_ref[0, :, sl] = o.astype(o_ref.dtype)
            st_scr[d, h] = st
            sfin_ref[0, d, h] = st


def _gla(p, lb, s0):
    B, L, _ = p.shape
    C = GLA_CHUNK
    nch = L // C
    masks, eye = _gla_masks(C)
    fcol = lambda k: pl.BlockSpec((1, C, C_D), lambda b, c: (b, c, k))
    bcol = lambda k: pl.BlockSpec((1, C, C_D), lambda b, c: (b, nch - 1 - c, k))
    st_spec = pl.BlockSpec((1, 2, C_HEADS, LANE, LANE), lambda b, c: (b, 0, 0, 0, 0))
    return pl.pallas_call(
        _gla_kernel,
        grid=(B, nch),
        in_specs=[fcol(0), fcol(2), fcol(3), bcol(0), bcol(2), bcol(4),
                  pl.BlockSpec((2, C_D), lambda b, c: (0, 0)),
                  pl.BlockSpec(masks.shape, lambda b, c: (0, 0, 0, 0)),
                  pl.BlockSpec((C, C), lambda b, c: (0, 0)),
                  st_spec],
        out_specs=[fcol(0), bcol(0), st_spec],
        out_shape=[jax.ShapeDtypeStruct((B, L, C_D), BF16),
                   jax.ShapeDtypeStruct((B, L, C_D), BF16),
                   jax.ShapeDtypeStruct((B, 2, C_HEADS, LANE, LANE), F32)],
        scratch_shapes=[pltpu.VMEM((2, C_HEADS, LANE, LANE), F32)],
        compiler_params=_cparams("parallel", "arbitrary"),
        name="gla",
    )(p, p, p, p, p, p, lb, masks, eye, s0)


def _outproj1_kernel(of_ref, ob_ref, g_ref, ng_ref, w_ref, x_ref, m_ref, o_ref):
    o = of_ref[0].astype(F32) + ob_ref[0].astype(F32)
    parts = []
    for h in range(C_HEADS):
        oh = o[:, h * LANE:(h + 1) * LANE]
        ms = jnp.mean(oh * oh, axis=-1, keepdims=True)
        parts.append(oh * lax.rsqrt(ms + EPS))
    on = jnp.concatenate(parts, axis=1) * ng_ref[...] * _sigmoid(g_ref[0].astype(F32))
    y = _bdot(on.astype(BF16), w_ref[...])
    o_ref[0] = x_ref[0] + m_ref[0] * y


def _outproj1(o_f, o_b, p, norm_g, w_out, x, m2):
    B, L, D = x.shape
    tl = _pick(L, 512, 256, 128)
    full = lambda b, i: (0, 0)
    tok = pl.BlockSpec((1, tl, C_D), lambda b, i: (b, i, 0))
    return pl.pallas_call(
        _outproj1_kernel,
        grid=(B, L // tl),
        in_specs=[tok, tok,
                  pl.BlockSpec((1, tl, C_D), lambda b, i: (b, i, 1)),
                  pl.BlockSpec((1, C_D), full),
                  pl.BlockSpec((C_D, D), full),
                  pl.BlockSpec((1, tl, D), lambda b, i: (b, i, 0)),
                  pl.BlockSpec((1, 1, D), lambda b, i: (b, 0, 0))],
        out_specs=pl.BlockSpec((1, tl, D), lambda b, i: (b, i, 0)),
        out_shape=jax.ShapeDtypeStruct((B, L, D), F32),
        compiler_params=_cparams("parallel", "parallel"),
        name="outproj1",
    )(o_f, o_b, p, norm_g.reshape(1, C_D), w_out.astype(BF16), x, m2)


def _mods(cmat, w, b, nb):
    R = cmat.shape[0]
    pad = (-R) % SUBLANE
    m = _modvec(jnp.pad(cmat, ((0, pad), (0, 0))), w, b)[:R]
    m = jnp.broadcast_to(m, (nb, m.shape[1])) if R == 1 else m
    return [m[:, None, k * D_MODEL:(k + 1) * D_MODEL] for k in range(6)]


def kernel(x, c, ctx, c_ctx, mod_w, mod_b, norm1_g, norm2_g, final_g,
           ab_w_in, ab_w_out, hy_conv_w, hy_conv_b, hy_fw1, hy_fb1, hy_ff1,
           hy_fw2, hy_fb2, hy_ff2, hy_fw3, hy_bias,
           s5_lam_re, s5_lam_im, s5_log_step, s5_b_re, s5_b_im, s5_c_re, s5_c_im,
           s5_d, s5_glu_w, s5_glu_b,
           c_w_in, c_w_out, c_lower_bounds, c_norm_g,
           moe_wg, moe_bg, moe_we, moe_be, moe_w_gate, moe_w_up, moe_w_down):
    B, L, D = x.shape
    row = lambda t: t.reshape(1, -1)

    m = _mods(c, mod_w[0], mod_b[0], B)
    mc = _mods(c_ctx[None, :], mod_w[0], mod_b[0], B)
    w_in = ab_w_in[0]
    wt_hy = w_in[:, :3 * HY_D].T.astype(BF16)
    w_s5 = w_in[:, 3 * HY_D:].astype(BF16)
    g1 = row(norm1_g[0])
    pt_c, *u_c = _inproj0(ctx, g1, 1.0 + mc[1], mc[0], wt_hy, w_s5)
    pt_l, *u_l = _inproj0(x, g1, 1.0 + m[1], m[0], wt_hy, w_s5)

    ops = _s5_operators(s5_lam_re[0], s5_lam_im[0], s5_log_step[0], s5_b_re[0], s5_b_im[0],
                        s5_c_re[0], s5_c_im[0])
    *ys_c, s_fin = _s5_pass(*u_c, ops, jnp.zeros((B, 2, 2 * S5_P), F32))
    *ys_l, _ = _s5_pass(*u_l, ops, s_fin)

    filt = (hy_fw1[0], hy_fb1[0], hy_ff1[0], hy_fw2[0], hy_fb2[0], hy_ff2[0], hy_fw3[0])
    hy_c = _hyena(pt_c, hy_conv_w[0], hy_conv_b[0], hy_bias[0], filt)
    hy_l = _hyena(pt_l, hy_conv_w[0], hy_conv_b[0], hy_bias[0], filt)

    x = _outproj0(hy_l, ys_l, u_l, s5_d[0], s5_glu_w[0], s5_glu_b[0], ab_w_out[0], x, m[2])
    ctx = _outproj0(hy_c, ys_c, u_c, s5_d[0], s5_glu_w[0], s5_glu_b[0], ab_w_out[0], ctx, mc[2])

    mp = _moe_params(moe_wg[0], moe_bg[0], moe_we[0], moe_be[0],
                     moe_w_gate[0], moe_w_up[0], moe_w_down[0])
    g2 = row(norm2_g[0])
    x = _moe(x, g2, 1.0 + m[4], m[3], m[5], *mp)
    ctx = _moe(ctx, g2, 1.0 + mc[4], mc[3], mc[5], *mp)

    m = _mods(c, mod_w[1], mod_b[1], B)
    mc = _mods(c_ctx[None, :], mod_w[1], mod_b[1], B)
    sm = jax.nn.softmax(c_lower_bounds.astype(F32), axis=1)
    lower = (jnp.cumsum(sm, axis=1) - sm[:, :1])[:, 1]
    g1 = row(norm1_g[1])
    w1 = c_w_in[0].astype(BF16)
    p_c = _inproj1(ctx, g1, 1.0 + mc[1], mc[0], w1)
    p_l = _inproj1(x, g1, 1.0 + m[1], m[0], w1)
    zeros = jnp.zeros((B, 2, C_HEADS, C_HEAD_DIM, C_HEAD_DIM), F32)
    _, _, s_ctx = _gla(p_c, lower, zeros)
    o_f, o_b, _ = _gla(p_l, lower, s_ctx)
    x = _outproj1(o_f, o_b, p_l, c_norm_g[0], c_w_out[0], x, m[2])

    mp = _moe_params(moe_wg[1], moe_bg[1], moe_we[1], moe_be[1],
                     moe_w_gate[1], moe_w_up[1], moe_w_down[1])
    return _moe(x, row(norm2_g[1]), 1.0 + m[4], m[3], m[5], *mp, final_g=final_g)
```

```python
import functools
import math

import numpy as np
import jax
import jax.numpy as jnp
from jax import lax
from jax.experimental import pallas as pl
from jax.experimental.pallas import tpu as pltpu

F32 = jnp.float32
BF16 = jnp.bfloat16
EPS = 1e-6
HIGHEST = lax.Precision.HIGHEST

D_MODEL = 1024
HY_D = 768
HY_ORDER = 2
HY_EMB = 33
HY_BANDS = (HY_EMB - 1) // 2
HY_DECAY_SHORT = 0.3
HY_DECAY_LONG = 1.5
HY_TARGET = 1e-2
S5_D = 256
S5_GROUP = 16
S5_GROUPS = S5_D // S5_GROUP
S5_STATE = 64
S5_P = S5_GROUPS * S5_STATE
S5_HALVES = S5_D // 128
S5_PH = S5_P // S5_HALVES
C_HEADS = 8
C_HEAD_DIM = 128
C_D = C_HEADS * C_HEAD_DIM
MOE_GROUPS = 4
MOE_EPG = 8
MOE_EXPERTS = MOE_GROUPS * MOE_EPG
MOE_HIDDEN = 256
MOE_UNIT = 128
MOE_ALIGN = 16

LANE = 128
SUBLANE = 8
DFT_BLK = 256
S5_CHUNK = 16
GLA_CHUNK = 64
VMEM_LIMIT = 56 * 1024 * 1024


def _cparams(*sem):
    return pltpu.CompilerParams(dimension_semantics=sem, vmem_limit_bytes=VMEM_LIMIT)


def _pick(n, *cands):
    for c in cands:
        if n % c == 0:
            return c
    return n


def _rms_mod(xv, g, sc, sh):
    ms = jnp.mean(xv * xv, axis=-1, keepdims=True)
    return xv * lax.rsqrt(ms + EPS) * g * sc + sh


def _sigmoid(v):
    return 1.0 / (1.0 + jnp.exp(-v))


def _bdot(a, b):
    return jnp.dot(a, b, preferred_element_type=F32)


_NT = (((1,), (1,)), ((), ()))
_TN = (((0,), (0,)), ((), ()))


def _mm_kernel(a_ref, b_ref, o_ref):
    o_ref[...] = _bdot(a_ref[...].astype(BF16), b_ref[...]).astype(o_ref.dtype)


def _mm(a, b, out_dtype=F32, name="mm"):
    M, K = a.shape
    N = b.shape[1]
    tm = _pick(M, 768, 512, 256, 128)
    tn = _pick(N, 512, 256, 128)
    return pl.pallas_call(
        _mm_kernel,
        grid=(M // tm, N // tn),
        in_specs=[pl.BlockSpec((tm, K), lambda i, j: (i, 0)),
                  pl.BlockSpec((K, tn), lambda i, j: (0, j))],
        out_specs=pl.BlockSpec((tm, tn), lambda i, j: (i, j)),
        out_shape=jax.ShapeDtypeStruct((M, N), out_dtype),
        compiler_params=_cparams("parallel", "parallel"),
        name=name,
    )(a, b.astype(BF16))


def _modvec_kernel(c_ref, w_ref, b_ref, o_ref):
    cv = c_ref[...]
    sc = cv * _sigmoid(cv)
    o_ref[...] = jnp.dot(sc, w_ref[...], preferred_element_type=F32, precision=HIGHEST) + b_ref[...]


def _modvec(cvec, w, b):
    R, D = cvec.shape
    N = w.shape[1]
    tn = _pick(N, 512, 256, 128)
    return pl.pallas_call(
        _modvec_kernel,
        grid=(N // tn,),
        in_specs=[pl.BlockSpec((R, D), lambda j: (0, 0)),
                  pl.BlockSpec((D, tn), lambda j: (0, j)),
                  pl.BlockSpec((1, tn), lambda j: (0, j))],
        out_specs=pl.BlockSpec((R, tn), lambda j: (0, j)),
        out_shape=jax.ShapeDtypeStruct((R, N), F32),
        compiler_params=_cparams("parallel"),
        name="modvec",
    )(cvec, w, b.reshape(1, N))


def _inproj0_kernel(x_ref, g_ref, sc_ref, sh_ref, wt_ref, w2_ref, pt_ref, ua_ref, ub_ref):
    h = _rms_mod(x_ref[0], g_ref[...], sc_ref[0], sh_ref[0]).astype(BF16)
    pt_ref[0] = lax.dot_general(wt_ref[...], h, _NT, preferred_element_type=F32).astype(BF16)
    u = _bdot(h, w2_ref[...])
    ua_ref[0] = u[:, :LANE]
    ub_ref[0] = u[:, LANE:]


def _inproj0(x, g, sc, sh, wt_hy, w_s5):
    B, L, D = x.shape
    C = wt_hy.shape[0]
    tl = _pick(L, 512, 256, 128)
    return pl.pallas_call(
        _inproj0_kernel,
        grid=(B, L // tl),
        in_specs=[pl.BlockSpec((1, tl, D), lambda b, i: (b, i, 0)),
                  pl.BlockSpec((1, D), lambda b, i: (0, 0)),
                  pl.BlockSpec((1, 1, D), lambda b, i: (b, 0, 0)),
                  pl.BlockSpec((1, 1, D), lambda b, i: (b, 0, 0)),
                  pl.BlockSpec((C, D), lambda b, i: (0, 0)),
                  pl.BlockSpec((D, S5_D), lambda b, i: (0, 0))],
        out_specs=[pl.BlockSpec((1, C, tl), lambda b, i: (b, 0, i)),
                   pl.BlockSpec((1, tl, LANE), lambda b, i: (b, i, 0)),
                   pl.BlockSpec((1, tl, LANE), lambda b, i: (b, i, 0))],
        out_shape=[jax.ShapeDtypeStruct((B, C, L), BF16),
                   jax.ShapeDtypeStruct((B, L, LANE), F32),
                   jax.ShapeDtypeStruct((B, L, LANE), F32)],
        compiler_params=_cparams("parallel", "parallel"),
        name="inproj0",
    )(x, g, sc, sh, wt_hy, w_s5)


def _shortconv_kernel(p1_ref, p2_ref, p3_ref, w_ref, b_ref, x1_ref, x2_ref, v_ref):
    L = p1_ref.shape[2]
    lane = lax.broadcasted_iota(jnp.int32, (1, L), 1)
    first = lane == 0
    last = lane == L - 1

    def conv(u, k):
        w = w_ref[k]
        prev = jnp.where(first, 0.0, pltpu.roll(u, 1, 1))
        nxt = jnp.where(last, 0.0, pltpu.roll(u, L - 1, 1))
        return w[:, 0:1] * prev + w[:, 1:2] * u + w[:, 2:3] * nxt + b_ref[k]

    x1_ref[0] = conv(p1_ref[0].astype(F32), 0).astype(BF16)
    x2_ref[0] = conv(p2_ref[0].astype(F32), 1).astype(BF16)
    v_ref[0] = conv(p3_ref[0].astype(F32), 2).astype(BF16)


def _shortconv(pt, conv_w, conv_b):
    B, C3, L = pt.shape
    tc = 128
    nb = HY_D // tc
    w = conv_w.T.reshape(3, HY_D, 3)
    b = conv_b.reshape(3, HY_D, 1)
    specs = [pl.BlockSpec((1, tc, L), (lambda b_, i, k=k: (b_, k * nb + i, 0))) for k in range(3)]
    o_spec = pl.BlockSpec((1, tc, L), lambda b_, i: (b_, i, 0))
    return pl.pallas_call(
        _shortconv_kernel,
        grid=(B, nb),
        in_specs=specs + [pl.BlockSpec((3, tc, 3), lambda b_, i: (0, i, 0)),
                          pl.BlockSpec((3, tc, 1), lambda b_, i: (0, i, 0))],
        out_specs=[o_spec, o_spec, o_spec],
        out_shape=[jax.ShapeDtypeStruct((B, HY_D, L), BF16)] * 3,
        compiler_params=_cparams("parallel", "parallel"),
        name="shortconv",
    )(pt, pt, pt, w, b)


def _filter_kernel(z_ref, w1_ref, b1_ref, f1_ref, w2_ref, b2_ref, f2_ref, w3_ref, dec_ref, o_ref):
    dot = functools.partial(jnp.dot, preferred_element_type=F32, precision=HIGHEST)
    z = z_ref[...]
    hdn = jnp.sin(f1_ref[...] * (dot(w1_ref[...], z) + b1_ref[...]))
    hdn = jnp.sin(f2_ref[...] * (dot(w2_ref[...], hdn) + b2_ref[...]))
    t = z[0:1, :]
    o_ref[...] = dot(w3_ref[...], hdn) * jnp.exp(-dec_ref[...] * t)


def _hyena_filter(L, fw1, fb1, ff1, fw2, fb2, ff2, fw3):
    pos = jnp.arange(L, dtype=F32)
    t = pos / max(L - 1, 1)
    w = 2.0 * math.pi * pos / L
    bands = jnp.linspace(1e-4, HY_BANDS - 1, HY_BANDS, dtype=F32)
    ang = bands[:, None] * w[None, :]
    z = jnp.concatenate([t[None, :], jnp.cos(ang), -jnp.sin(ang)], axis=0)
    z = jnp.pad(z, ((0, LANE - HY_EMB), (0, 0)))
    w1 = jnp.pad(fw1, ((0, LANE - HY_EMB), (0, 0))).T
    deltas = jnp.abs(jnp.linspace(math.log(HY_TARGET) / HY_DECAY_LONG,
                                  math.log(HY_TARGET) / HY_DECAY_SHORT, HY_D, dtype=F32))
    dec = jnp.tile(deltas, HY_ORDER * 2).reshape(-1, 1)
    nf = fw1.shape[1]
    No = fw3.shape[1]
    tl = _pick(L, 512, 256)
    full = lambda i: (0, 0)
    col = lambda v: v.reshape(nf, 1)
    return pl.pallas_call(
        _filter_kernel,
        grid=(L // tl,),
        in_specs=[pl.BlockSpec((LANE, tl), lambda i: (0, i)),
                  pl.BlockSpec((nf, LANE), full), pl.BlockSpec((nf, 1), full), pl.BlockSpec((nf, 1), full),
                  pl.BlockSpec((nf, nf), full), pl.BlockSpec((nf, 1), full), pl.BlockSpec((nf, 1), full),
                  pl.BlockSpec((No, nf), full), pl.BlockSpec((No, 1), full)],
        out_specs=pl.BlockSpec((No, tl), lambda i: (0, i)),
        out_shape=jax.ShapeDtypeStruct((No, L), F32),
        compiler_params=_cparams("parallel"),
        name="hyena_filter",
    )(z, w1, col(fb1), col(ff1), fw2.T, col(fb2), col(ff2), fw3.T, dec)


def _dft_mats(L):
    N = 2 * L
    W = 64
    j = jnp.arange(2 * L, dtype=jnp.int32)
    kk = (j // (2 * DFT_BLK)) * DFT_BLK + j % DFT_BLK
    is_im = (j // DFT_BLK) % 2 == 1
    nyq = is_im & (kk == 0)
    hi = jnp.arange(L // W, dtype=jnp.int32) * W
    lo = jnp.arange(W, dtype=jnp.int32)
    alt = (1 - 2 * (lo % 2)).astype(F32)
    th = 2.0 * math.pi / N
    ah = ((hi[:, None] * kk[None, :]) % N).astype(F32) * th
    al = ((lo[:, None] * kk[None, :]) % N).astype(F32) * th
    ch, sh, cl, sl = jnp.cos(ah), jnp.sin(ah), jnp.cos(al), jnp.sin(al)
    c = ch[:, None, :] * cl[None] - sh[:, None, :] * sl[None]
    s = sh[:, None, :] * cl[None] + ch[:, None, :] * sl[None]
    wf = jnp.where(nyq[None, None, :], alt[None, :, None], jnp.where(is_im[None, None, :], -s, c))
    ct = ch.T[:, :, None] * cl.T[:, None, :] - sh.T[:, :, None] * sl.T[:, None, :]
    st = sh.T[:, :, None] * cl.T[:, None, :] + ch.T[:, :, None] * sl.T[:, None, :]
    scale = jnp.where(kk == 0, 1.0 / N, 2.0 / N).astype(F32)[:, None, None]
    wi = jnp.where(nyq[:, None, None], alt[None, None, :], jnp.where(is_im[:, None, None], -st, ct)) * scale
    return wf.reshape(L, 2 * L).astype(BF16), wi.reshape(2 * L, L).astype(BF16)


def _filter_spectrum_kernel(hf_ref, hb_ref, wf_ref, kr_ref, ki_ref, krp_ref):
    j = pl.program_id(1)
    hf = hf_ref[...]
    lag = lax.broadcasted_iota(jnp.int32, (1, hf.shape[1]), 1)
    hb = jnp.where(lag == 0, 0.0, hb_ref[...])
    nrm = lax.rsqrt(jnp.sum(hf * hf, axis=1, keepdims=True) + jnp.sum(hb * hb, axis=1, keepdims=True))
    af = _bdot(hf.astype(BF16), wf_ref[...])
    ab = _bdot(hb.astype(BF16), wf_ref[...])
    re = (af[:, :DFT_BLK] + ab[:, :DFT_BLK]) * nrm
    im = (af[:, DFT_BLK:] - ab[:, DFT_BLK:]) * nrm
    nyq = (af[:, DFT_BLK:] + ab[:, DFT_BLK:]) * nrm
    col = lax.broadcasted_iota(jnp.int32, (1, DFT_BLK), 1)
    bin0 = (col == 0) & (j == 0)
    kr_ref[...] = re
    ki_ref[...] = jnp.where(bin0, 0.0, im)
    krp_ref[...] = jnp.where(bin0, nyq, re)


def _filter_spectrum(ht, wf, L):
    tm = 256
    nc = HY_D // tm
    out = jax.ShapeDtypeStruct((HY_ORDER * HY_D, L), F32)
    ospec = pl.BlockSpec((tm, DFT_BLK), lambda i, j: (i, j))
    kr, ki, krp = pl.pallas_call(
        _filter_spectrum_kernel,
        grid=(HY_ORDER * nc, L // DFT_BLK),
        in_specs=[pl.BlockSpec((tm, L), lambda i, j: ((i // nc) * 2 * nc + i % nc, 0)),
                  pl.BlockSpec((tm, L), lambda i, j: ((i // nc) * 2 * nc + nc + i % nc, 0)),
                  pl.BlockSpec((L, 2 * DFT_BLK), lambda i, j: (0, j))],
        out_specs=[ospec, ospec, ospec],
        out_shape=[out, out, out],
        compiler_params=_cparams("parallel", "arbitrary"),
        name="filter_spectrum",
    )(ht, ht, wf)
    shape = (HY_ORDER, HY_D, L)
    return kr.reshape(shape), ki.reshape(shape), krp.reshape(shape)


def _dftfwd_kernel(v_ref, wf_ref, kr_ref, ki_ref, krp_ref, y_ref):
    acc = _bdot(v_ref[0], wf_ref[...])
    vr = acc[:, :DFT_BLK]
    vi = acc[:, DFT_BLK:]
    kr = kr_ref[...]
    ki = ki_ref[...]
    yr = vr * kr - vi * ki
    yi = vr * ki + vi * krp_ref[...]
    y_ref[0] = jnp.concatenate([yr, yi], axis=1).astype(BF16)


def _dftfwd(v, wf, kr, ki, krp):
    B, C, L = v.shape
    nb = L // DFT_BLK
    kspec = pl.BlockSpec((C, DFT_BLK), lambda b, j: (0, j))
    return pl.pallas_call(
        _dftfwd_kernel,
        grid=(B, nb),
        in_specs=[pl.BlockSpec((1, C, L), lambda b, j: (b, 0, 0)),
                  pl.BlockSpec((L, 2 * DFT_BLK), lambda b, j: (0, j)),
                  kspec, kspec, kspec],
        out_specs=pl.BlockSpec((1, C, 2 * DFT_BLK), lambda b, j: (b, 0, j)),
        out_shape=jax.ShapeDtypeStruct((B, C, 2 * L), BF16),
        compiler_params=_cparams("parallel", "arbitrary"),
        name="hyena_dft_fwd",
    )(v, wf, kr, ki, krp)


def _dftinv_kernel(y_ref, wi_ref, xg_ref, v_ref, bias_ref, o_ref):
    conv = _bdot(y_ref[0], wi_ref[...])
    o_ref[0] = (xg_ref[0].astype(F32) * (conv + v_ref[0].astype(F32) * bias_ref[...])).astype(o_ref.dtype)


def _dftinv(y, wi, xg, v, bias, out_dtype):
    B, C, L2 = y.shape
    L = L2 // 2
    tn = 256
    tspec = pl.BlockSpec((1, C, tn), lambda b, j: (b, 0, j))
    return pl.pallas_call(
        _dftinv_kernel,
        grid=(B, L // tn),
        in_specs=[pl.BlockSpec((1, C, L2), lambda b, j: (b, 0, 0)),
                  pl.BlockSpec((L2, tn), lambda b, j: (0, j)),
                  tspec, tspec,
                  pl.BlockSpec((C, 1), lambda b, j: (0, 0))],
        out_specs=tspec,
        out_shape=jax.ShapeDtypeStruct((B, C, L), out_dtype),
        compiler_params=_cparams("parallel", "arbitrary"),
        name="hyena_dft_inv",
    )(y, wi, xg, v, bias)


def _hyena(pt, conv_w, conv_b, hy_bias, filt):
    L = pt.shape[2]
    wf, wi = _dft_mats(L)
    kr, ki, krp = _filter_spectrum(_hyena_filter(L, *filt), wf, L)
    x1, x2, v = _shortconv(pt, conv_w, conv_b)
    y1 = _dftfwd(v, wf, kr[0], ki[0], krp[0])
    z = _dftinv(y1, wi, x1, v, hy_bias[0].reshape(HY_D, 1), BF16)
    y2 = _dftfwd(z, wf, kr[1], ki[1], krp[1])
    return _dftinv(y2, wi, x2, z, hy_bias[1].reshape(HY_D, 1), BF16)


def _s5_operators(lam_re, lam_im, log_step, b_re, b_im, c_re, c_im):
    lam = lax.complex(lam_re.astype(F32), lam_im.astype(F32))
    dt = jnp.exp(log_step.astype(F32))[..., None]
    lam_bar = jnp.exp(lam * dt)
    lam_t = jnp.exp(lam * dt * S5_CHUNK)
    b_bar = ((lam_bar - 1.0) / lam)[..., None] * lax.complex(b_re.astype(F32), b_im.astype(F32))
    c_mat = lax.complex(c_re.astype(F32), c_im.astype(F32))
    gh = S5_GROUPS // S5_HALVES
    eye = jnp.eye(gh, dtype=F32)

    def bd_in(t):
        t = t.reshape(2, S5_HALVES, gh, S5_STATE, S5_GROUP)
        return jnp.einsum('dfgpn,gh->dfgnhp', t, eye).reshape(2, S5_HALVES, LANE, S5_PH)

    def bd_out(t):
        t = t.reshape(2, S5_HALVES, gh, S5_GROUP, S5_STATE)
        return jnp.einsum('dfgnp,gh->dfgphn', t, eye).reshape(2, S5_HALVES, S5_PH, LANE)

    bbd = jnp.stack([bd_in(jnp.real(b_bar)), bd_in(jnp.imag(b_bar))], axis=1).astype(BF16)
    cbd = jnp.stack([bd_out(jnp.real(c_mat)), bd_out(-jnp.imag(c_mat))], axis=1).astype(BF16)
    flat = lambda z: jnp.stack([jnp.real(z), jnp.imag(z)], axis=1).reshape(2, 2, S5_P)
    return bbd, cbd, flat(lam_bar), flat(lam_t)


def _s5_kernel(ua_ref, ub_ref, s0_ref, bbd_ref, cbd_ref, lam_ref, lamt_ref, ya_ref, yb_ref, sfin_ref,
               sloc_scr, sinit_scr):
    T = S5_CHUNK
    PH = S5_PH
    M = ua_ref.shape[1] // T
    nt = M // SUBLANE
    rid = lax.broadcasted_iota(jnp.int32, (SUBLANE, PH), 0)
    combos = [(hf, d) for hf in range(S5_HALVES) for d in (0, 1)]
    u_refs = (ua_ref, ub_ref)
    y_refs = (ya_ref, yb_ref)
    re_sl = lambda hf: slice(hf * PH, (hf + 1) * PH)
    im_sl = lambda hf: slice(S5_P + hf * PH, S5_P + (hf + 1) * PH)
    order = lambda d: list(range(T)) if d == 0 else list(range(T - 1, -1, -1))

    def drive(hf, d, s):
        us = u_refs[hf][0, pl.ds(s, M, stride=T), :].astype(BF16)
        return _bdot(us, bbd_ref[d, 0, hf]), _bdot(us, bbd_ref[d, 1, hf])

    def advance(hf, d, s, sr, si):
        lr = lam_ref[d, 0:1, re_sl(hf)]
        li = lam_ref[d, 1:2, re_sl(hf)]
        br, bi = drive(hf, d, s)
        return lr * sr - li * si + br, lr * si + li * sr + bi

    for q, (hf, d) in enumerate(combos):
        steps = order(d)
        sr, si = drive(hf, d, steps[0])
        for s in steps[1:]:
            sr, si = advance(hf, d, s, sr, si)
        sloc_scr[q, :, :PH] = sr
        sloc_scr[q, :, PH:] = si

    def scan_tile(n, carry):
        new = []
        for q, (hf, d) in enumerate(combos):
            cr, ci = carry[2 * q], carry[2 * q + 1]
            ar = lamt_ref[d, 0:1, re_sl(hf)]
            ai = lamt_ref[d, 1:2, re_sl(hf)]
            base = pl.multiple_of((n if d == 0 else nt - 1 - n) * SUBLANE, SUBLANE)
            lr_t = sloc_scr[q, pl.ds(base, SUBLANE), :PH]
            li_t = sloc_scr[q, pl.ds(base, SUBLANE), PH:]
            out_r = jnp.zeros((SUBLANE, PH), F32)
            out_i = jnp.zeros((SUBLANE, PH), F32)
            for r in (range(SUBLANE) if d == 0 else range(SUBLANE - 1, -1, -1)):
                out_r = jnp.where(rid == r, cr, out_r)
                out_i = jnp.where(rid == r, ci, out_i)
                cr, ci = (ar * cr - ai * ci + lr_t[r:r + 1], ar * ci + ai * cr + li_t[r:r + 1])
            sinit_scr[q, pl.ds(base, SUBLANE), :PH] = out_r
            sinit_scr[q, pl.ds(base, SUBLANE), PH:] = out_i
            new += [cr, ci]
        return tuple(new)

    init = []
    for hf, d in combos:
        init += [s0_ref[0, d:d + 1, re_sl(hf)], s0_ref[0, d:d + 1, im_sl(hf)]]
    fin = lax.fori_loop(0, nt, scan_tile, tuple(init))
    for q, (hf, d) in enumerate(combos):
        sfin_ref[0, d:d + 1, re_sl(hf)] = fin[2 * q]
        sfin_ref[0, d:d + 1, im_sl(hf)] = fin[2 * q + 1]

    for q, (hf, d) in enumerate(combos):
        sr = sinit_scr[q, :, :PH]
        si = sinit_scr[q, :, PH:]
        for s in order(d):
            sr, si = advance(hf, d, s, sr, si)
            ys = (_bdot(sr.astype(BF16), cbd_ref[d, 0, hf]) + _bdot(si.astype(BF16), cbd_ref[d, 1, hf]))
            if d == 0:
                y_refs[hf][0, pl.ds(s, M, stride=T), :] = ys
            else:
                y_refs[hf][0, pl.ds(s, M, stride=T), :] += ys


def _s5_pass(ua, ub, ops, s0):
    bbd, cbd, lam, lam_t = ops
    B, L, _ = ua.shape
    M = L // S5_CHUNK
    full = lambda nd: (lambda b: (0,) * nd)
    tok = pl.BlockSpec((1, L, LANE), lambda b: (b, 0, 0))
    st = pl.BlockSpec((1, 2, 2 * S5_P), lambda b: (b, 0, 0))
    return pl.pallas_call(
        _s5_kernel,
        grid=(B,),
        in_specs=[tok, tok, st,
                  pl.BlockSpec(bbd.shape, full(5)),
                  pl.BlockSpec(cbd.shape, full(5)),
                  pl.BlockSpec((2, 2, S5_P), full(3)),
                  pl.BlockSpec((2, 2, S5_P), full(3))],
        out_specs=[tok, tok, st],
        out_shape=[jax.ShapeDtypeStruct((B, L, LANE), F32),
                   jax.ShapeDtypeStruct((B, L, LANE), F32),
                   jax.ShapeDtypeStruct((B, 2, 2 * S5_P), F32)],
        scratch_shapes=[pltpu.VMEM((2 * S5_HALVES, M, 2 * S5_PH), F32),
                        pltpu.VMEM((2 * S5_HALVES, M, 2 * S5_PH), F32)],
        compiler_params=_cparams("parallel"),
        name="s5",
    )(ua, ub, s0, bbd, cbd, lam, lam_t)


def _gelu_tanh(v):
    return 0.5 * v * (1.0 + jnp.tanh(math.sqrt(2.0 / math.pi) * (v + 0.044715 * v * v * v)))


def _outproj0_kernel(hy_ref, ya_ref, yb_ref, ua_ref, ub_ref, d_ref, gw_ref, gb_ref, wa_ref, wb_ref,
                     x_ref, m_ref, o_ref):
    ys = (jnp.concatenate([ya_ref[0], yb_ref[0]], axis=1)
          + d_ref[...] * jnp.concatenate([ua_ref[0], ub_ref[0]], axis=1))
    glu = _bdot(_gelu_tanh(ys).astype(BF16), gw_ref[...]) + gb_ref[...]
    s5 = glu[:, :S5_D] * _sigmoid(glu[:, S5_D:])
    y = lax.dot_general(hy_ref[0], wa_ref[...], _TN, preferred_element_type=F32)
    y = y + _bdot(s5.astype(BF16), wb_ref[...])
    o_ref[0] = x_ref[0] + m_ref[0] * y


def _outproj0(hy, ys, u, d, glu_w, glu_b, w_out, x, m2):
    B, L, D = x.shape
    tl = _pick(L, 512, 256, 128)
    full = lambda b, i: (0, 0)
    tok = lambda w: pl.BlockSpec((1, tl, w), lambda b, i: (b, i, 0))
    return pl.pallas_call(
        _outproj0_kernel,
        grid=(B, L // tl),
        in_specs=[pl.BlockSpec((1, HY_D, tl), lambda b, i: (b, 0, i)),
                  tok(LANE), tok(LANE), tok(LANE), tok(LANE),
                  pl.BlockSpec((1, S5_D), full),
                  pl.BlockSpec((S5_D, 2 * S5_D), full),
                  pl.BlockSpec((1, 2 * S5_D), full),
                  pl.BlockSpec((HY_D, D), full),
                  pl.BlockSpec((S5_D, D), full),
                  tok(D),
                  pl.BlockSpec((1, 1, D), lambda b, i: (b, 0, 0))],
        out_specs=tok(D),
        out_shape=jax.ShapeDtypeStruct((B, L, D), F32),
        compiler_params=_cparams("parallel", "parallel"),
        name="outproj0",
    )(hy, ys[0], ys[1], u[0], u[1], d.reshape(1, S5_D), glu_w.astype(BF16), glu_b.reshape(1, -1),
      w_out[:HY_D].astype(BF16), w_out[HY_D:].astype(BF16), x, m2)


def _moe_kernel(x_ref, g_ref, sc_ref, sh_ref, m_ref, wrh_ref, wrl_ref, br_ref, tril_ref,
                wg_ref, wu_ref, wd_ref, fg_ref, o_ref, hs_scr, gs_scr, pt_scr, ys_scr, seg_smem,
                *, ne, final_norm):
    j = pl.program_id(2)
    nj = pl.num_programs(2)
    tl = x_ref.shape[1]
    S = hs_scr.shape[0]

    @pl.when(j == 0)
    def _():
        hn = _rms_mod(x_ref[0], g_ref[...], sc_ref[0], sh_ref[0])
        hi = hn.astype(BF16)
        lo = (hn - hi.astype(F32)).astype(BF16)
        wrh = wrh_ref[...]
        logits = _bdot(hi, wrh) + _bdot(lo, wrh) + _bdot(hi, wrl_ref[...]) + br_ref[...]
        lane = lax.broadcasted_iota(jnp.int32, logits.shape, 1)
        neg = -jnp.inf
        lgm = jnp.where(lane < MOE_GROUPS, logits, neg)
        gmax = jnp.max(lgm, axis=1, keepdims=True)
        p_top = 1.0 / jnp.sum(jnp.exp(lgm - gmax), axis=1, keepdims=True)
        gidx = jnp.min(jnp.where(lgm == gmax, lane, LANE), axis=1, keepdims=True)
        elane = lane - MOE_GROUPS
        in_group = (elane >= gidx * MOE_EPG) & (elane < (gidx + 1) * MOE_EPG)
        lem = jnp.where(in_group, logits, neg)
        v1 = jnp.max(lem, axis=1, keepdims=True)
        i1 = jnp.min(jnp.where(lem == v1, lane, LANE), axis=1, keepdims=True)
        lem2 = jnp.where(lane == i1, neg, lem)
        v2 = jnp.max(lem2, axis=1, keepdims=True)
        i2 = jnp.min(jnp.where(lem2 == v2, lane, LANE), axis=1, keepdims=True)
        e2 = jnp.exp(v2 - v1)
        w1 = p_top / (1.0 + e2)
        gate = jnp.where(lane == i1, w1, jnp.where(lane == i2, w1 * e2, 0.0))
        onehot = (lane == gidx).astype(F32)
        rank = _bdot(tril_ref[...], onehot.astype(BF16))
        off = jnp.int32(0)
        dest = jnp.zeros((tl, 1), F32)
        for g in range(MOE_GROUPS):
            col = onehot[:, g:g + 1]
            n_g = jnp.sum(col).astype(jnp.int32)
            seg_smem[g] = off
            seg_smem[MOE_GROUPS + g] = n_g
            dest = dest + col * (off.astype(F32) + rank[:, g:g + 1])
            off = off + ((n_g + MOE_ALIGN - 1) // MOE_ALIGN) * MOE_ALIGN
        slot = lax.broadcasted_iota(jnp.int32, (tl, S), 1)
        pt = (slot == dest.astype(jnp.int32)).astype(BF16)
        pt_scr[...] = pt
        hs_scr[...] = lax.dot_general(pt, hi, _TN, preferred_element_type=F32).astype(BF16)
        ghi = gate.astype(BF16)
        glo = (gate - ghi.astype(F32)).astype(BF16)
        gs_scr[...] = (lax.dot_general(pt, ghi, _TN, preferred_element_type=F32)
                       + lax.dot_general(pt, glo, _TN, preferred_element_type=F32))
        ys_scr[...] = jnp.zeros_like(ys_scr)

    grp = j // (MOE_EPG // ne)
    start = pl.multiple_of(seg_smem[grp], MOE_ALIGN)
    units = (seg_smem[MOE_GROUPS + grp] + MOE_UNIT - 1) // MOE_UNIT
    lane1 = lax.broadcasted_iota(jnp.int32, (1, LANE), 1)

    def block(r0, rows):
        hs = hs_scr[pl.ds(r0, rows), :]
        gsb = gs_scr[pl.ds(r0, rows), :]
        acc = None
        for e in range(ne):
            a = _bdot(hs, wg_ref[e])
            u = _bdot(hs, wu_ref[e])
            gcol = jnp.sum(jnp.where(lane1 == MOE_GROUPS + j * ne + e, gsb, 0.0), axis=1, keepdims=True)
            y = _bdot((a * _sigmoid(a) * u * gcol).astype(BF16), wd_ref[e])
            acc = y if acc is None else acc + y
        ys_scr[pl.ds(r0, rows), :] += acc

    @pl.when(units <= 2)
    def _():
        block(start, 2 * MOE_UNIT)

    @pl.when(units == 3)
    def _():
        block(start, 3 * MOE_UNIT)

    @pl.when(units >= 4)
    def _():
        def body(i, carry):
            block(pl.multiple_of(start + i * (2 * MOE_UNIT), MOE_ALIGN), 2 * MOE_UNIT)
            return carry

        lax.fori_loop(0, (units + 1) // 2, body, 0)

    @pl.when(j == nj - 1)
    def _():
        moe = _bdot(pt_scr[...], ys_scr[...].astype(BF16))
        out = x_ref[0] + m_ref[0] * moe
        if final_norm:
            ms = jnp.mean(out * out, axis=-1, keepdims=True)
            out = out * lax.rsqrt(ms + EPS) * fg_ref[...]
        o_ref[0] = out


def _moe(x, g2, sc, sh, m5, wrh, wrl, br, wg, wu, wd, final_g=None):
    B, L, D = x.shape
    ne = 4
    tl = _pick(L, 1024, 512, 256, 128)
    S = tl + 3 * MOE_UNIT
    final_norm = final_g is not None
    fg = (final_g if final_norm else jnp.ones((D,), F32)).reshape(1, D)
    idx = jnp.arange(tl, dtype=jnp.int32)
    tril = (idx[None, :] < idx[:, None]).astype(BF16)
    full = lambda b, i, j: (0, 0)
    tok = pl.BlockSpec((1, tl, D), lambda b, i, j: (b, i, 0))
    vec = pl.BlockSpec((1, 1, D), lambda b, i, j: (b, 0, 0))
    wspec = lambda k, n: pl.BlockSpec((ne, k, n), lambda b, i, j: (j, 0, 0))
    return pl.pallas_call(
        functools.partial(_moe_kernel, ne=ne, final_norm=final_norm),
        grid=(B, L // tl, MOE_EXPERTS // ne),
        in_specs=[tok, pl.BlockSpec((1, D), full), vec, vec, vec,
                  pl.BlockSpec((D, LANE), full), pl.BlockSpec((D, LANE), full), pl.BlockSpec((1, LANE), full),
                  pl.BlockSpec((tl, tl), full),
                  wspec(D, MOE_HIDDEN), wspec(D, MOE_HIDDEN), wspec(MOE_HIDDEN, D),
                  pl.BlockSpec((1, D), full)],
        out_specs=tok,
        out_shape=jax.ShapeDtypeStruct((B, L, D), F32),
        scratch_shapes=[pltpu.VMEM((S, D), BF16),
                        pltpu.VMEM((S, LANE), F32),
                        pltpu.VMEM((tl, S), BF16),
                        pltpu.VMEM((S, D), F32),
                        pltpu.SMEM((2 * MOE_GROUPS,), jnp.int32)],
        compiler_params=_cparams("parallel", "parallel", "arbitrary"),
        name="moe",
    )(x, g2, sc, sh, m5, wrh, wrl, br, tril, wg, wu, wd, fg)


def _moe_params(wg, bg, we, be, w_gate, w_up, w_down):
    D = wg.shape[0]
    pad = LANE - MOE_GROUPS - MOE_EXPERTS
    wr = jnp.pad(jnp.concatenate([wg, we], axis=1), ((0, 0), (0, pad)))
    br = jnp.pad(jnp.concatenate([bg, be]), (0, pad)).reshape(1, LANE)
    wrh = wr.astype(BF16)
    wrl = (wr - wrh.astype(F32)).astype(BF16)
    return (wrh, wrl, br,
            w_gate.reshape(MOE_EXPERTS, D, MOE_HIDDEN).astype(BF16),
            w_up.reshape(MOE_EXPERTS, D, MOE_HIDDEN).astype(BF16),
            w_down.reshape(MOE_EXPERTS, MOE_HIDDEN, D).astype(BF16))


def _inproj1_kernel(x_ref, g_ref, sc_ref, sh_ref, w_ref, o_ref, h_scr):
    @pl.when(pl.program_id(2) == 0)
    def _():
        h_scr[...] = _rms_mod(x_ref[0], g_ref[...], sc_ref[0], sh_ref[0]).astype(BF16)

    o_ref[0] = _bdot(h_scr[...], w_ref[...]).astype(o_ref.dtype)


def _inproj1(x, g, sc, sh, w):
    B, L, D = x.shape
    N = w.shape[1]
    tl = _pick(L, 1024, 512, 256, 128)
    tn = _pick(N, 2560, 1280, 1024, 512)
    return pl.pallas_call(
        _inproj1_kernel,
        grid=(B, L // tl, N // tn),
        in_specs=[pl.BlockSpec((1, tl, D), lambda b, i, j: (b, i, 0)),
                  pl.BlockSpec((1, D), lambda b, i, j: (0, 0)),
                  pl.BlockSpec((1, 1, D), lambda b, i, j: (b, 0, 0)),
                  pl.BlockSpec((1, 1, D), lambda b, i, j: (b, 0, 0)),
                  pl.BlockSpec((D, tn), lambda b, i, j: (0, j))],
        out_specs=pl.BlockSpec((1, tl, tn), lambda b, i, j: (b, i, j)),
        out_shape=jax.ShapeDtypeStruct((B, L, N), BF16),
        scratch_shapes=[pltpu.VMEM((tl, D), BF16)],
        compiler_params=_cparams("parallel", "parallel", "arbitrary"),
        name="inproj1",
    )(x, g, sc, sh, w)


def _gla_masks(C):
    t = np.arange(C)[:, None]
    s = np.arange(C)[None, :]
    fwd = []
    hs = 1
    while hs < C:
        fwd.append(((t // (2 * hs)) == (s // (2 * hs))) & ((t & hs) != 0) & ((s & hs) == 0))
        hs *= 2
    fwd = np.stack(fwd).astype(np.float32)
    return jnp.asarray(np.stack([fwd, fwd.transpose(0, 2, 1)])), jnp.asarray(np.eye(C, dtype=np.float32))


def _gla_direction(pq, v, z, lb, masks, eye, st_ref, rev):
    C = pq.shape[0]
    heads = [slice(h * LANE, (h + 1) * LANE) for h in range(C_HEADS)]
    q = pq * _sigmoid(pq)
    f = lb + (1.0 - lb) * _sigmoid(z)
    k = 1.0 - f
    g = jnp.log2(f)
    row = lax.broadcasted_iota(jnp.int32, (C, C_D), 0)
    qb = q.astype(BF16)
    kb = k.astype(BF16)
    att = [eye * lax.dot_general(qb[:, hd], kb[:, hd], _NT, preferred_element_type=F32) for hd in heads]
    pinc = g
    tot = g
    hs = 1
    lvl = 0
    while hs < C:
        if rev:
            aq = tot - pinc + g
            ak = pinc - g
        else:
            aq = pinc
            ak = tot - pinc
        qe = (q * jnp.exp2(aq)).astype(BF16)
        ke = (k * jnp.exp2(ak)).astype(BF16)
        for h, hd in enumerate(heads):
            blk = lax.dot_general(qe[:, hd], ke[:, hd], _NT, preferred_element_type=F32)
            att[h] = jnp.where(masks[lvl], blk, att[h])
        if hs < SUBLANE:
            odd = (row & hs) != 0
            t3 = tot.reshape(C // SUBLANE, SUBLANE, C_D)
            tprev = pltpu.roll(t3, hs, 1).reshape(C, C_D)
            tnext = pltpu.roll(t3, SUBLANE - hs, 1).reshape(C, C_D)
            pinc = pinc + jnp.where(odd, tprev, 0.0)
            tot = tot + jnp.where(odd, tprev, tnext)
        else:
            w = hs // SUBLANE
            tile = lambda a, i: a[i * SUBLANE:(i + 1) * SUBLANE]
            new_p, new_t = [], []
            for i in range(C // SUBLANE):
                if (i // w) % 2 == 1:
                    new_p.append(tile(pinc, i) + tile(tot, i - w))
                    new_t.append(tile(tot, i) + tile(tot, i - w))
                else:
                    new_p.append(tile(pinc, i))
                    new_t.append(tile(tot, i) + tile(tot, i + w))
            pinc = jnp.concatenate(new_p, axis=0)
            tot = jnp.concatenate(new_t, axis=0)
        hs *= 2
        lvl += 1
    if rev:
        q_dec = tot - pinc + g
        k_dec = pinc - g
    else:
        q_dec = pinc
        k_dec = tot - pinc
    vb = v.astype(BF16)
    qd = (q * jnp.exp2(q_dec)).astype(BF16)
    kd = (k * jnp.exp2(k_dec)).astype(BF16)
    keep = jnp.exp2(tot[0:1, :])
    outs = []
    for h, hd in enumerate(heads):
        st = st_ref[h]
        o = _bdot(att[h].astype(BF16), vb[:, hd])
        o = o + lax.dot_general(qd[:, hd], st.astype(BF16), _NT, preferred_element_type=F32)
        upd = lax.dot_general(vb[:, hd], kd[:, hd], _TN, preferred_element_type=F32)
        st_ref[h] = st * keep[:, hd] + upd
        outs.append(o)
    return jnp.concatenate(outs, axis=1)


def _gla_kernel(qf_ref, if_ref, zf_ref, qb_ref, ib_ref, zb_ref, lb_ref, mk_ref, eye_ref, s0_ref,
                of_ref, ob_ref, sfin_ref, st_scr):
    c = pl.program_id(1)

    @pl.when(c == 0)
    def _():
        st_scr[...] = s0_ref[0]

    eye = eye_ref[...]
    nlev = mk_ref.shape[1]
    for d, (q_ref, i_ref, z_ref, o_ref) in enumerate(((qf_ref, if_ref, zf_ref, of_ref),
                                                      (qb_ref, ib_ref, zb_ref, ob_ref))):
        masks = [mk_ref[d, lv] > 0.5 for lv in range(nlev)]
        o = _gla_direction(q_ref[0].astype(F32), i_ref[0].astype(F32), z_ref[0].astype(F32),
                           lb_ref[d:d + 1, :], masks, eye, st_scr.at[d], rev=(d == 1))
        o_ref[0] = o.astype(o_ref.dtype)

    @pl.when(c == pl.num_programs(1) - 1)
    def _():
        sfin_ref[0] = st_scr[...]


def _gla(p, lb, s0):
    B, L, _ = p.shape
    C = GLA_CHUNK
    nch = L // C
    masks, eye = _gla_masks(C)
    fcol = lambda k: pl.BlockSpec((1, C, C_D), lambda b, c: (b, c, k))
    bcol = lambda k: pl.BlockSpec((1, C, C_D), lambda b, c: (b, nch - 1 - c, k))
    st_spec = pl.BlockSpec((1, 2, C_HEADS, LANE, LANE), lambda b, c: (b, 0, 0, 0, 0))
    return pl.pallas_call(
        _gla_kernel,
        grid=(B, nch),
        in_specs=[fcol(0), fcol(2), fcol(3), bcol(0), bcol(2), bcol(4),
                  pl.BlockSpec((2, C_D), lambda b, c: (0, 0)),
                  pl.BlockSpec(masks.shape, lambda b, c: (0, 0, 0, 0)),
                  pl.BlockSpec((C, C), lambda b, c: (0, 0)),
                  st_spec],
        out_specs=[fcol(0), bcol(0), st_spec],
        out_shape=[jax.ShapeDtypeStruct((B, L, C_D), BF16),
                   jax.ShapeDtypeStruct((B, L, C_D), BF16),
                   jax.ShapeDtypeStruct((B, 2, C_HEADS, LANE, LANE), F32)],
        scratch_shapes=[pltpu.VMEM((2, C_HEADS, LANE, LANE), F32)],
        compiler_params=_cparams("parallel", "arbitrary"),
        name="gla",
    )(p, p, p, p, p, p, lb, masks, eye, s0)


def _outproj1_kernel(of_ref, ob_ref, g_ref, ng_ref, w_ref, x_ref, m_ref, o_ref):
    o = of_ref[0].astype(F32) + ob_ref[0].astype(F32)
    parts = []
    for h in range(C_HEADS):
        oh = o[:, h * LANE:(h + 1) * LANE]
        ms = jnp.mean(oh * oh, axis=-1, keepdims=True)
        parts.append(oh * lax.rsqrt(ms + EPS))
    on = jnp.concatenate(parts, axis=1) * ng_ref[...] * _sigmoid(g_ref[0].astype(F32))
    y = _bdot(on.astype(BF16), w_ref[...])
    o_ref[0] = x_ref[0] + m_ref[0] * y


def _outproj1(o_f, o_b, p, norm_g, w_out, x, m2):
    B, L, D = x.shape
    tl = _pick(L, 512, 256, 128)
    full = lambda b, i: (0, 0)
    tok = pl.BlockSpec((1, tl, C_D), lambda b, i: (b, i, 0))
    return pl.pallas_call(
        _outproj1_kernel,
        grid=(B, L // tl),
        in_specs=[tok, tok,
                  pl.BlockSpec((1, tl, C_D), lambda b, i: (b, i, 1)),
                  pl.BlockSpec((1, C_D), full),
                  pl.BlockSpec((C_D, D), full),
                  pl.BlockSpec((1, tl, D), lambda b, i: (b, i, 0)),
                  pl.BlockSpec((1, 1, D), lambda b, i: (b, 0, 0))],
        out_specs=pl.BlockSpec((1, tl, D), lambda b, i: (b, i, 0)),
        out_shape=jax.ShapeDtypeStruct((B, L, D), F32),
        compiler_params=_cparams("parallel", "parallel"),
        name="outproj1",
    )(o_f, o_b, p, norm_g.reshape(1, C_D), w_out.astype(BF16), x, m2)


def _mods(cmat, w, b, nb):
    R = cmat.shape[0]
    pad = (-R) % SUBLANE
    m = _modvec(jnp.pad(cmat, ((0, pad), (0, 0))), w, b)[:R]
    m = jnp.broadcast_to(m, (nb, m.shape[1])) if R == 1 else m
    return [m[:, None, k * D_MODEL:(k + 1) * D_MODEL] for k in range(6)]


def kernel(x, c, ctx, c_ctx, mod_w, mod_b, norm1_g, norm2_g, final_g,
           ab_w_in, ab_w_out, hy_conv_w, hy_conv_b, hy_fw1, hy_fb1, hy_ff1,
           hy_fw2, hy_fb2, hy_ff2, hy_fw3, hy_bias,
           s5_lam_re, s5_lam_im, s5_log_step, s5_b_re, s5_b_im, s5_c_re, s5_c_im,
           s5_d, s5_glu_w, s5_glu_b,
           c_w_in, c_w_out, c_lower_bounds, c_norm_g,
           moe_wg, moe_bg, moe_we, moe_be, moe_w_gate, moe_w_up, moe_w_down):
    B, L, D = x.shape
    row = lambda t: t.reshape(1, -1)

    m = _mods(c, mod_w[0], mod_b[0], B)
    mc = _mods(c_ctx[None, :], mod_w[0], mod_b[0], B)
    w_in = ab_w_in[0]
    wt_hy = w_in[:, :3 * HY_D].T.astype(BF16)
    w_s5 = w_in[:, 3 * HY_D:].astype(BF16)
    g1 = row(norm1_g[0])
    pt_c, *u_c = _inproj0(ctx, g1, 1.0 + mc[1], mc[0], wt_hy, w_s5)
    pt_l, *u_l = _inproj0(x, g1, 1.0 + m[1], m[0], wt_hy, w_s5)

    ops = _s5_operators(s5_lam_re[0], s5_lam_im[0], s5_log_step[0], s5_b_re[0], s5_b_im[0],
                        s5_c_re[0], s5_c_im[0])
    *ys_c, s_fin = _s5_pass(*u_c, ops, jnp.zeros((B, 2, 2 * S5_P), F32))
    *ys_l, _ = _s5_pass(*u_l, ops, s_fin)

    filt = (hy_fw1[0], hy_fb1[0], hy_ff1[0], hy_fw2[0], hy_fb2[0], hy_ff2[0], hy_fw3[0])
    hy_c = _hyena(pt_c, hy_conv_w[0], hy_conv_b[0], hy_bias[0], filt)
    hy_l = _hyena(pt_l, hy_conv_w[0], hy_conv_b[0], hy_bias[0], filt)

    x = _outproj0(hy_l, ys_l, u_l, s5_d[0], s5_glu_w[0], s5_glu_b[0], ab_w_out[0], x, m[2])
    ctx = _outproj0(hy_c, ys_c, u_c, s5_d[0], s5_glu_w[0], s5_glu_b[0], ab_w_out[0], ctx, mc[2])

    mp = _moe_params(moe_wg[0], moe_bg[0], moe_we[0], moe_be[0],
                     moe_w_gate[0], moe_w_up[0], moe_w_down[0])
    g2 = row(norm2_g[0])
    x = _moe(x, g2, 1.0 + m[4], m[3], m[5], *mp)
    ctx = _moe(ctx, g2, 1.0 + mc[4], mc[3], mc[5], *mp)

    m = _mods(c, mod_w[1], mod_b[1], B)
    mc = _mods(c_ctx[None, :], mod_w[1], mod_b[1], B)
    sm = jax.nn.softmax(c_lower_bounds.astype(F32), axis=1)
    lower = (jnp.cumsum(sm, axis=1) - sm[:, :1])[:, 1]
    g1 = row(norm1_g[1])
    w1 = c_w_in[0].astype(BF16)
    p_c = _inproj1(ctx, g1, 1.0 + mc[1], mc[0], w1)
    p_l = _inproj1(x, g1, 1.0 + m[1], m[0], w1)
    zeros = jnp.zeros((B, 2, C_HEADS, C_HEAD_DIM, C_HEAD_DIM), F32)
    _, _, s_ctx = _gla(p_c, lower, zeros)
    o_f, o_b, _ = _gla(p_l, lower, s_ctx)
    x = _outproj1(o_f, o_b, p_l, c_norm_g[0], c_w_out[0], x, m[2])

    mp = _moe_params(moe_wg[1], moe_bg[1], moe_we[1], moe_be[1],
                     moe_w_gate[1], moe_w_up[1], moe_w_down[1])
    return _moe(x, row(norm2_g[1]), 1.0 + m[4], m[3], m[5], *mp, final_g=final_g)
```

```python
import functools
import math

import numpy as np
import jax
import jax.numpy as jnp
from jax import lax
from jax.experimental import pallas as pl
from jax.experimental.pallas import tpu as pltpu

F32 = jnp.float32
BF16 = jnp.bfloat16
EPS = 1e-6
HIGHEST = lax.Precision.HIGHEST

D_MODEL = 1024
HY_D = 768
HY_ORDER = 2
HY_EMB = 33
HY_BANDS = (HY_EMB - 1) // 2
HY_DECAY_SHORT = 0.3
HY_DECAY_LONG = 1.5
HY_TARGET = 1e-2
S5_D = 256
S5_GROUP = 16
S5_GROUPS = S5_D // S5_GROUP
S5_STATE = 64
S5_P = S5_GROUPS * S5_STATE
S5_HALVES = S5_D // 128
S5_PH = S5_P // S5_HALVES
C_HEADS = 8
C_HEAD_DIM = 128
C_D = C_HEADS * C_HEAD_DIM
MOE_GROUPS = 4
MOE_EPG = 8
MOE_EXPERTS = MOE_GROUPS * MOE_EPG
MOE_HIDDEN = 256
MOE_UNIT = 128
MOE_ALIGN = 16

LANE = 128
SUBLANE = 8
HY_BLK = 512
S5_CHUNK = 16
GLA_CHUNK = 64
VMEM_LIMIT = 56 * 1024 * 1024


def _cparams(*sem):
    return pltpu.CompilerParams(dimension_semantics=sem, vmem_limit_bytes=VMEM_LIMIT)


def _pick(n, *cands):
    for c in cands:
        if n % c == 0:
            return c
    return n


def _rms_mod(xv, g, sc, sh):
    ms = jnp.mean(xv * xv, axis=-1, keepdims=True)
    return xv * lax.rsqrt(ms + EPS) * g * sc + sh


def _sigmoid(v):
    return 1.0 / (1.0 + jnp.exp(-v))


def _bdot(a, b):
    return jnp.dot(a, b, preferred_element_type=F32)


_NT = (((1,), (1,)), ((), ()))
_TN = (((0,), (0,)), ((), ()))


def _mm_kernel(a_ref, b_ref, o_ref):
    o_ref[...] = _bdot(a_ref[...].astype(BF16), b_ref[...]).astype(o_ref.dtype)


def _mm(a, b, out_dtype=F32, name="mm"):
    M, K = a.shape
    N = b.shape[1]
    tm = _pick(M, 768, 512, 256, 128)
    tn = _pick(N, 512, 256, 128)
    return pl.pallas_call(
        _mm_kernel,
        grid=(M // tm, N // tn),
        in_specs=[pl.BlockSpec((tm, K), lambda i, j: (i, 0)),
                  pl.BlockSpec((K, tn), lambda i, j: (0, j))],
        out_specs=pl.BlockSpec((tm, tn), lambda i, j: (i, j)),
        out_shape=jax.ShapeDtypeStruct((M, N), out_dtype),
        compiler_params=_cparams("parallel", "parallel"),
        name=name,
    )(a, b.astype(BF16))


def _modvec_kernel(c_ref, w_ref, b_ref, o_ref):
    cv = c_ref[...]
    sc = cv * _sigmoid(cv)
    o_ref[...] = jnp.dot(sc, w_ref[...], preferred_element_type=F32, precision=HIGHEST) + b_ref[...]


def _modvec(cvec, w, b):
    R, D = cvec.shape
    N = w.shape[1]
    tn = _pick(N, 512, 256, 128)
    return pl.pallas_call(
        _modvec_kernel,
        grid=(N // tn,),
        in_specs=[pl.BlockSpec((R, D), lambda j: (0, 0)),
                  pl.BlockSpec((D, tn), lambda j: (0, j)),
                  pl.BlockSpec((1, tn), lambda j: (0, j))],
        out_specs=pl.BlockSpec((R, tn), lambda j: (0, j)),
        out_shape=jax.ShapeDtypeStruct((R, N), F32),
        compiler_params=_cparams("parallel"),
        name="modvec",
    )(cvec, w, b.reshape(1, N))


def _inproj0_kernel(x_ref, g_ref, sc_ref, sh_ref, wt_ref, w2_ref, pt_ref, ua_ref, ub_ref):
    h = _rms_mod(x_ref[0], g_ref[...], sc_ref[0], sh_ref[0]).astype(BF16)
    pt_ref[0] = lax.dot_general(wt_ref[...], h, _NT, preferred_element_type=F32).astype(BF16)
    u = _bdot(h, w2_ref[...])
    ua_ref[0] = u[:, :LANE]
    ub_ref[0] = u[:, LANE:]


def _inproj0(x, g, sc, sh, wt_hy, w_s5):
    B, L, D = x.shape
    C = wt_hy.shape[0]
    tl = _pick(L, 512, 256, 128)
    return pl.pallas_call(
        _inproj0_kernel,
        grid=(B, L // tl),
        in_specs=[pl.BlockSpec((1, tl, D), lambda b, i: (b, i, 0)),
                  pl.BlockSpec((1, D), lambda b, i: (0, 0)),
                  pl.BlockSpec((1, 1, D), lambda b, i: (b, 0, 0)),
                  pl.BlockSpec((1, 1, D), lambda b, i: (b, 0, 0)),
                  pl.BlockSpec((C, D), lambda b, i: (0, 0)),
                  pl.BlockSpec((D, S5_D), lambda b, i: (0, 0))],
        out_specs=[pl.BlockSpec((1, C, tl), lambda b, i: (b, 0, i)),
                   pl.BlockSpec((1, tl, LANE), lambda b, i: (b, i, 0)),
                   pl.BlockSpec((1, tl, LANE), lambda b, i: (b, i, 0))],
        out_shape=[jax.ShapeDtypeStruct((B, C, L), BF16),
                   jax.ShapeDtypeStruct((B, L, LANE), F32),
                   jax.ShapeDtypeStruct((B, L, LANE), F32)],
        compiler_params=_cparams("parallel", "parallel"),
        name="inproj0",
    )(x, g, sc, sh, wt_hy, w_s5)


def _shortconv_kernel(p1_ref, p2_ref, p3_ref, w_ref, b_ref, x1_ref, x2_ref, v_ref):
    L = p1_ref.shape[2]
    nb, lb = x1_ref.shape[1], x1_ref.shape[3]
    lane = lax.broadcasted_iota(jnp.int32, (1, L), 1)
    first = lane == 0
    last = lane == L - 1
    for k, (p_ref, o_ref) in enumerate(((p1_ref, x1_ref), (p2_ref, x2_ref), (p3_ref, v_ref))):
        u = p_ref[0].astype(F32)
        w = w_ref[k]
        prev = jnp.where(first, 0.0, pltpu.roll(u, 1, 1))
        nxt = jnp.where(last, 0.0, pltpu.roll(u, L - 1, 1))
        y = (w[:, 0:1] * prev + w[:, 1:2] * u + w[:, 2:3] * nxt + b_ref[k]).astype(BF16)
        for j in range(nb):
            o_ref[0, j] = y[:, j * lb:(j + 1) * lb]


def _shortconv(pt, conv_w, conv_b, lb):
    B, C3, L = pt.shape
    nb = L // lb
    tc = 128
    nc = HY_D // tc
    w = conv_w.T.reshape(3, HY_D, 3)
    b = conv_b.reshape(3, HY_D, 1)
    specs = [pl.BlockSpec((1, tc, L), (lambda b_, i, k=k: (b_, k * nc + i, 0))) for k in range(3)]
    o_spec = pl.BlockSpec((1, nb, tc, lb), lambda b_, i: (b_, 0, i, 0))
    return pl.pallas_call(
        _shortconv_kernel,
        grid=(B, nc),
        in_specs=specs + [pl.BlockSpec((3, tc, 3), lambda b_, i: (0, i, 0)),
                          pl.BlockSpec((3, tc, 1), lambda b_, i: (0, i, 0))],
        out_specs=[o_spec, o_spec, o_spec],
        out_shape=[jax.ShapeDtypeStruct((B, nb, HY_D, lb), BF16)] * 3,
        compiler_params=_cparams("parallel", "parallel"),
        name="shortconv",
    )(pt, pt, pt, w, b)


def _filter_kernel(z_ref, w1_ref, b1_ref, f1_ref, w2_ref, b2_ref, f2_ref, w3_ref, dec_ref, o_ref):
    dot = functools.partial(jnp.dot, preferred_element_type=F32, precision=HIGHEST)
    z = z_ref[...]
    hdn = jnp.sin(f1_ref[...] * (dot(w1_ref[...], z) + b1_ref[...]))
    hdn = jnp.sin(f2_ref[...] * (dot(w2_ref[...], hdn) + b2_ref[...]))
    t = z[0:1, :]
    o_ref[...] = dot(w3_ref[...], hdn) * jnp.exp(-dec_ref[...] * t)


def _hyena_filter(L, fw1, fb1, ff1, fw2, fb2, ff2, fw3):
    pos = jnp.arange(L, dtype=F32)
    t = pos / max(L - 1, 1)
    w = 2.0 * math.pi * pos / L
    bands = jnp.linspace(1e-4, HY_BANDS - 1, HY_BANDS, dtype=F32)
    ang = bands[:, None] * w[None, :]
    z = jnp.concatenate([t[None, :], jnp.cos(ang), -jnp.sin(ang)], axis=0)
    z = jnp.pad(z, ((0, LANE - HY_EMB), (0, 0)))
    w1 = jnp.pad(fw1, ((0, LANE - HY_EMB), (0, 0))).T
    deltas = jnp.abs(jnp.linspace(math.log(HY_TARGET) / HY_DECAY_LONG,
                                  math.log(HY_TARGET) / HY_DECAY_SHORT, HY_D, dtype=F32))
    dec = jnp.tile(deltas, HY_ORDER * 2).reshape(-1, 1)
    nf = fw1.shape[1]
    No = fw3.shape[1]
    tl = _pick(L, 512, 256)
    full = lambda i: (0, 0)
    col = lambda v: v.reshape(nf, 1)
    return pl.pallas_call(
        _filter_kernel,
        grid=(L // tl,),
        in_specs=[pl.BlockSpec((LANE, tl), lambda i: (0, i)),
                  pl.BlockSpec((nf, LANE), full), pl.BlockSpec((nf, 1), full), pl.BlockSpec((nf, 1), full),
                  pl.BlockSpec((nf, nf), full), pl.BlockSpec((nf, 1), full), pl.BlockSpec((nf, 1), full),
                  pl.BlockSpec((No, nf), full), pl.BlockSpec((No, 1), full)],
        out_specs=pl.BlockSpec((No, tl), lambda i: (0, i)),
        out_shape=jax.ShapeDtypeStruct((No, L), F32),
        compiler_params=_cparams("parallel"),
        name="hyena_filter",
    )(z, w1, col(fb1), col(ff1), fw2.T, col(fb2), col(ff2), fw3.T, dec)


def _dft_mats(L, blk):
    N = 2 * L
    W = 64
    j = jnp.arange(2 * L, dtype=jnp.int32)
    kk = (j // (2 * blk)) * blk + j % blk
    is_im = (j // blk) % 2 == 1
    nyq = is_im & (kk == 0)
    hi = jnp.arange(L // W, dtype=jnp.int32) * W
    lo = jnp.arange(W, dtype=jnp.int32)
    alt = (1 - 2 * (lo % 2)).astype(F32)
    th = 2.0 * math.pi / N
    ah = ((hi[:, None] * kk[None, :]) % N).astype(F32) * th
    al = ((lo[:, None] * kk[None, :]) % N).astype(F32) * th
    ch, sh, cl, sl = jnp.cos(ah), jnp.sin(ah), jnp.cos(al), jnp.sin(al)
    c = ch[:, None, :] * cl[None] - sh[:, None, :] * sl[None]
    s = sh[:, None, :] * cl[None] + ch[:, None, :] * sl[None]
    wf = jnp.where(nyq[None, None, :], alt[None, :, None], jnp.where(is_im[None, None, :], -s, c))
    ct = ch.T[:, :, None] * cl.T[:, None, :] - sh.T[:, :, None] * sl.T[:, None, :]
    st = sh.T[:, :, None] * cl.T[:, None, :] + ch.T[:, :, None] * sl.T[:, None, :]
    scale = jnp.where(kk == 0, 1.0 / N, 2.0 / N).astype(F32)[:, None, None]
    wi = jnp.where(nyq[:, None, None], alt[None, None, :], jnp.where(is_im[:, None, None], -st, ct)) * scale
    return wf.reshape(L, 2 * L).astype(BF16), wi.reshape(2 * L, L).astype(BF16)


def _segdft_kernel(a_ref, w_ref, o_ref):
    o_ref[...] = _bdot(a_ref[...].astype(BF16), w_ref[...])


def _segdft(ht, wf, lb):
    R, L = ht.shape
    tm = 256
    return pl.pallas_call(
        _segdft_kernel,
        grid=(R // tm, L // lb),
        in_specs=[pl.BlockSpec((tm, lb), lambda i, m: (i, m)),
                  pl.BlockSpec((lb, 2 * lb), lambda i, m: (0, 0))],
        out_specs=pl.BlockSpec((tm, 2 * lb), lambda i, m: (i, m)),
        out_shape=jax.ShapeDtypeStruct((R, 2 * L), F32),
        compiler_params=_cparams("parallel", "parallel"),
        name="filter_segdft",
    )(ht, wf)


def _filter_blocks_kernel(hf_ref, hb_ref, af_ref, ab_ref, kr_ref, ki_ref, kn_ref, *, nb, lb):
    hf = hf_ref[...]
    hb = hb_ref[...]
    lag = lax.broadcasted_iota(jnp.int32, (1, hf.shape[1]), 1)
    hbz = jnp.where(lag == 0, 0.0, hb)
    nrm = lax.rsqrt(jnp.sum(hf * hf, axis=1, keepdims=True) + jnp.sum(hbz * hbz, axis=1, keepdims=True))
    k = lax.broadcasted_iota(jnp.int32, (1, lb), 1)
    sgn = jnp.where(k % 2 == 0, 1.0, -1.0)
    bin0 = k == 0

    def seg(ref, m):
        re = ref[:, m * 2 * lb:m * 2 * lb + lb]
        imp = ref[:, m * 2 * lb + lb:(m + 1) * 2 * lb]
        return re, jnp.where(bin0, 0.0, imp), imp[:, 0:1]

    def emit(idx, re, im, nyq):
        kr_ref[0, idx] = (re * nrm).astype(BF16)
        ki_ref[0, idx] = (im * nrm).astype(BF16)
        kn_ref[0, idx] = nyq * nrm

    fr, fi, fn = seg(af_ref, 0)
    br, bi, bn = seg(ab_ref, 0)
    b0 = hb[:, 0:1]
    emit(nb - 1, fr + br - b0, fi - bi, fn + bn - b0)
    for d in range(1, nb):
        for ref, taps, conj, idx in ((af_ref, hf, 1.0, nb - 1 + d), (ab_ref, hb, -1.0, nb - 1 - d)):
            r1, i1, n1 = seg(ref, d)
            r0, i0, n0 = seg(ref, d - 1)
            e0 = taps[:, (d - 1) * lb:(d - 1) * lb + 1]
            emit(idx, r1 + sgn * (r0 - e0), conj * (i1 + sgn * i0), n1 + n0 - e0)


def _filter_blocks(ht, seg, lb):
    R, L = ht.shape
    nb = L // lb
    nlag = 2 * nb - 1
    tm = 128
    nc = HY_D // tm
    fwd = lambda i: ((i // nc) * 2 * nc + i % nc, 0)
    bwd = lambda i: ((i // nc) * 2 * nc + nc + i % nc, 0)
    kspec = pl.BlockSpec((1, nlag, tm, lb), lambda i: (i // nc, 0, i % nc, 0))
    return pl.pallas_call(
        functools.partial(_filter_blocks_kernel, nb=nb, lb=lb),
        grid=(HY_ORDER * nc,),
        in_specs=[pl.BlockSpec((tm, L), fwd), pl.BlockSpec((tm, L), bwd),
                  pl.BlockSpec((tm, 2 * L), fwd), pl.BlockSpec((tm, 2 * L), bwd)],
        out_specs=[kspec, kspec, pl.BlockSpec((1, nlag, tm, 1), lambda i: (i // nc, 0, i % nc, 0))],
        out_shape=[jax.ShapeDtypeStruct((HY_ORDER, nlag, HY_D, lb), BF16),
                   jax.ShapeDtypeStruct((HY_ORDER, nlag, HY_D, lb), BF16),
                   jax.ShapeDtypeStruct((HY_ORDER, nlag, HY_D, 1), F32)],
        compiler_params=_cparams("parallel"),
        name="filter_blocks",
    )(ht, ht, seg, seg)


def _hyconv_kernel(v_ref, g_ref, bias_ref, kr_ref, ki_ref, kn_ref, wf_ref, wi_ref, o_ref,
                   vr_scr, vi_scr, vn_scr):
    nb, lb = v_ref.shape[1], v_ref.shape[3]
    bin0 = lax.broadcasted_iota(jnp.int32, (1, lb), 1) == 0
    for j in range(nb):
        acc = _bdot(v_ref[0, j], wf_ref[...])
        imp = acc[:, lb:]
        vr_scr[j] = acc[:, :lb]
        vi_scr[j] = jnp.where(bin0, 0.0, imp)
        vn_scr[j] = imp[:, 0:1]

    def out_block(i, carry):
        yr = jnp.zeros(vr_scr.shape[1:], F32)
        yi = jnp.zeros(vr_scr.shape[1:], F32)
        yn = jnp.zeros(vn_scr.shape[1:], F32)
        for j in range(nb):
            d = i - j + (nb - 1)
            kr = kr_ref[0, d].astype(F32)
            ki = ki_ref[0, d].astype(F32)
            vr = vr_scr[j]
            vi = vi_scr[j]
            yr = yr + (vr * kr - vi * ki)
            yi = yi + (vr * ki + vi * kr)
            yn = yn + vn_scr[j] * kn_ref[0, d]
        yi = jnp.where(bin0, yn, yi)
        conv = _bdot(yr.astype(BF16), wi_ref[:lb, :]) + _bdot(yi.astype(BF16), wi_ref[lb:, :])
        vb = v_ref[0, i].astype(F32)
        o_ref[0, i] = (g_ref[0, i].astype(F32) * (conv + vb * bias_ref[...])).astype(o_ref.dtype)
        return carry

    lax.fori_loop(0, nb, out_block, 0)


def _hyconv(v, gate, bias, kr, ki, kn, order, wf, wi):
    B, nb, C, lb = v.shape
    nlag = kr.shape[1]
    ct = 256
    blk = pl.BlockSpec((1, nb, ct, lb), lambda c, b: (b, 0, c, 0))
    kspec = pl.BlockSpec((1, nlag, ct, lb), lambda c, b: (order, 0, c, 0))
    return pl.pallas_call(
        _hyconv_kernel,
        grid=(C // ct, B),
        in_specs=[blk, blk,
                  pl.BlockSpec((ct, 1), lambda c, b: (c, 0)),
                  kspec, kspec,
                  pl.BlockSpec((1, nlag, ct, 1), lambda c, b: (order, 0, c, 0)),
                  pl.BlockSpec((lb, 2 * lb), lambda c, b: (0, 0)),
                  pl.BlockSpec((2 * lb, lb), lambda c, b: (0, 0))],
        out_specs=blk,
        out_shape=jax.ShapeDtypeStruct((B, nb, C, lb), BF16),
        scratch_shapes=[pltpu.VMEM((nb, ct, lb), F32), pltpu.VMEM((nb, ct, lb), F32),
                        pltpu.VMEM((nb, ct, 1), F32)],
        compiler_params=_cparams("parallel", "parallel"),
        name="hyena_conv",
    )(v, gate, bias.reshape(C, 1), kr, ki, kn, wf, wi)


def _hyena(pt, conv_w, conv_b, hy_bias, filt):
    L = pt.shape[2]
    lb = HY_BLK if L >= 2 * HY_BLK else L // 2
    wf, wi = _dft_mats(lb, lb)
    ht = _hyena_filter(L, *filt)
    kr, ki, kn = _filter_blocks(ht, _segdft(ht, wf, lb), lb)
    x1, x2, v = _shortconv(pt, conv_w, conv_b, lb)
    z = _hyconv(v, x1, hy_bias[0], kr, ki, kn, 0, wf, wi)
    return _hyconv(z, x2, hy_bias[1], kr, ki, kn, 1, wf, wi)


def _s5_operators(lam_re, lam_im, log_step, b_re, b_im, c_re, c_im):
    lam = lax.complex(lam_re.astype(F32), lam_im.astype(F32))
    dt = jnp.exp(log_step.astype(F32))[..., None]
    lam_bar = jnp.exp(lam * dt)
    lam_t = jnp.exp(lam * dt * S5_CHUNK)
    b_bar = ((lam_bar - 1.0) / lam)[..., None] * lax.complex(b_re.astype(F32), b_im.astype(F32))
    c_mat = lax.complex(c_re.astype(F32), c_im.astype(F32))
    gh = S5_GROUPS // S5_HALVES
    eye = jnp.eye(gh, dtype=F32)

    def bd_in(t):
        t = t.reshape(2, S5_HALVES, gh, S5_STATE, S5_GROUP)
        return jnp.einsum('dfgpn,gh->dfgnhp', t, eye).reshape(2, S5_HALVES, LANE, S5_PH)

    def bd_out(t):
        t = t.reshape(2, S5_HALVES, gh, S5_GROUP, S5_STATE)
        return jnp.einsum('dfgnp,gh->dfgphn', t, eye).reshape(2, S5_HALVES, S5_PH, LANE)

    bbd = jnp.stack([bd_in(jnp.real(b_bar)), bd_in(jnp.imag(b_bar))], axis=1).astype(BF16)
    cbd = jnp.stack([bd_out(jnp.real(c_mat)), bd_out(-jnp.imag(c_mat))], axis=1).astype(BF16)
    flat = lambda z: jnp.stack([jnp.real(z), jnp.imag(z)], axis=1).reshape(2, 2, S5_P)
    return bbd, cbd, flat(lam_bar), flat(lam_t)


def _s5_kernel(ua_ref, ub_ref, s0_ref, bbd_ref, cbd_ref, lam_ref, lamt_ref, ya_ref, yb_ref, sfin_ref,
               sloc_scr, sinit_scr):
    T = S5_CHUNK
    PH = S5_PH
    M = ua_ref.shape[1] // T
    nt = M // SUBLANE
    rid = lax.broadcasted_iota(jnp.int32, (SUBLANE, PH), 0)
    combos = [(hf, d) for hf in range(S5_HALVES) for d in (0, 1)]
    u_refs = (ua_ref, ub_ref)
    y_refs = (ya_ref, yb_ref)
    re_sl = lambda hf: slice(hf * PH, (hf + 1) * PH)
    im_sl = lambda hf: slice(S5_P + hf * PH, S5_P + (hf + 1) * PH)
    order = lambda d: list(range(T)) if d == 0 else list(range(T - 1, -1, -1))

    def drive(hf, d, s):
        us = u_refs[hf][0, pl.ds(s, M, stride=T), :].astype(BF16)
        return _bdot(us, bbd_ref[d, 0, hf]), _bdot(us, bbd_ref[d, 1, hf])

    def advance(hf, d, s, sr, si):
        lr = lam_ref[d, 0:1, re_sl(hf)]
        li = lam_ref[d, 1:2, re_sl(hf)]
        br, bi = drive(hf, d, s)
        return lr * sr - li * si + br, lr * si + li * sr + bi

    for q, (hf, d) in enumerate(combos):
        steps = order(d)
        sr, si = drive(hf, d, steps[0])
        for s in steps[1:]:
            sr, si = advance(hf, d, s, sr, si)
        sloc_scr[q, :, :PH] = sr
        sloc_scr[q, :, PH:] = si

    def scan_tile(n, carry):
        new = []
        for q, (hf, d) in enumerate(combos):
            cr, ci = carry[2 * q], carry[2 * q + 1]
            ar = lamt_ref[d, 0:1, re_sl(hf)]
            ai = lamt_ref[d, 1:2, re_sl(hf)]
            base = pl.multiple_of((n if d == 0 else nt - 1 - n) * SUBLANE, SUBLANE)
            lr_t = sloc_scr[q, pl.ds(base, SUBLANE), :PH]
            li_t = sloc_scr[q, pl.ds(base, SUBLANE), PH:]
            out_r = jnp.zeros((SUBLANE, PH), F32)
            out_i = jnp.zeros((SUBLANE, PH), F32)
            for r in (range(SUBLANE) if d == 0 else range(SUBLANE - 1, -1, -1)):
                out_r = jnp.where(rid == r, cr, out_r)
                out_i = jnp.where(rid == r, ci, out_i)
                cr, ci = (ar * cr - ai * ci + lr_t[r:r + 1], ar * ci + ai * cr + li_t[r:r + 1])
            sinit_scr[q, pl.ds(base, SUBLANE), :PH] = out_r
            sinit_scr[q, pl.ds(base, SUBLANE), PH:] = out_i
            new += [cr, ci]
        return tuple(new)

    init = []
    for hf, d in combos:
        init += [s0_ref[0, d:d + 1, re_sl(hf)], s0_ref[0, d:d + 1, im_sl(hf)]]
    fin = lax.fori_loop(0, nt, scan_tile, tuple(init))
    for q, (hf, d) in enumerate(combos):
        sfin_ref[0, d:d + 1, re_sl(hf)] = fin[2 * q]
        sfin_ref[0, d:d + 1, im_sl(hf)] = fin[2 * q + 1]

    for q, (hf, d) in enumerate(combos):
        sr = sinit_scr[q, :, :PH]
        si = sinit_scr[q, :, PH:]
        for s in order(d):
            sr, si = advance(hf, d, s, sr, si)
            ys = (_bdot(sr.astype(BF16), cbd_ref[d, 0, hf]) + _bdot(si.astype(BF16), cbd_ref[d, 1, hf]))
            if d == 0:
                y_refs[hf][0, pl.ds(s, M, stride=T), :] = ys
            else:
                y_refs[hf][0, pl.ds(s, M, stride=T), :] += ys


def _s5_pass(ua, ub, ops, s0):
    bbd, cbd, lam, lam_t = ops
    B, L, _ = ua.shape
    M = L // S5_CHUNK
    full = lambda nd: (lambda b: (0,) * nd)
    tok = pl.BlockSpec((1, L, LANE), lambda b: (b, 0, 0))
    st = pl.BlockSpec((1, 2, 2 * S5_P), lambda b: (b, 0, 0))
    return pl.pallas_call(
        _s5_kernel,
        grid=(B,),
        in_specs=[tok, tok, st,
                  pl.BlockSpec(bbd.shape, full(5)),
                  pl.BlockSpec(cbd.shape, full(5)),
                  pl.BlockSpec((2, 2, S5_P), full(3)),
                  pl.BlockSpec((2, 2, S5_P), full(3))],
        out_specs=[tok, tok, st],
        out_shape=[jax.ShapeDtypeStruct((B, L, LANE), F32),
                   jax.ShapeDtypeStruct((B, L, LANE), F32),
                   jax.ShapeDtypeStruct((B, 2, 2 * S5_P), F32)],
        scratch_shapes=[pltpu.VMEM((2 * S5_HALVES, M, 2 * S5_PH), F32),
                        pltpu.VMEM((2 * S5_HALVES, M, 2 * S5_PH), F32)],
        compiler_params=_cparams("parallel"),
        name="s5",
    )(ua, ub, s0, bbd, cbd, lam, lam_t)


def _gelu_tanh(v):
    return 0.5 * v * (1.0 + jnp.tanh(math.sqrt(2.0 / math.pi) * (v + 0.044715 * v * v * v)))


def _outproj0_kernel(hy_ref, ya_ref, yb_ref, ua_ref, ub_ref, d_ref, gw_ref, gb_ref, wa_ref, wb_ref,
                     x_ref, m_ref, o_ref):
    ys = (jnp.concatenate([ya_ref[0], yb_ref[0]], axis=1)
          + d_ref[...] * jnp.concatenate([ua_ref[0], ub_ref[0]], axis=1))
    glu = _bdot(_gelu_tanh(ys).astype(BF16), gw_ref[...]) + gb_ref[...]
    s5 = glu[:, :S5_D] * _sigmoid(glu[:, S5_D:])
    hy = jnp.concatenate([hy_ref[0, a] for a in range(hy_ref.shape[1])], axis=1)
    y = lax.dot_general(hy, wa_ref[...], _TN, preferred_element_type=F32)
    y = y + _bdot(s5.astype(BF16), wb_ref[...])
    o_ref[0] = x_ref[0] + m_ref[0] * y


def _outproj0(hy, ys, u, d, glu_w, glu_b, w_out, x, m2):
    B, L, D = x.shape
    tl = _pick(L, 512, 256, 128)
    lb = hy.shape[3]
    kb = tl // lb
    full = lambda b, i: (0, 0)
    tok = lambda w: pl.BlockSpec((1, tl, w), lambda b, i: (b, i, 0))
    return pl.pallas_call(
        _outproj0_kernel,
        grid=(B, L // tl),
        in_specs=[pl.BlockSpec((1, kb, HY_D, lb), lambda b, i: (b, i, 0, 0)),
                  tok(LANE), tok(LANE), tok(LANE), tok(LANE),
                  pl.BlockSpec((1, S5_D), full),
                  pl.BlockSpec((S5_D, 2 * S5_D), full),
                  pl.BlockSpec((1, 2 * S5_D), full),
                  pl.BlockSpec((HY_D, D), full),
                  pl.BlockSpec((S5_D, D), full),
                  tok(D),
                  pl.BlockSpec((1, 1, D), lambda b, i: (b, 0, 0))],
        out_specs=tok(D),
        out_shape=jax.ShapeDtypeStruct((B, L, D), F32),
        compiler_params=_cparams("parallel", "parallel"),
        name="outproj0",
    )(hy, ys[0], ys[1], u[0], u[1], d.reshape(1, S5_D), glu_w.astype(BF16), glu_b.reshape(1, -1),
      w_out[:HY_D].astype(BF16), w_out[HY_D:].astype(BF16), x, m2)


def _moe_kernel(x_ref, g_ref, sc_ref, sh_ref, m_ref, wrh_ref, wrl_ref, br_ref, tril_ref,
                wg_ref, wu_ref, wd_ref, fg_ref, o_ref, hs_scr, gs_scr, pt_scr, ys_scr, seg_smem,
                *, ne, final_norm):
    j = pl.program_id(2)
    nj = pl.num_programs(2)
    tl = x_ref.shape[1]
    S = hs_scr.shape[0]

    @pl.when(j == 0)
    def _():
        hn = _rms_mod(x_ref[0], g_ref[...], sc_ref[0], sh_ref[0])
        hi = hn.astype(BF16)
        lo = (hn - hi.astype(F32)).astype(BF16)
        wrh = wrh_ref[...]
        logits = _bdot(hi, wrh) + _bdot(lo, wrh) + _bdot(hi, wrl_ref[...]) + br_ref[...]
        lane = lax.broadcasted_iota(jnp.int32, logits.shape, 1)
        neg = -jnp.inf
        lgm = jnp.where(lane < MOE_GROUPS, logits, neg)
        gmax = jnp.max(lgm, axis=1, keepdims=True)
        p_top = 1.0 / jnp.sum(jnp.exp(lgm - gmax), axis=1, keepdims=True)
        gidx = jnp.min(jnp.where(lgm == gmax, lane, LANE), axis=1, keepdims=True)
        elane = lane - MOE_GROUPS
        in_group = (elane >= gidx * MOE_EPG) & (elane < (gidx + 1) * MOE_EPG)
        lem = jnp.where(in_group, logits, neg)
        v1 = jnp.max(lem, axis=1, keepdims=True)
        i1 = jnp.min(jnp.where(lem == v1, lane, LANE), axis=1, keepdims=True)
        lem2 = jnp.where(lane == i1, neg, lem)
        v2 = jnp.max(lem2, axis=1, keepdims=True)
        i2 = jnp.min(jnp.where(lem2 == v2, lane, LANE), axis=1, keepdims=True)
        e2 = jnp.exp(v2 - v1)
        w1 = p_top / (1.0 + e2)
        gate = jnp.where(lane == i1, w1, jnp.where(lane == i2, w1 * e2, 0.0))
        onehot = (lane == gidx).astype(F32)
        rank = _bdot(tril_ref[...], onehot.astype(BF16))
        off = jnp.int32(0)
        dest = jnp.zeros((tl, 1), F32)
        for g in range(MOE_GROUPS):
            col = onehot[:, g:g + 1]
            n_g = jnp.sum(col).astype(jnp.int32)
            seg_smem[g] = off
            seg_smem[MOE_GROUPS + g] = n_g
            dest = dest + col * (off.astype(F32) + rank[:, g:g + 1])
            off = off + ((n_g + MOE_ALIGN - 1) // MOE_ALIGN) * MOE_ALIGN
        slot = lax.broadcasted_iota(jnp.int32, (tl, S), 1)
        pt = (slot == dest.astype(jnp.int32)).astype(BF16)
        pt_scr[...] = pt
        hs_scr[...] = lax.dot_general(pt, hi, _TN, preferred_element_type=F32).astype(BF16)
        ghi = gate.astype(BF16)
        glo = (gate - ghi.astype(F32)).astype(BF16)
        gs_scr[...] = (lax.dot_general(pt, ghi, _TN, preferred_element_type=F32)
                       + lax.dot_general(pt, glo, _TN, preferred_element_type=F32))
        ys_scr[...] = jnp.zeros_like(ys_scr)

    grp = j // (MOE_EPG // ne)
    start = pl.multiple_of(seg_smem[grp], MOE_ALIGN)
    units = (seg_smem[MOE_GROUPS + grp] + MOE_UNIT - 1) // MOE_UNIT
    lane1 = lax.broadcasted_iota(jnp.int32, (1, LANE), 1)

    def block(r0, rows):
        hs = hs_scr[pl.ds(r0, rows), :]
        gsb = gs_scr[pl.ds(r0, rows), :]
        acc = None
        for e in range(ne):
            a = _bdot(hs, wg_ref[e])
            u = _bdot(hs, wu_ref[e])
            gcol = jnp.sum(jnp.where(lane1 == MOE_GROUPS + j * ne + e, gsb, 0.0), axis=1, keepdims=True)
            y = _bdot((a * _sigmoid(a) * u * gcol).astype(BF16), wd_ref[e])
            acc = y if acc is None else acc + y
        ys_scr[pl.ds(r0, rows), :] += acc

    @pl.when(units <= 2)
    def _():
        block(start, 2 * MOE_UNIT)

    @pl.when(units == 3)
    def _():
        block(start, 3 * MOE_UNIT)

    @pl.when(units >= 4)
    def _():
        def body(i, carry):
            block(pl.multiple_of(start + i * (2 * MOE_UNIT), MOE_ALIGN), 2 * MOE_UNIT)
            return carry

        lax.fori_loop(0, (units + 1) // 2, body, 0)

    @pl.when(j == nj - 1)
    def _():
        moe = _bdot(pt_scr[...], ys_scr[...].astype(BF16))
        out = x_ref[0] + m_ref[0] * moe
        if final_norm:
            ms = jnp.mean(out * out, axis=-1, keepdims=True)
            out = out * lax.rsqrt(ms + EPS) * fg_ref[...]
        o_ref[0] = out


def _moe(x, g2, sc, sh, m5, wrh, wrl, br, wg, wu, wd, final_g=None):
    B, L, D = x.shape
    ne = 4
    tl = _pick(L, 1024, 512, 256, 128)
    S = tl + 3 * MOE_UNIT
    final_norm = final_g is not None
    fg = (final_g if final_norm else jnp.ones((D,), F32)).reshape(1, D)
    idx = jnp.arange(tl, dtype=jnp.int32)
    tril = (idx[None, :] < idx[:, None]).astype(BF16)
    full = lambda b, i, j: (0, 0)
    tok = pl.BlockSpec((1, tl, D), lambda b, i, j: (b, i, 0))
    vec = pl.BlockSpec((1, 1, D), lambda b, i, j: (b, 0, 0))
    wspec = lambda k, n: pl.BlockSpec((ne, k, n), lambda b, i, j: (j, 0, 0))
    return pl.pallas_call(
        functools.partial(_moe_kernel, ne=ne, final_norm=final_norm),
        grid=(B, L // tl, MOE_EXPERTS // ne),
        in_specs=[tok, pl.BlockSpec((1, D), full), vec, vec, vec,
                  pl.BlockSpec((D, LANE), full), pl.BlockSpec((D, LANE), full), pl.BlockSpec((1, LANE), full),
                  pl.BlockSpec((tl, tl), full),
                  wspec(D, MOE_HIDDEN), wspec(D, MOE_HIDDEN), wspec(MOE_HIDDEN, D),
                  pl.BlockSpec((1, D), full)],
        out_specs=tok,
        out_shape=jax.ShapeDtypeStruct((B, L, D), F32),
        scratch_shapes=[pltpu.VMEM((S, D), BF16),
                        pltpu.VMEM((S, LANE), F32),
                        pltpu.VMEM((tl, S), BF16),
                        pltpu.VMEM((S, D), F32),
                        pltpu.SMEM((2 * MOE_GROUPS,), jnp.int32)],
        compiler_params=_cparams("parallel", "parallel", "arbitrary"),
        name="moe",
    )(x, g2, sc, sh, m5, wrh, wrl, br, tril, wg, wu, wd, fg)


def _moe_params(wg, bg, we, be, w_gate, w_up, w_down):
    D = wg.shape[0]
    pad = LANE - MOE_GROUPS - MOE_EXPERTS
    wr = jnp.pad(jnp.concatenate([wg, we], axis=1), ((0, 0), (0, pad)))
    br = jnp.pad(jnp.concatenate([bg, be]), (0, pad)).reshape(1, LANE)
    wrh = wr.astype(BF16)
    wrl = (wr - wrh.astype(F32)).astype(BF16)
    return (wrh, wrl, br,
            w_gate.reshape(MOE_EXPERTS, D, MOE_HIDDEN).astype(BF16),
            w_up.reshape(MOE_EXPERTS, D, MOE_HIDDEN).astype(BF16),
            w_down.reshape(MOE_EXPERTS, MOE_HIDDEN, D).astype(BF16))


def _inproj1_kernel(x_ref, g_ref, sc_ref, sh_ref, w_ref, o_ref, h_scr):
    @pl.when(pl.program_id(2) == 0)
    def _():
        h_scr[...] = _rms_mod(x_ref[0], g_ref[...], sc_ref[0], sh_ref[0]).astype(BF16)

    o_ref[0] = _bdot(h_scr[...], w_ref[...]).astype(o_ref.dtype)


def _inproj1(x, g, sc, sh, w):
    B, L, D = x.shape
    N = w.shape[1]
    tl = _pick(L, 1024, 512, 256, 128)
    tn = _pick(N, 2560, 1280, 1024, 512)
    return pl.pallas_call(
        _inproj1_kernel,
        grid=(B, L // tl, N // tn),
        in_specs=[pl.BlockSpec((1, tl, D), lambda b, i, j: (b, i, 0)),
                  pl.BlockSpec((1, D), lambda b, i, j: (0, 0)),
                  pl.BlockSpec((1, 1, D), lambda b, i, j: (b, 0, 0)),
                  pl.BlockSpec((1, 1, D), lambda b, i, j: (b, 0, 0)),
                  pl.BlockSpec((D, tn), lambda b, i, j: (0, j))],
        out_specs=pl.BlockSpec((1, tl, tn), lambda b, i, j: (b, i, j)),
        out_shape=jax.ShapeDtypeStruct((B, L, N), BF16),
        scratch_shapes=[pltpu.VMEM((tl, D), BF16)],
        compiler_params=_cparams("parallel", "parallel", "arbitrary"),
        name="inproj1",
    )(x, g, sc, sh, w)


def _gla_masks(C):
    t = np.arange(C)[:, None]
    s = np.arange(C)[None, :]
    fwd = []
    hs = 1
    while hs < C:
        fwd.append(((t // (2 * hs)) == (s // (2 * hs))) & ((t & hs) != 0) & ((s & hs) == 0))
        hs *= 2
    fwd = np.stack(fwd).astype(np.float32)
    return jnp.asarray(np.stack([fwd, fwd.transpose(0, 2, 1)])), jnp.asarray(np.eye(C, dtype=np.float32))


def _gla_direction(pq, v, z, lb, masks, eye, st_ref, rev):
    C = pq.shape[0]
    heads = [slice(h * LANE, (h + 1) * LANE) for h in range(C_HEADS)]
    q = pq * _sigmoid(pq)
    f = lb + (1.0 - lb) * _sigmoid(z)
    k = 1.0 - f
    g = jnp.log2(f)
    row = lax.broadcasted_iota(jnp.int32, (C, C_D), 0)
    qb = q.astype(BF16)
    kb = k.astype(BF16)
    att = [eye * lax.dot_general(qb[:, hd], kb[:, hd], _NT, preferred_element_type=F32) for hd in heads]
    pinc = g
    tot = g
    hs = 1
    lvl = 0
    while hs < C:
        if rev:
            aq = tot - pinc + g
            ak = pinc - g
        else:
            aq = pinc
            ak = tot - pinc
        qe = (q * jnp.exp2(aq)).astype(BF16)
        ke = (k * jnp.exp2(ak)).astype(BF16)
        for h, hd in enumerate(heads):
            blk = lax.dot_general(qe[:, hd], ke[:, hd], _NT, preferred_element_type=F32)
            att[h] = jnp.where(masks[lvl], blk, att[h])
        if hs < SUBLANE:
            odd = (row & hs) != 0
            t3 = tot.reshape(C // SUBLANE, SUBLANE, C_D)
            tprev = pltpu.roll(t3, hs, 1).reshape(C, C_D)
            tnext = pltpu.roll(t3, SUBLANE - hs, 1).reshape(C, C_D)
            pinc = pinc + jnp.where(odd, tprev, 0.0)
            tot = tot + jnp.where(odd, tprev, tnext)
        else:
            w = hs // SUBLANE
            tile = lambda a, i: a[i * SUBLANE:(i + 1) * SUBLANE]
            new_p, new_t = [], []
            for i in range(C // SUBLANE):
                if (i // w) % 2 == 1:
                    new_p.append(tile(pinc, i) + tile(tot, i - w))
                    new_t.append(tile(tot, i) + tile(tot, i - w))
                else:
                    new_p.append(tile(pinc, i))
                    new_t.append(tile(tot, i) + tile(tot, i + w))
            pinc = jnp.concatenate(new_p, axis=0)
            tot = jnp.concatenate(new_t, axis=0)
        hs *= 2
        lvl += 1
    if rev:
        q_dec = tot - pinc + g
        k_dec = pinc - g
    else:
        q_dec = pinc
        k_dec = tot - pinc
    vb = v.astype(BF16)
    qd = (q * jnp.exp2(q_dec)).astype(BF16)
    kd = (k * jnp.exp2(k_dec)).astype(BF16)
    keep = jnp.exp2(tot[0:1, :])
    outs = []
    for h, hd in enumerate(heads):
        st = st_ref[h]
        o = _bdot(att[h].astype(BF16), vb[:, hd])
        o = o + lax.dot_general(qd[:, hd], st.astype(BF16), _NT, preferred_element_type=F32)
        upd = lax.dot_general(vb[:, hd], kd[:, hd], _TN, preferred_element_type=F32)
        st_ref[h] = st * keep[:, hd] + upd
        outs.append(o)
    return jnp.concatenate(outs, axis=1)


def _gla_kernel(qf_ref, if_ref, zf_ref, qb_ref, ib_ref, zb_ref, lb_ref, mk_ref, eye_ref, s0_ref,
                of_ref, ob_ref, sfin_ref, st_scr):
    c = pl.program_id(1)

    @pl.when(c == 0)
    def _():
        st_scr[...] = s0_ref[0]

    eye = eye_ref[...]
    nlev = mk_ref.shape[1]
    for d, (q_ref, i_ref, z_ref, o_ref) in enumerate(((qf_ref, if_ref, zf_ref, of_ref),
                                                      (qb_ref, ib_ref, zb_ref, ob_ref))):
        masks = [mk_ref[d, lv] > 0.5 for lv in range(nlev)]
        o = _gla_direction(q_ref[0].astype(F32), i_ref[0].astype(F32), z_ref[0].astype(F32),
                           lb_ref[d:d + 1, :], masks, eye, st_scr.at[d], rev=(d == 1))
        o_ref[0] = o.astype(o_ref.dtype)

    @pl.when(c == pl.num_programs(1) - 1)
    def _():
        sfin_ref[0] = st_scr[...]


def _gla(p, lb, s0):
    B, L, _ = p.shape
    C = GLA_CHUNK
    nch = L // C
    masks, eye = _gla_masks(C)
    fcol = lambda k: pl.BlockSpec((1, C, C_D), lambda b, c: (b, c, k))
    bcol = lambda k: pl.BlockSpec((1, C, C_D), lambda b, c: (b, nch - 1 - c, k))
    st_spec = pl.BlockSpec((1, 2, C_HEADS, LANE, LANE), lambda b, c: (b, 0, 0, 0, 0))
    return pl.pallas_call(
        _gla_kernel,
        grid=(B, nch),
        in_specs=[fcol(0), fcol(2), fcol(3), bcol(0), bcol(2), bcol(4),
                  pl.BlockSpec((2, C_D), lambda b, c: (0, 0)),
                  pl.BlockSpec(masks.shape, lambda b, c: (0, 0, 0, 0)),
                  pl.BlockSpec((C, C), lambda b, c: (0, 0)),
                  st_spec],
        out_specs=[fcol(0), bcol(0), st_spec],
        out_shape=[jax.ShapeDtypeStruct((B, L, C_D), BF16),
                   jax.ShapeDtypeStruct((B, L, C_D), BF16),
                   jax.ShapeDtypeStruct((B, 2, C_HEADS, LANE, LANE), F32)],
        scratch_shapes=[pltpu.VMEM((2, C_HEADS, LANE, LANE), F32)],
        compiler_params=_cparams("parallel", "arbitrary"),
        name="gla",
    )(p, p, p, p, p, p, lb, masks, eye, s0)


def _outproj1_kernel(of_ref, ob_ref, g_ref, ng_ref, w_ref, x_ref, m_ref, o_ref):
    o = of_ref[0].astype(F32) + ob_ref[0].astype(F32)
    parts = []
    for h in range(C_HEADS):
        oh = o[:, h * LANE:(h + 1) * LANE]
        ms = jnp.mean(oh * oh, axis=-1, keepdims=True)
        parts.append(oh * lax.rsqrt(ms + EPS))
    on = jnp.concatenate(parts, axis=1) * ng_ref[...] * _sigmoid(g_ref[0].astype(F32))
    y = _bdot(on.astype(BF16), w_ref[...])
    o_ref[0] = x_ref[0] + m_ref[0] * y


def _outproj1(o_f, o_b, p, norm_g, w_out, x, m2):
    B, L, D = x.shape
    tl = _pick(L, 512, 256, 128)
    full = lambda b, i: (0, 0)
    tok = pl.BlockSpec((1, tl, C_D), lambda b, i: (b, i, 0))
    return pl.pallas_call(
        _outproj1_kernel,
        grid=(B, L // tl),
        in_specs=[tok, tok,
                  pl.BlockSpec((1, tl, C_D), lambda b, i: (b, i, 1)),
                  pl.BlockSpec((1, C_D), full),
                  pl.BlockSpec((C_D, D), full),
                  pl.BlockSpec((1, tl, D), lambda b, i: (b, i, 0)),
                  pl.BlockSpec((1, 1, D), lambda b, i: (b, 0, 0))],
        out_specs=pl.BlockSpec((1, tl, D), lambda b, i: (b, i, 0)),
        out_shape=jax.ShapeDtypeStruct((B, L, D), F32),
        compiler_params=_cparams("parallel", "parallel"),
        name="outproj1",
    )(o_f, o_b, p, norm_g.reshape(1, C_D), w_out.astype(BF16), x, m2)


def _mods(cmat, w, b, nb):
    R = cmat.shape[0]
    pad = (-R) % SUBLANE
    m = _modvec(jnp.pad(cmat, ((0, pad), (0, 0))), w, b)[:R]
    m = jnp.broadcast_to(m, (nb, m.shape[1])) if R == 1 else m
    return [m[:, None, k * D_MODEL:(k + 1) * D_MODEL] for k in range(6)]


def kernel(x, c, ctx, c_ctx, mod_w, mod_b, norm1_g, norm2_g, final_g,
           ab_w_in, ab_w_out, hy_conv_w, hy_conv_b, hy_fw1, hy_fb1, hy_ff1,
           hy_fw2, hy_fb2, hy_ff2, hy_fw3, hy_bias,
           s5_lam_re, s5_lam_im, s5_log_step, s5_b_re, s5_b_im, s5_c_re, s5_c_im,
           s5_d, s5_glu_w, s5_glu_b,
           c_w_in, c_w_out, c_lower_bounds, c_norm_g,
           moe_wg, moe_bg, moe_we, moe_be, moe_w_gate, moe_w_up, moe_w_down):
    B, L, D = x.shape
    row = lambda t: t.reshape(1, -1)

    m = _mods(c, mod_w[0], mod_b[0], B)
    mc = _mods(c_ctx[None, :], mod_w[0], mod_b[0], B)
    w_in = ab_w_in[0]
    wt_hy = w_in[:, :3 * HY_D].T.astype(BF16)
    w_s5 = w_in[:, 3 * HY_D:].astype(BF16)
    g1 = row(norm1_g[0])
    pt_c, *u_c = _inproj0(ctx, g1, 1.0 + mc[1], mc[0], wt_hy, w_s5)
    pt_l, *u_l = _inproj0(x, g1, 1.0 + m[1], m[0], wt_hy, w_s5)

    ops = _s5_operators(s5_lam_re[0], s5_lam_im[0], s5_log_step[0], s5_b_re[0], s5_b_im[0],
                        s5_c_re[0], s5_c_im[0])
    *ys_c, s_fin = _s5_pass(*u_c, ops, jnp.zeros((B, 2, 2 * S5_P), F32))
    *ys_l, _ = _s5_pass(*u_l, ops, s_fin)

    filt = (hy_fw1[0], hy_fb1[0], hy_ff1[0], hy_fw2[0], hy_fb2[0], hy_ff2[0], hy_fw3[0])
    hy_c = _hyena(pt_c, hy_conv_w[0], hy_conv_b[0], hy_bias[0], filt)
    hy_l = _hyena(pt_l, hy_conv_w[0], hy_conv_b[0], hy_bias[0], filt)

    x = _outproj0(hy_l, ys_l, u_l, s5_d[0], s5_glu_w[0], s5_glu_b[0], ab_w_out[0], x, m[2])
    ctx = _outproj0(hy_c, ys_c, u_c, s5_d[0], s5_glu_w[0], s5_glu_b[0], ab_w_out[0], ctx, mc[2])

    mp = _moe_params(moe_wg[0], moe_bg[0], moe_we[0], moe_be[0],
                     moe_w_gate[0], moe_w_up[0], moe_w_down[0])
    g2 = row(norm2_g[0])
    x = _moe(x, g2, 1.0 + m[4], m[3], m[5], *mp)
    ctx = _moe(ctx, g2, 1.0 + mc[4], mc[3], mc[5], *mp)

    m = _mods(c, mod_w[1], mod_b[1], B)
    mc = _mods(c_ctx[None, :], mod_w[1], mod_b[1], B)
    sm = jax.nn.softmax(c_lower_bounds.astype(F32), axis=1)
    lower = (jnp.cumsum(sm, axis=1) - sm[:, :1])[:, 1]
    g1 = row(norm1_g[1])
    w1 = c_w_in[0].astype(BF16)
    p_c = _inproj1(ctx, g1, 1.0 + mc[1], mc[0], w1)
    p_l = _inproj1(x, g1, 1.0 + m[1], m[0], w1)
    zeros = jnp.zeros((B, 2, C_HEADS, C_HEAD_DIM, C_HEAD_DIM), F32)
    _, _, s_ctx = _gla(p_c, lower, zeros)
    o_f, o_b, _ = _gla(p_l, lower, s_ctx)
    x = _outproj1(o_f, o_b, p_l, c_norm_g[0], c_w_out[0], x, m[2])

    mp = _moe_params(moe_wg[1], moe_bg[1], moe_we[1], moe_be[1],
                     moe_w_gate[1], moe_w_up[1], moe_w_down[1])
    return _moe(x, row(norm2_g[1]), 1.0 + m[4], m[3], m[5], *mp, final_g=final_g)
```

```python
import functools
import math

import numpy as np
import jax
import jax.numpy as jnp
from jax import lax
from jax.experimental import pallas as pl
from jax.experimental.pallas import tpu as pltpu

F32 = jnp.float32
BF16 = jnp.bfloat16
EPS = 1e-6
HIGHEST = lax.Precision.HIGHEST

D_MODEL = 1024
HY_D = 768
HY_ORDER = 2
HY_EMB = 33
HY_BANDS = (HY_EMB - 1) // 2
HY_DECAY_SHORT = 0.3
HY_DECAY_LONG = 1.5
HY_TARGET = 1e-2
S5_D = 256
S5_GROUP = 16
S5_GROUPS = S5_D // S5_GROUP
S5_STATE = 64
S5_P = S5_GROUPS * S5_STATE
S5_HALVES = S5_D // 128
S5_PH = S5_P // S5_HALVES
C_HEADS = 8
C_HEAD_DIM = 128
C_D = C_HEADS * C_HEAD_DIM
MOE_GROUPS = 4
MOE_EPG = 8
MOE_EXPERTS = MOE_GROUPS * MOE_EPG
MOE_HIDDEN = 256
MOE_UNIT = 128
MOE_ALIGN = 16

LANE = 128
SUBLANE = 8
HY_BLK = 512
S5_CHUNK = 16
GLA_CHUNK = 64
VMEM_LIMIT = 56 * 1024 * 1024


def _cparams(*sem):
    return pltpu.CompilerParams(dimension_semantics=sem, vmem_limit_bytes=VMEM_LIMIT)


def _pick(n, *cands):
    for c in cands:
        if n % c == 0:
            return c
    return n


def _rms_mod(xv, g, sc, sh):
    ms = jnp.mean(xv * xv, axis=-1, keepdims=True)
    return xv * lax.rsqrt(ms + EPS) * g * sc + sh


def _sigmoid(v):
    return 1.0 / (1.0 + jnp.exp(-v))


def _bdot(a, b):
    return jnp.dot(a, b, preferred_element_type=F32)


_NT = (((1,), (1,)), ((), ()))
_TN = (((0,), (0,)), ((), ()))


def _mm_kernel(a_ref, b_ref, o_ref):
    o_ref[...] = _bdot(a_ref[...].astype(BF16), b_ref[...]).astype(o_ref.dtype)


def _mm(a, b, out_dtype=F32, name="mm"):
    M, K = a.shape
    N = b.shape[1]
    tm = _pick(M, 768, 512, 256, 128)
    tn = _pick(N, 512, 256, 128)
    return pl.pallas_call(
        _mm_kernel,
        grid=(M // tm, N // tn),
        in_specs=[pl.BlockSpec((tm, K), lambda i, j: (i, 0)),
                  pl.BlockSpec((K, tn), lambda i, j: (0, j))],
        out_specs=pl.BlockSpec((tm, tn), lambda i, j: (i, j)),
        out_shape=jax.ShapeDtypeStruct((M, N), out_dtype),
        compiler_params=_cparams("parallel", "parallel"),
        name=name,
    )(a, b.astype(BF16))


def _modvec_kernel(c_ref, w_ref, b_ref, o_ref):
    cv = c_ref[...]
    sc = cv * _sigmoid(cv)
    o_ref[...] = jnp.dot(sc, w_ref[...], preferred_element_type=F32, precision=HIGHEST) + b_ref[...]


def _modvec(cvec, w, b):
    R, D = cvec.shape
    N = w.shape[1]
    tn = _pick(N, 512, 256, 128)
    return pl.pallas_call(
        _modvec_kernel,
        grid=(N // tn,),
        in_specs=[pl.BlockSpec((R, D), lambda j: (0, 0)),
                  pl.BlockSpec((D, tn), lambda j: (0, j)),
                  pl.BlockSpec((1, tn), lambda j: (0, j))],
        out_specs=pl.BlockSpec((R, tn), lambda j: (0, j)),
        out_shape=jax.ShapeDtypeStruct((R, N), F32),
        compiler_params=_cparams("parallel"),
        name="modvec",
    )(cvec, w, b.reshape(1, N))


def _inproj0_kernel(x_ref, g_ref, sc_ref, sh_ref, wt_ref, w2_ref, pt_ref, ua_ref, ub_ref):
    h = _rms_mod(x_ref[0], g_ref[...], sc_ref[0], sh_ref[0]).astype(BF16)
    pt_ref[0] = lax.dot_general(wt_ref[...], h, _NT, preferred_element_type=F32).astype(BF16)
    u = _bdot(h, w2_ref[...])
    ua_ref[0] = u[:, :LANE]
    ub_ref[0] = u[:, LANE:]


def _inproj0(x, g, sc, sh, wt_hy, w_s5):
    B, L, D = x.shape
    C = wt_hy.shape[0]
    tl = _pick(L, 512, 256, 128)
    return pl.pallas_call(
        _inproj0_kernel,
        grid=(B, L // tl),
        in_specs=[pl.BlockSpec((1, tl, D), lambda b, i: (b, i, 0)),
                  pl.BlockSpec((1, D), lambda b, i: (0, 0)),
                  pl.BlockSpec((1, 1, D), lambda b, i: (b, 0, 0)),
                  pl.BlockSpec((1, 1, D), lambda b, i: (b, 0, 0)),
                  pl.BlockSpec((C, D), lambda b, i: (0, 0)),
                  pl.BlockSpec((D, S5_D), lambda b, i: (0, 0))],
        out_specs=[pl.BlockSpec((1, C, tl), lambda b, i: (b, 0, i)),
                   pl.BlockSpec((1, tl, LANE), lambda b, i: (b, i, 0)),
                   pl.BlockSpec((1, tl, LANE), lambda b, i: (b, i, 0))],
        out_shape=[jax.ShapeDtypeStruct((B, C, L), BF16),
                   jax.ShapeDtypeStruct((B, L, LANE), F32),
                   jax.ShapeDtypeStruct((B, L, LANE), F32)],
        compiler_params=_cparams("parallel", "parallel"),
        name="inproj0",
    )(x, g, sc, sh, wt_hy, w_s5)


def _shortconv_kernel(p1_ref, p2_ref, p3_ref, w_ref, b_ref, x1_ref, x2_ref, v_ref):
    L = p1_ref.shape[2]
    nb, lb = x1_ref.shape[1], x1_ref.shape[3]
    lane = lax.broadcasted_iota(jnp.int32, (1, L), 1)
    first = lane == 0
    last = lane == L - 1
    for k, (p_ref, o_ref) in enumerate(((p1_ref, x1_ref), (p2_ref, x2_ref), (p3_ref, v_ref))):
        u = p_ref[0].astype(F32)
        w = w_ref[k]
        prev = jnp.where(first, 0.0, pltpu.roll(u, 1, 1))
        nxt = jnp.where(last, 0.0, pltpu.roll(u, L - 1, 1))
        y = (w[:, 0:1] * prev + w[:, 1:2] * u + w[:, 2:3] * nxt + b_ref[k]).astype(BF16)
        for j in range(nb):
            o_ref[0, j] = y[:, j * lb:(j + 1) * lb]


def _shortconv(pt, conv_w, conv_b, lb):
    B, C3, L = pt.shape
    nb = L // lb
    tc = 128
    nc = HY_D // tc
    w = conv_w.T.reshape(3, HY_D, 3)
    b = conv_b.reshape(3, HY_D, 1)
    specs = [pl.BlockSpec((1, tc, L), (lambda b_, i, k=k: (b_, k * nc + i, 0))) for k in range(3)]
    o_spec = pl.BlockSpec((1, nb, tc, lb), lambda b_, i: (b_, 0, i, 0))
    return pl.pallas_call(
        _shortconv_kernel,
        grid=(B, nc),
        in_specs=specs + [pl.BlockSpec((3, tc, 3), lambda b_, i: (0, i, 0)),
                          pl.BlockSpec((3, tc, 1), lambda b_, i: (0, i, 0))],
        out_specs=[o_spec, o_spec, o_spec],
        out_shape=[jax.ShapeDtypeStruct((B, nb, HY_D, lb), BF16)] * 3,
        compiler_params=_cparams("parallel", "parallel"),
        name="shortconv",
    )(pt, pt, pt, w, b)


def _filter_kernel(z_ref, w1_ref, b1_ref, f1_ref, w2_ref, b2_ref, f2_ref, w3_ref, dec_ref, o_ref):
    dot = functools.partial(jnp.dot, preferred_element_type=F32, precision=HIGHEST)
    z = z_ref[...]
    hdn = jnp.sin(f1_ref[...] * (dot(w1_ref[...], z) + b1_ref[...]))
    hdn = jnp.sin(f2_ref[...] * (dot(w2_ref[...], hdn) + b2_ref[...]))
    t = z[0:1, :]
    o_ref[...] = dot(w3_ref[...], hdn) * jnp.exp(-dec_ref[...] * t)


def _hyena_filter(L, fw1, fb1, ff1, fw2, fb2, ff2, fw3):
    pos = jnp.arange(L, dtype=F32)
    t = pos / max(L - 1, 1)
    w = 2.0 * math.pi * pos / L
    bands = jnp.linspace(1e-4, HY_BANDS - 1, HY_BANDS, dtype=F32)
    ang = bands[:, None] * w[None, :]
    z = jnp.concatenate([t[None, :], jnp.cos(ang), -jnp.sin(ang)], axis=0)
    z = jnp.pad(z, ((0, LANE - HY_EMB), (0, 0)))
    w1 = jnp.pad(fw1, ((0, LANE - HY_EMB), (0, 0))).T
    deltas = jnp.abs(jnp.linspace(math.log(HY_TARGET) / HY_DECAY_LONG,
                                  math.log(HY_TARGET) / HY_DECAY_SHORT, HY_D, dtype=F32))
    dec = jnp.tile(deltas, HY_ORDER * 2).reshape(-1, 1)
    nf = fw1.shape[1]
    No = fw3.shape[1]
    tl = _pick(L, 512, 256)
    full = lambda i: (0, 0)
    col = lambda v: v.reshape(nf, 1)
    return pl.pallas_call(
        _filter_kernel,
        grid=(L // tl,),
        in_specs=[pl.BlockSpec((LANE, tl), lambda i: (0, i)),
                  pl.BlockSpec((nf, LANE), full), pl.BlockSpec((nf, 1), full), pl.BlockSpec((nf, 1), full),
                  pl.BlockSpec((nf, nf), full), pl.BlockSpec((nf, 1), full), pl.BlockSpec((nf, 1), full),
                  pl.BlockSpec((No, nf), full), pl.BlockSpec((No, 1), full)],
        out_specs=pl.BlockSpec((No, tl), lambda i: (0, i)),
        out_shape=jax.ShapeDtypeStruct((No, L), F32),
        compiler_params=_cparams("parallel"),
        name="hyena_filter",
    )(z, w1, col(fb1), col(ff1), fw2.T, col(fb2), col(ff2), fw3.T, dec)


def _dft_mats(L, blk):
    N = 2 * L
    W = 64
    j = jnp.arange(2 * L, dtype=jnp.int32)
    kk = (j // (2 * blk)) * blk + j % blk
    is_im = (j // blk) % 2 == 1
    nyq = is_im & (kk == 0)
    hi = jnp.arange(L // W, dtype=jnp.int32) * W
    lo = jnp.arange(W, dtype=jnp.int32)
    alt = (1 - 2 * (lo % 2)).astype(F32)
    th = 2.0 * math.pi / N
    ah = ((hi[:, None] * kk[None, :]) % N).astype(F32) * th
    al = ((lo[:, None] * kk[None, :]) % N).astype(F32) * th
    ch, sh, cl, sl = jnp.cos(ah), jnp.sin(ah), jnp.cos(al), jnp.sin(al)
    c = ch[:, None, :] * cl[None] - sh[:, None, :] * sl[None]
    s = sh[:, None, :] * cl[None] + ch[:, None, :] * sl[None]
    wf = jnp.where(nyq[None, None, :], alt[None, :, None], jnp.where(is_im[None, None, :], -s, c))
    ct = ch.T[:, :, None] * cl.T[:, None, :] - sh.T[:, :, None] * sl.T[:, None, :]
    st = sh.T[:, :, None] * cl.T[:, None, :] + ch.T[:, :, None] * sl.T[:, None, :]
    scale = jnp.where(kk == 0, 1.0 / N, 2.0 / N).astype(F32)[:, None, None]
    wi = jnp.where(nyq[:, None, None], alt[None, None, :], jnp.where(is_im[:, None, None], -st, ct)) * scale
    return wf.reshape(L, 2 * L).astype(BF16), wi.reshape(2 * L, L).astype(BF16)


def _segdft_kernel(a_ref, w_ref, o_ref):
    o_ref[...] = _bdot(a_ref[...].astype(BF16), w_ref[...])


def _segdft(ht, wf, lb):
    R, L = ht.shape
    tm = 256
    return pl.pallas_call(
        _segdft_kernel,
        grid=(R // tm, L // lb),
        in_specs=[pl.BlockSpec((tm, lb), lambda i, m: (i, m)),
                  pl.BlockSpec((lb, 2 * lb), lambda i, m: (0, 0))],
        out_specs=pl.BlockSpec((tm, 2 * lb), lambda i, m: (i, m)),
        out_shape=jax.ShapeDtypeStruct((R, 2 * L), F32),
        compiler_params=_cparams("parallel", "parallel"),
        name="filter_segdft",
    )(ht, wf)


def _filter_blocks_kernel(hf_ref, hb_ref, af_ref, ab_ref, kr_ref, ki_ref, kn_ref, *, nb, lb):
    hf = hf_ref[...]
    hb = hb_ref[...]
    lag = lax.broadcasted_iota(jnp.int32, (1, hf.shape[1]), 1)
    hbz = jnp.where(lag == 0, 0.0, hb)
    nrm = lax.rsqrt(jnp.sum(hf * hf, axis=1, keepdims=True) + jnp.sum(hbz * hbz, axis=1, keepdims=True))
    k = lax.broadcasted_iota(jnp.int32, (1, lb), 1)
    sgn = jnp.where(k % 2 == 0, 1.0, -1.0)
    bin0 = k == 0

    def seg(ref, m):
        re = ref[:, m * 2 * lb:m * 2 * lb + lb]
        imp = ref[:, m * 2 * lb + lb:(m + 1) * 2 * lb]
        return re, jnp.where(bin0, 0.0, imp), imp[:, 0:1]

    def emit(idx, re, im, nyq):
        kr_ref[0, idx] = re * nrm
        ki_ref[0, idx] = im * nrm
        kn_ref[0, idx] = nyq * nrm

    fr, fi, fn = seg(af_ref, 0)
    br, bi, bn = seg(ab_ref, 0)
    b0 = hb[:, 0:1]
    emit(nb - 1, fr + br - b0, fi - bi, fn + bn - b0)
    for d in range(1, nb):
        for ref, taps, conj, idx in ((af_ref, hf, 1.0, nb - 1 + d), (ab_ref, hb, -1.0, nb - 1 - d)):
            r1, i1, n1 = seg(ref, d)
            r0, i0, n0 = seg(ref, d - 1)
            e0 = taps[:, (d - 1) * lb:(d - 1) * lb + 1]
            emit(idx, r1 + sgn * (r0 - e0), conj * (i1 + sgn * i0), n1 + n0 - e0)


def _filter_blocks(ht, seg, lb):
    R, L = ht.shape
    nb = L // lb
    nlag = 2 * nb - 1
    tm = 128
    nc = HY_D // tm
    fwd = lambda i: ((i // nc) * 2 * nc + i % nc, 0)
    bwd = lambda i: ((i // nc) * 2 * nc + nc + i % nc, 0)
    kspec = pl.BlockSpec((1, nlag, tm, lb), lambda i: (i // nc, 0, i % nc, 0))
    return pl.pallas_call(
        functools.partial(_filter_blocks_kernel, nb=nb, lb=lb),
        grid=(HY_ORDER * nc,),
        in_specs=[pl.BlockSpec((tm, L), fwd), pl.BlockSpec((tm, L), bwd),
                  pl.BlockSpec((tm, 2 * L), fwd), pl.BlockSpec((tm, 2 * L), bwd)],
        out_specs=[kspec, kspec, pl.BlockSpec((1, nlag, tm, 1), lambda i: (i // nc, 0, i % nc, 0))],
        out_shape=[jax.ShapeDtypeStruct((HY_ORDER, nlag, HY_D, lb), F32),
                   jax.ShapeDtypeStruct((HY_ORDER, nlag, HY_D, lb), F32),
                   jax.ShapeDtypeStruct((HY_ORDER, nlag, HY_D, 1), F32)],
        compiler_params=_cparams("parallel"),
        name="filter_blocks",
    )(ht, ht, seg, seg)


def _hyconv_kernel(v_ref, g_ref, bias_ref, kr_ref, ki_ref, kn_ref, wf_ref, wi_ref, o_ref,
                   vr_scr, vi_scr, vn_scr, y_scr):
    nb, ct, lb = v_ref.shape[1], v_ref.shape[2], v_ref.shape[3]
    bin0 = lax.broadcasted_iota(jnp.int32, (1, lb), 1) == 0
    v = v_ref[0]
    acc = _bdot(v.reshape(nb * ct, lb), wf_ref[...])
    imp = acc[:, lb:]
    vr_scr[...] = acc[:, :lb].reshape(nb, ct, lb)
    vi_scr[...] = jnp.where(bin0, 0.0, imp).reshape(nb, ct, lb)
    vn_scr[...] = imp[:, 0:1].reshape(nb, ct, 1)

    rt = 16
    nr = ct // rt

    def out_rows(n, carry):
        i = n // nr
        rows = pl.ds(pl.multiple_of((n % nr) * rt, rt), rt)
        yr = jnp.zeros((rt, lb), F32)
        yi = jnp.zeros((rt, lb), F32)
        yn = jnp.zeros((rt, 1), F32)
        for j in range(nb):
            d = i - j + (nb - 1)
            kr = kr_ref[0, d, rows, :]
            ki = ki_ref[0, d, rows, :]
            vr = vr_scr[j, rows, :]
            vi = vi_scr[j, rows, :]
            yr = yr + (vr * kr - vi * ki)
            yi = yi + (vr * ki + vi * kr)
            yn = yn + vn_scr[j, rows, :] * kn_ref[0, d, rows, :]
        y_scr[i, rows, :lb] = yr.astype(BF16)
        y_scr[i, rows, lb:] = jnp.where(bin0, yn, yi).astype(BF16)
        return carry

    lax.fori_loop(0, nb * nr, out_rows, 0)
    conv = _bdot(y_scr[...].reshape(nb * ct, 2 * lb), wi_ref[...]).reshape(nb, ct, lb)
    o_ref[0] = (g_ref[0].astype(F32) * (conv + v.astype(F32) * bias_ref[...])).astype(o_ref.dtype)


def _hyconv(v, gate, bias, kr, ki, kn, order, wf, wi):
    B, nb, C, lb = v.shape
    nlag = kr.shape[1]
    ct = 128
    blk = pl.BlockSpec((1, nb, ct, lb), lambda c, b: (b, 0, c, 0))
    kspec = pl.BlockSpec((1, nlag, ct, lb), lambda c, b: (order, 0, c, 0))
    return pl.pallas_call(
        _hyconv_kernel,
        grid=(C // ct, B),
        in_specs=[blk, blk,
                  pl.BlockSpec((ct, 1), lambda c, b: (c, 0)),
                  kspec, kspec,
                  pl.BlockSpec((1, nlag, ct, 1), lambda c, b: (order, 0, c, 0)),
                  pl.BlockSpec((lb, 2 * lb), lambda c, b: (0, 0)),
                  pl.BlockSpec((2 * lb, lb), lambda c, b: (0, 0))],
        out_specs=blk,
        out_shape=jax.ShapeDtypeStruct((B, nb, C, lb), BF16),
        scratch_shapes=[pltpu.VMEM((nb, ct, lb), F32), pltpu.VMEM((nb, ct, lb), F32),
                        pltpu.VMEM((nb, ct, 1), F32), pltpu.VMEM((nb, ct, 2 * lb), BF16)],
        compiler_params=_cparams("parallel", "parallel"),
        name="hyena_conv",
    )(v, gate, bias.reshape(C, 1), kr, ki, kn, wf, wi)


def _hyena(pt, conv_w, conv_b, hy_bias, filt):
    L = pt.shape[2]
    lb = HY_BLK if L >= 2 * HY_BLK else L // 2
    wf, wi = _dft_mats(lb, lb)
    ht = _hyena_filter(L, *filt)
    kr, ki, kn = _filter_blocks(ht, _segdft(ht, wf, lb), lb)
    x1, x2, v = _shortconv(pt, conv_w, conv_b, lb)
    z = _hyconv(v, x1, hy_bias[0], kr, ki, kn, 0, wf, wi)
    return _hyconv(z, x2, hy_bias[1], kr, ki, kn, 1, wf, wi)


def _s5_operators(lam_re, lam_im, log_step, b_re, b_im, c_re, c_im):
    lam = lax.complex(lam_re.astype(F32), lam_im.astype(F32))
    dt = jnp.exp(log_step.astype(F32))[..., None]
    lam_bar = jnp.exp(lam * dt)
    lam_t = jnp.exp(lam * dt * S5_CHUNK)
    b_bar = ((lam_bar - 1.0) / lam)[..., None] * lax.complex(b_re.astype(F32), b_im.astype(F32))
    c_mat = lax.complex(c_re.astype(F32), c_im.astype(F32))
    gh = S5_GROUPS // S5_HALVES
    eye = jnp.eye(gh, dtype=F32)

    def bd_in(t):
        t = t.reshape(2, S5_HALVES, gh, S5_STATE, S5_GROUP)
        return jnp.einsum('dfgpn,gh->dfgnhp', t, eye).reshape(2, S5_HALVES, LANE, S5_PH)

    def bd_out(t):
        t = t.reshape(2, S5_HALVES, gh, S5_GROUP, S5_STATE)
        return jnp.einsum('dfgnp,gh->dfgphn', t, eye).reshape(2, S5_HALVES, S5_PH, LANE)

    bbd = jnp.stack([bd_in(jnp.real(b_bar)), bd_in(jnp.imag(b_bar))], axis=1).astype(BF16)
    cbd = jnp.stack([bd_out(jnp.real(c_mat)), bd_out(-jnp.imag(c_mat))], axis=1).astype(BF16)
    flat = lambda z: jnp.stack([jnp.real(z), jnp.imag(z)], axis=1).reshape(2, 2, S5_P)
    return bbd, cbd, flat(lam_bar), flat(lam_t)


def _s5_kernel(ua_ref, ub_ref, s0_ref, bbd_ref, cbd_ref, lam_ref, lamt_ref, ya_ref, yb_ref, sfin_ref,
               sloc_scr, sinit_scr, bu_scr):
    T = S5_CHUNK
    PH = S5_PH
    M = ua_ref.shape[1] // T
    nt = M // SUBLANE
    rid = lax.broadcasted_iota(jnp.int32, (SUBLANE, PH), 0)
    combos = [(hf, d) for hf in range(S5_HALVES) for d in (0, 1)]
    u_refs = (ua_ref, ub_ref)
    y_refs = (ya_ref, yb_ref)
    re_sl = lambda hf: slice(hf * PH, (hf + 1) * PH)
    im_sl = lambda hf: slice(S5_P + hf * PH, S5_P + (hf + 1) * PH)
    order = lambda d: list(range(T)) if d == 0 else list(range(T - 1, -1, -1))

    def drive(hf, d, s):
        us = u_refs[hf][0, pl.ds(s, M, stride=T), :].astype(BF16)
        return _bdot(us, bbd_ref[d, 0, hf]), _bdot(us, bbd_ref[d, 1, hf])

    def advance(hf, d, sr, si, br, bi):
        lr = lam_ref[d, 0:1, re_sl(hf)]
        li = lam_ref[d, 1:2, re_sl(hf)]
        return lr * sr - li * si + br, lr * si + li * sr + bi

    for hf, d in combos:
        steps = order(d)
        for n, s in enumerate(steps):
            br, bi = drive(hf, d, s)
            bu_scr[s, :, :PH] = br
            bu_scr[s, :, PH:] = bi
            sr, si = (br, bi) if n == 0 else advance(hf, d, sr, si, br, bi)
        sloc_scr[:, :PH] = sr
        sloc_scr[:, PH:] = si

        ar = lamt_ref[d, 0:1, re_sl(hf)]
        ai = lamt_ref[d, 1:2, re_sl(hf)]

        def scan_tile(n, carry, d=d, ar=ar, ai=ai):
            cr, ci = carry
            base = pl.multiple_of((n if d == 0 else nt - 1 - n) * SUBLANE, SUBLANE)
            lr_t = sloc_scr[pl.ds(base, SUBLANE), :PH]
            li_t = sloc_scr[pl.ds(base, SUBLANE), PH:]
            out_r = jnp.zeros((SUBLANE, PH), F32)
            out_i = jnp.zeros((SUBLANE, PH), F32)
            for r in (range(SUBLANE) if d == 0 else range(SUBLANE - 1, -1, -1)):
                out_r = jnp.where(rid == r, cr, out_r)
                out_i = jnp.where(rid == r, ci, out_i)
                cr, ci = (ar * cr - ai * ci + lr_t[r:r + 1], ar * ci + ai * cr + li_t[r:r + 1])
            sinit_scr[pl.ds(base, SUBLANE), :PH] = out_r
            sinit_scr[pl.ds(base, SUBLANE), PH:] = out_i
            return cr, ci

        cr, ci = lax.fori_loop(0, nt, scan_tile, (s0_ref[0, d:d + 1, re_sl(hf)], s0_ref[0, d:d + 1, im_sl(hf)]))
        sfin_ref[0, d:d + 1, re_sl(hf)] = cr
        sfin_ref[0, d:d + 1, im_sl(hf)] = ci

        sr = sinit_scr[:, :PH]
        si = sinit_scr[:, PH:]
        for s in steps:
            sr, si = advance(hf, d, sr, si, bu_scr[s, :, :PH], bu_scr[s, :, PH:])
            ys = (_bdot(sr.astype(BF16), cbd_ref[d, 0, hf]) + _bdot(si.astype(BF16), cbd_ref[d, 1, hf]))
            if d == 0:
                y_refs[hf][0, pl.ds(s, M, stride=T), :] = ys
            else:
                y_refs[hf][0, pl.ds(s, M, stride=T), :] += ys


def _s5_pass(ua, ub, ops, s0):
    bbd, cbd, lam, lam_t = ops
    B, L, _ = ua.shape
    M = L // S5_CHUNK
    full = lambda nd: (lambda b: (0,) * nd)
    tok = pl.BlockSpec((1, L, LANE), lambda b: (b, 0, 0))
    st = pl.BlockSpec((1, 2, 2 * S5_P), lambda b: (b, 0, 0))
    return pl.pallas_call(
        _s5_kernel,
        grid=(B,),
        in_specs=[tok, tok, st,
                  pl.BlockSpec(bbd.shape, full(5)),
                  pl.BlockSpec(cbd.shape, full(5)),
                  pl.BlockSpec((2, 2, S5_P), full(3)),
                  pl.BlockSpec((2, 2, S5_P), full(3))],
        out_specs=[tok, tok, st],
        out_shape=[jax.ShapeDtypeStruct((B, L, LANE), F32),
                   jax.ShapeDtypeStruct((B, L, LANE), F32),
                   jax.ShapeDtypeStruct((B, 2, 2 * S5_P), F32)],
        scratch_shapes=[pltpu.VMEM((M, 2 * S5_PH), F32),
                        pltpu.VMEM((M, 2 * S5_PH), F32),
                        pltpu.VMEM((S5_CHUNK, M, 2 * S5_PH), F32)],
        compiler_params=_cparams("parallel"),
        name="s5",
    )(ua, ub, s0, bbd, cbd, lam, lam_t)


def _gelu_tanh(v):
    return 0.5 * v * (1.0 + jnp.tanh(math.sqrt(2.0 / math.pi) * (v + 0.044715 * v * v * v)))


def _outproj0_kernel(hy_ref, ya_ref, yb_ref, ua_ref, ub_ref, d_ref, gw_ref, gb_ref, wa_ref, wb_ref,
                     x_ref, m_ref, o_ref):
    ys = (jnp.concatenate([ya_ref[0], yb_ref[0]], axis=1)
          + d_ref[...] * jnp.concatenate([ua_ref[0], ub_ref[0]], axis=1))
    glu = _bdot(_gelu_tanh(ys).astype(BF16), gw_ref[...]) + gb_ref[...]
    s5 = glu[:, :S5_D] * _sigmoid(glu[:, S5_D:])
    hy = jnp.concatenate([hy_ref[0, a] for a in range(hy_ref.shape[1])], axis=1)
    y = lax.dot_general(hy, wa_ref[...], _TN, preferred_element_type=F32)
    y = y + _bdot(s5.astype(BF16), wb_ref[...])
    o_ref[0] = x_ref[0] + m_ref[0] * y


def _outproj0(hy, ys, u, d, glu_w, glu_b, w_out, x, m2):
    B, L, D = x.shape
    tl = _pick(L, 512, 256, 128)
    lb = hy.shape[3]
    kb = tl // lb
    full = lambda b, i: (0, 0)
    tok = lambda w: pl.BlockSpec((1, tl, w), lambda b, i: (b, i, 0))
    return pl.pallas_call(
        _outproj0_kernel,
        grid=(B, L // tl),
        in_specs=[pl.BlockSpec((1, kb, HY_D, lb), lambda b, i: (b, i, 0, 0)),
                  tok(LANE), tok(LANE), tok(LANE), tok(LANE),
                  pl.BlockSpec((1, S5_D), full),
                  pl.BlockSpec((S5_D, 2 * S5_D), full),
                  pl.BlockSpec((1, 2 * S5_D), full),
                  pl.BlockSpec((HY_D, D), full),
                  pl.BlockSpec((S5_D, D), full),
                  tok(D),
                  pl.BlockSpec((1, 1, D), lambda b, i: (b, 0, 0))],
        out_specs=tok(D),
        out_shape=jax.ShapeDtypeStruct((B, L, D), F32),
        compiler_params=_cparams("parallel", "parallel"),
        name="outproj0",
    )(hy, ys[0], ys[1], u[0], u[1], d.reshape(1, S5_D), glu_w.astype(BF16), glu_b.reshape(1, -1),
      w_out[:HY_D].astype(BF16), w_out[HY_D:].astype(BF16), x, m2)


def _moe_kernel(x_ref, g_ref, sc_ref, sh_ref, m_ref, wrh_ref, wrl_ref, br_ref, tril_ref,
                wg_ref, wu_ref, wd_ref, fg_ref, o_ref, hs_scr, gs_scr, pt_scr, ys_scr, seg_smem,
                *, ne, final_norm):
    j = pl.program_id(2)
    nj = pl.num_programs(2)
    tl = x_ref.shape[1]
    S = hs_scr.shape[0]

    @pl.when(j == 0)
    def _():
        hn = _rms_mod(x_ref[0], g_ref[...], sc_ref[0], sh_ref[0])
        hi = hn.astype(BF16)
        lo = (hn - hi.astype(F32)).astype(BF16)
        wrh = wrh_ref[...]
        logits = _bdot(hi, wrh) + _bdot(lo, wrh) + _bdot(hi, wrl_ref[...]) + br_ref[...]
        lane = lax.broadcasted_iota(jnp.int32, logits.shape, 1)
        neg = -jnp.inf
        lgm = jnp.where(lane < MOE_GROUPS, logits, neg)
        gmax = jnp.max(lgm, axis=1, keepdims=True)
        p_top = 1.0 / jnp.sum(jnp.exp(lgm - gmax), axis=1, keepdims=True)
        gidx = jnp.min(jnp.where(lgm == gmax, lane, LANE), axis=1, keepdims=True)
        elane = lane - MOE_GROUPS
        in_group = (elane >= gidx * MOE_EPG) & (elane < (gidx + 1) * MOE_EPG)
        lem = jnp.where(in_group, logits, neg)
        v1 = jnp.max(lem, axis=1, keepdims=True)
        i1 = jnp.min(jnp.where(lem == v1, lane, LANE), axis=1, keepdims=True)
        lem2 = jnp.where(lane == i1, neg, lem)
        v2 = jnp.max(lem2, axis=1, keepdims=True)
        i2 = jnp.min(jnp.where(lem2 == v2, lane, LANE), axis=1, keepdims=True)
        e2 = jnp.exp(v2 - v1)
        w1 = p_top / (1.0 + e2)
        gate = jnp.where(lane == i1, w1, jnp.where(lane == i2, w1 * e2, 0.0))
        onehot = (lane == gidx).astype(F32)
        rank = _bdot(tril_ref[...], onehot.astype(BF16))
        off = jnp.int32(0)
        dest = jnp.zeros((tl, 1), F32)
        for g in range(MOE_GROUPS):
            col = onehot[:, g:g + 1]
            n_g = jnp.sum(col).astype(jnp.int32)
            seg_smem[g] = off
            seg_smem[MOE_GROUPS + g] = n_g
            dest = dest + col * (off.astype(F32) + rank[:, g:g + 1])
            off = off + ((n_g + MOE_ALIGN - 1) // MOE_ALIGN) * MOE_ALIGN
        slot = lax.broadcasted_iota(jnp.int32, (tl, S), 1)
        pt = (slot == dest.astype(jnp.int32)).astype(BF16)
        pt_scr[...] = pt
        hs_scr[...] = lax.dot_general(pt, hi, _TN, preferred_element_type=F32).astype(BF16)
        ghi = gate.astype(BF16)
        glo = (gate - ghi.astype(F32)).astype(BF16)
        gs_scr[...] = (lax.dot_general(pt, ghi, _TN, preferred_element_type=F32)
                       + lax.dot_general(pt, glo, _TN, preferred_element_type=F32))
        ys_scr[...] = jnp.zeros_like(ys_scr)

    grp = j // (MOE_EPG // ne)
    start = pl.multiple_of(seg_smem[grp], MOE_ALIGN)
    units = (seg_smem[MOE_GROUPS + grp] + MOE_UNIT - 1) // MOE_UNIT
    lane1 = lax.broadcasted_iota(jnp.int32, (1, LANE), 1)

    def block(r0, rows):
        hs = hs_scr[pl.ds(r0, rows), :]
        gsb = gs_scr[pl.ds(r0, rows), :]
        acc = None
        for e in range(ne):
            a = _bdot(hs, wg_ref[e])
            u = _bdot(hs, wu_ref[e])
            gcol = jnp.sum(jnp.where(lane1 == MOE_GROUPS + j * ne + e, gsb, 0.0), axis=1, keepdims=True)
            y = _bdot((a * _sigmoid(a) * u * gcol).astype(BF16), wd_ref[e])
            acc = y if acc is None else acc + y
        ys_scr[pl.ds(r0, rows), :] += acc

    @pl.when(units <= 2)
    def _():
        block(start, 2 * MOE_UNIT)

    @pl.when(units == 3)
    def _():
        block(start, 3 * MOE_UNIT)

    @pl.when(units >= 4)
    def _():
        def body(i, carry):
            block(pl.multiple_of(start + i * (2 * MOE_UNIT), MOE_ALIGN), 2 * MOE_UNIT)
            return carry

        lax.fori_loop(0, (units + 1) // 2, body, 0)

    @pl.when(j == nj - 1)
    def _():
        moe = _bdot(pt_scr[...], ys_scr[...].astype(BF16))
        out = x_ref[0] + m_ref[0] * moe
        if final_norm:
            ms = jnp.mean(out * out, axis=-1, keepdims=True)
            out = out * lax.rsqrt(ms + EPS) * fg_ref[...]
        o_ref[0] = out


def _moe(x, g2, sc, sh, m5, wrh, wrl, br, wg, wu, wd, final_g=None):
    B, L, D = x.shape
    ne = 4
    tl = _pick(L, 1024, 512, 256, 128)
    S = tl + 3 * MOE_UNIT
    final_norm = final_g is not None
    fg = (final_g if final_norm else jnp.ones((D,), F32)).reshape(1, D)
    idx = jnp.arange(tl, dtype=jnp.int32)
    tril = (idx[None, :] < idx[:, None]).astype(BF16)
    full = lambda b, i, j: (0, 0)
    tok = pl.BlockSpec((1, tl, D), lambda b, i, j: (b, i, 0))
    vec = pl.BlockSpec((1, 1, D), lambda b, i, j: (b, 0, 0))
    wspec = lambda k, n: pl.BlockSpec((ne, k, n), lambda b, i, j: (j, 0, 0))
    return pl.pallas_call(
        functools.partial(_moe_kernel, ne=ne, final_norm=final_norm),
        grid=(B, L // tl, MOE_EXPERTS // ne),
        in_specs=[tok, pl.BlockSpec((1, D), full), vec, vec, vec,
                  pl.BlockSpec((D, LANE), full), pl.BlockSpec((D, LANE), full), pl.BlockSpec((1, LANE), full),
                  pl.BlockSpec((tl, tl), full, pipeline_mode=pl.Buffered(1)),
                  wspec(D, MOE_HIDDEN), wspec(D, MOE_HIDDEN), wspec(MOE_HIDDEN, D),
                  pl.BlockSpec((1, D), full)],
        out_specs=tok,
        out_shape=jax.ShapeDtypeStruct((B, L, D), F32),
        scratch_shapes=[pltpu.VMEM((S, D), BF16),
                        pltpu.VMEM((S, LANE), F32),
                        pltpu.VMEM((tl, S), BF16),
                        pltpu.VMEM((S, D), F32),
                        pltpu.SMEM((2 * MOE_GROUPS,), jnp.int32)],
        compiler_params=_cparams("parallel", "parallel", "arbitrary"),
        name="moe",
    )(x, g2, sc, sh, m5, wrh, wrl, br, tril, wg, wu, wd, fg)


def _moe_params(wg, bg, we, be, w_gate, w_up, w_down):
    D = wg.shape[0]
    pad = LANE - MOE_GROUPS - MOE_EXPERTS
    wr = jnp.pad(jnp.concatenate([wg, we], axis=1), ((0, 0), (0, pad)))
    br = jnp.pad(jnp.concatenate([bg, be]), (0, pad)).reshape(1, LANE)
    wrh = wr.astype(BF16)
    wrl = (wr - wrh.astype(F32)).astype(BF16)
    return (wrh, wrl, br,
            w_gate.reshape(MOE_EXPERTS, D, MOE_HIDDEN).astype(BF16),
            w_up.reshape(MOE_EXPERTS, D, MOE_HIDDEN).astype(BF16),
            w_down.reshape(MOE_EXPERTS, MOE_HIDDEN, D).astype(BF16))


def _inproj1_kernel(x_ref, g_ref, sc_ref, sh_ref, w_ref, o_ref, h_scr):
    @pl.when(pl.program_id(2) == 0)
    def _():
        h_scr[...] = _rms_mod(x_ref[0], g_ref[...], sc_ref[0], sh_ref[0]).astype(BF16)

    o_ref[0] = _bdot(h_scr[...], w_ref[...]).astype(o_ref.dtype)


def _inproj1(x, g, sc, sh, w):
    B, L, D = x.shape
    N = w.shape[1]
    tl = _pick(L, 1024, 512, 256, 128)
    tn = _pick(N, 2560, 1280, 1024, 512)
    return pl.pallas_call(
        _inproj1_kernel,
        grid=(B, L // tl, N // tn),
        in_specs=[pl.BlockSpec((1, tl, D), lambda b, i, j: (b, i, 0)),
                  pl.BlockSpec((1, D), lambda b, i, j: (0, 0)),
                  pl.BlockSpec((1, 1, D), lambda b, i, j: (b, 0, 0)),
                  pl.BlockSpec((1, 1, D), lambda b, i, j: (b, 0, 0)),
                  pl.BlockSpec((D, tn), lambda b, i, j: (0, j))],
        out_specs=pl.BlockSpec((1, tl, tn), lambda b, i, j: (b, i, j)),
        out_shape=jax.ShapeDtypeStruct((B, L, N), BF16),
        scratch_shapes=[pltpu.VMEM((tl, D), BF16)],
        compiler_params=_cparams("parallel", "parallel", "arbitrary"),
        name="inproj1",
    )(x, g, sc, sh, w)


def _gla_masks(C):
    t = np.arange(C)[:, None]
    s = np.arange(C)[None, :]
    fwd = []
    hs = 1
    while hs < C:
        fwd.append(((t // (2 * hs)) == (s // (2 * hs))) & ((t & hs) != 0) & ((s & hs) == 0))
        hs *= 2
    fwd = np.stack(fwd).astype(np.float32)
    return jnp.asarray(np.stack([fwd, fwd.transpose(0, 2, 1)])), jnp.asarray(np.eye(C, dtype=np.float32))


def _gla_direction(pq, v, z, lb, masks, eye, st_ref, rev):
    C = pq.shape[0]
    heads = [slice(h * LANE, (h + 1) * LANE) for h in range(C_HEADS)]
    q = pq * _sigmoid(pq)
    f = lb + (1.0 - lb) * _sigmoid(z)
    k = 1.0 - f
    g = jnp.log2(f)
    row = lax.broadcasted_iota(jnp.int32, (C, C_D), 0)
    qb = q.astype(BF16)
    kb = k.astype(BF16)
    att = [eye * lax.dot_general(qb[:, hd], kb[:, hd], _NT, preferred_element_type=F32) for hd in heads]
    pinc = g
    tot = g
    hs = 1
    lvl = 0
    while hs < C:
        if rev:
            aq = tot - pinc + g
            ak = pinc - g
        else:
            aq = pinc
            ak = tot - pinc
        qe = (q * jnp.exp2(aq)).astype(BF16)
        ke = (k * jnp.exp2(ak)).astype(BF16)
        for h, hd in enumerate(heads):
            blk = lax.dot_general(qe[:, hd], ke[:, hd], _NT, preferred_element_type=F32)
            att[h] = jnp.where(masks[lvl], blk, att[h])
        if hs < SUBLANE:
            odd = (row & hs) != 0
            t3 = tot.reshape(C // SUBLANE, SUBLANE, C_D)
            tprev = pltpu.roll(t3, hs, 1).reshape(C, C_D)
            tnext = pltpu.roll(t3, SUBLANE - hs, 1).reshape(C, C_D)
            pinc = pinc + jnp.where(odd, tprev, 0.0)
            tot = tot + jnp.where(odd, tprev, tnext)
        else:
            w = hs // SUBLANE
            tile = lambda a, i: a[i * SUBLANE:(i + 1) * SUBLANE]
            new_p, new_t = [], []
            for i in range(C // SUBLANE):
                if (i // w) % 2 == 1:
                    new_p.append(tile(pinc, i) + tile(tot, i - w))
                    new_t.append(tile(tot, i) + tile(tot, i - w))
                else:
                    new_p.append(tile(pinc, i))
                    new_t.append(tile(tot, i) + tile(tot, i + w))
            pinc = jnp.concatenate(new_p, axis=0)
            tot = jnp.concatenate(new_t, axis=0)
        hs *= 2
        lvl += 1
    if rev:
        q_dec = tot - pinc + g
        k_dec = pinc - g
    else:
        q_dec = pinc
        k_dec = tot - pinc
    vb = v.astype(BF16)
    qd = (q * jnp.exp2(q_dec)).astype(BF16)
    kd = (k * jnp.exp2(k_dec)).astype(BF16)
    keep = jnp.exp2(tot[0:1, :])
    outs = []
    for h, hd in enumerate(heads):
        st = st_ref[h]
        o = _bdot(att[h].astype(BF16), vb[:, hd])
        o = o + lax.dot_general(qd[:, hd], st.astype(BF16), _NT, preferred_element_type=F32)
        upd = lax.dot_general(vb[:, hd], kd[:, hd], _TN, preferred_element_type=F32)
        st_ref[h] = st * keep[:, hd] + upd
        outs.append(o)
    return jnp.concatenate(outs, axis=1)


def _gla_kernel(qf_ref, if_ref, zf_ref, qb_ref, ib_ref, zb_ref, lb_ref, mk_ref, eye_ref, s0_ref,
                of_ref, ob_ref, sfin_ref, st_scr):
    c = pl.program_id(1)

    @pl.when(c == 0)
    def _():
        st_scr[...] = s0_ref[0]

    eye = eye_ref[...]
    nlev = mk_ref.shape[1]
    for d, (q_ref, i_ref, z_ref, o_ref) in enumerate(((qf_ref, if_ref, zf_ref, of_ref),
                                                      (qb_ref, ib_ref, zb_ref, ob_ref))):
        masks = [mk_ref[d, lv] > 0.5 for lv in range(nlev)]
        o = _gla_direction(q_ref[0].astype(F32), i_ref[0].astype(F32), z_ref[0].astype(F32),
                           lb_ref[d:d + 1, :], masks, eye, st_scr.at[d], rev=(d == 1))
        o_ref[0] = o.astype(o_ref.dtype)

    @pl.when(c == pl.num_programs(1) - 1)
    def _():
        sfin_ref[0] = st_scr[...]


def _gla(p, lb, s0):
    B, L, _ = p.shape
    C = GLA_CHUNK
    nch = L // C
    masks, eye = _gla_masks(C)
    fcol = lambda k: pl.BlockSpec((1, C, C_D), lambda b, c: (b, c, k))
    bcol = lambda k: pl.BlockSpec((1, C, C_D), lambda b, c: (b, nch - 1 - c, k))
    st_spec = pl.BlockSpec((1, 2, C_HEADS, LANE, LANE), lambda b, c: (b, 0, 0, 0, 0))
    return pl.pallas_call(
        _gla_kernel,
        grid=(B, nch),
        in_specs=[fcol(0), fcol(2), fcol(3), bcol(0), bcol(2), bcol(4),
                  pl.BlockSpec((2, C_D), lambda b, c: (0, 0)),
                  pl.BlockSpec(masks.shape, lambda b, c: (0, 0, 0, 0)),
                  pl.BlockSpec((C, C), lambda b, c: (0, 0)),
                  st_spec],
        out_specs=[fcol(0), bcol(0), st_spec],
        out_shape=[jax.ShapeDtypeStruct((B, L, C_D), BF16),
                   jax.ShapeDtypeStruct((B, L, C_D), BF16),
                   jax.ShapeDtypeStruct((B, 2, C_HEADS, LANE, LANE), F32)],
        scratch_shapes=[pltpu.VMEM((2, C_HEADS, LANE, LANE), F32)],
        compiler_params=_cparams("parallel", "arbitrary"),
        name="gla",
    )(p, p, p, p, p, p, lb, masks, eye, s0)


def _outproj1_kernel(of_ref, ob_ref, g_ref, ng_ref, w_ref, x_ref, m_ref, o_ref):
    o = of_ref[0].astype(F32) + ob_ref[0].astype(F32)
    parts = []
    for h in range(C_HEADS):
        oh = o[:, h * LANE:(h + 1) * LANE]
        ms = jnp.mean(oh * oh, axis=-1, keepdims=True)
        parts.append(oh * lax.rsqrt(ms + EPS))
    on = jnp.concatenate(parts, axis=1) * ng_ref[...] * _sigmoid(g_ref[0].astype(F32))
    y = _bdot(on.astype(BF16), w_ref[...])
    o_ref[0] = x_ref[0] + m_ref[0] * y


def _outproj1(o_f, o_b, p, norm_g, w_out, x, m2):
    B, L, D = x.shape
    tl = _pick(L, 512, 256, 128)
    full = lambda b, i: (0, 0)
    tok = pl.BlockSpec((1, tl, C_D), lambda b, i: (b, i, 0))
    return pl.pallas_call(
        _outproj1_kernel,
        grid=(B, L // tl),
        in_specs=[tok, tok,
                  pl.BlockSpec((1, tl, C_D), lambda b, i: (b, i, 1)),
                  pl.BlockSpec((1, C_D), full),
                  pl.BlockSpec((C_D, D), full),
                  pl.BlockSpec((1, tl, D), lambda b, i: (b, i, 0)),
                  pl.BlockSpec((1, 1, D), lambda b, i: (b, 0, 0))],
        out_specs=pl.BlockSpec((1, tl, D), lambda b, i: (b, i, 0)),
        out_shape=jax.ShapeDtypeStruct((B, L, D), F32),
        compiler_params=_cparams("parallel", "parallel"),
        name="outproj1",
    )(o_f, o_b, p, norm_g.reshape(1, C_D), w_out.astype(BF16), x, m2)


def _mods(cmat, w, b, nb):
    R = cmat.shape[0]
    pad = (-R) % SUBLANE
    m = _modvec(jnp.pad(cmat, ((0, pad), (0, 0))), w, b)[:R]
    m = jnp.broadcast_to(m, (nb, m.shape[1])) if R == 1 else m
    return [m[:, None, k * D_MODEL:(k + 1) * D_MODEL] for k in range(6)]


def kernel(x, c, ctx, c_ctx, mod_w, mod_b, norm1_g, norm2_g, final_g,
           ab_w_in, ab_w_out, hy_conv_w, hy_conv_b, hy_fw1, hy_fb1, hy_ff1,
           hy_fw2, hy_fb2, hy_ff2, hy_fw3, hy_bias,
           s5_lam_re, s5_lam_im, s5_log_step, s5_b_re, s5_b_im, s5_c_re, s5_c_im,
           s5_d, s5_glu_w, s5_glu_b,
           c_w_in, c_w_out, c_lower_bounds, c_norm_g,
           moe_wg, moe_bg, moe_we, moe_be, moe_w_gate, moe_w_up, moe_w_down):
    B, L, D = x.shape
    row = lambda t: t.reshape(1, -1)

    m = _mods(c, mod_w[0], mod_b[0], B)
    mc = _mods(c_ctx[None, :], mod_w[0], mod_b[0], B)
    w_in = ab_w_in[0]
    wt_hy = w_in[:, :3 * HY_D].T.astype(BF16)
    w_s5 = w_in[:, 3 * HY_D:].astype(BF16)
    g1 = row(norm1_g[0])
    pt_c, *u_c = _inproj0(ctx, g1, 1.0 + mc[1], mc[0], wt_hy, w_s5)
    pt_l, *u_l = _inproj0(x, g1, 1.0 + m[1], m[0], wt_hy, w_s5)

    ops = _s5_operators(s5_lam_re[0], s5_lam_im[0], s5_log_step[0], s5_b_re[0], s5_b_im[0],
                        s5_c_re[0], s5_c_im[0])
    *ys_c, s_fin = _s5_pass(*u_c, ops, jnp.zeros((B, 2, 2 * S5_P), F32))
    *ys_l, _ = _s5_pass(*u_l, ops, s_fin)

    filt = (hy_fw1[0], hy_fb1[0], hy_ff1[0], hy_fw2[0], hy_fb2[0], hy_ff2[0], hy_fw3[0])
    hy_c = _hyena(pt_c, hy_conv_w[0], hy_conv_b[0], hy_bias[0], filt)
    hy_l = _hyena(pt_l, hy_conv_w[0], hy_conv_b[0], hy_bias[0], filt)

    x = _outproj0(hy_l, ys_l, u_l, s5_d[0], s5_glu_w[0], s5_glu_b[0], ab_w_out[0], x, m[2])
    ctx = _outproj0(hy_c, ys_c, u_c, s5_d[0], s5_glu_w[0], s5_glu_b[0], ab_w_out[0], ctx, mc[2])

    mp = _moe_params(moe_wg[0], moe_bg[0], moe_we[0], moe_be[0],
                     moe_w_gate[0], moe_w_up[0], moe_w_down[0])
    g2 = row(norm2_g[0])
    x = _moe(x, g2, 1.0 + m[4], m[3], m[5], *mp)
    ctx = _moe(ctx.reshape(1, -1, D), g2, 1.0 + mc[4][:1], mc[3][:1], mc[5][:1], *mp).reshape(ctx.shape)

    m = _mods(c, mod_w[1], mod_b[1], B)
    mc = _mods(c_ctx[None, :], mod_w[1], mod_b[1], B)
    sm = jax.nn.softmax(c_lower_bounds.astype(F32), axis=1)
    lower = (jnp.cumsum(sm, axis=1) - sm[:, :1])[:, 1]
    g1 = row(norm1_g[1])
    w1 = c_w_in[0].astype(BF16)
    p_c = _inproj1(ctx, g1, 1.0 + mc[1], mc[0], w1)
    p_l = _inproj1(x, g1, 1.0 + m[1], m[0], w1)
    zeros = jnp.zeros((B, 2, C_HEADS, C_HEAD_DIM, C_HEAD_DIM), F32)
    _, _, s_ctx = _gla(p_c, lower, zeros)
    o_f, o_b, _ = _gla(p_l, lower, s_ctx)
    x = _outproj1(o_f, o_b, p_l, c_norm_g[0], c_w_out[0], x, m[2])

    mp = _moe_params(moe_wg[1], moe_bg[1], moe_we[1], moe_be[1],
                     moe_w_gate[1], moe_w_up[1], moe_w_down[1])
    return _moe(x, row(norm2_g[1]), 1.0 + m[4], m[3], m[5], *mp, final_g=final_g)
```

```python
import functools
import math

import numpy as np
import jax
import jax.numpy as jnp
from jax import lax
from jax.experimental import pallas as pl
from jax.experimental.pallas import tpu as pltpu

F32 = jnp.float32
BF16 = jnp.bfloat16
EPS = 1e-6
HIGHEST = lax.Precision.HIGHEST

D_MODEL = 1024
HY_D = 768
HY_ORDER = 2
HY_EMB = 33
HY_BANDS = (HY_EMB - 1) // 2
HY_DECAY_SHORT = 0.3
HY_DECAY_LONG = 1.5
HY_TARGET = 1e-2
S5_D = 256
S5_GROUP = 16
S5_GROUPS = S5_D // S5_GROUP
S5_STATE = 64
S5_P = S5_GROUPS * S5_STATE
S5_HALVES = S5_D // 128
S5_PH = S5_P // S5_HALVES
C_HEADS = 8
C_HEAD_DIM = 128
C_D = C_HEADS * C_HEAD_DIM
MOE_GROUPS = 4
MOE_EPG = 8
MOE_EXPERTS = MOE_GROUPS * MOE_EPG
MOE_HIDDEN = 256
MOE_UNIT = 128
MOE_ALIGN = 16

LANE = 128
SUBLANE = 8
HY_BLK = 512
S5_CHUNK = 16
GLA_CHUNK = 64
VMEM_LIMIT = 56 * 1024 * 1024


def _cparams(*sem):
    return pltpu.CompilerParams(dimension_semantics=sem, vmem_limit_bytes=VMEM_LIMIT)


def _pick(n, *cands):
    for c in cands:
        if n % c == 0:
            return c
    return n


def _rms_mod(xv, g, sc, sh):
    ms = jnp.mean(xv * xv, axis=-1, keepdims=True)
    return xv * lax.rsqrt(ms + EPS) * g * sc + sh


def _sigmoid(v):
    return 1.0 / (1.0 + jnp.exp(-v))


def _bdot(a, b):
    return jnp.dot(a, b, preferred_element_type=F32)


_NT = (((1,), (1,)), ((), ()))
_TN = (((0,), (0,)), ((), ()))


def _mm_kernel(a_ref, b_ref, o_ref):
    o_ref[...] = _bdot(a_ref[...].astype(BF16), b_ref[...]).astype(o_ref.dtype)


def _mm(a, b, out_dtype=F32, name="mm"):
    M, K = a.shape
    N = b.shape[1]
    tm = _pick(M, 768, 512, 256, 128)
    tn = _pick(N, 512, 256, 128)
    return pl.pallas_call(
        _mm_kernel,
        grid=(M // tm, N // tn),
        in_specs=[pl.BlockSpec((tm, K), lambda i, j: (i, 0)),
                  pl.BlockSpec((K, tn), lambda i, j: (0, j))],
        out_specs=pl.BlockSpec((tm, tn), lambda i, j: (i, j)),
        out_shape=jax.ShapeDtypeStruct((M, N), out_dtype),
        compiler_params=_cparams("parallel", "parallel"),
        name=name,
    )(a, b.astype(BF16))


def _modvec_kernel(c_ref, w_ref, b_ref, o_ref):
    cv = c_ref[...]
    sc = cv * _sigmoid(cv)
    o_ref[...] = jnp.dot(sc, w_ref[...], preferred_element_type=F32, precision=HIGHEST) + b_ref[...]


def _modvec(cvec, w, b):
    R, D = cvec.shape
    N = w.shape[1]
    tn = _pick(N, 512, 256, 128)
    return pl.pallas_call(
        _modvec_kernel,
        grid=(N // tn,),
        in_specs=[pl.BlockSpec((R, D), lambda j: (0, 0)),
                  pl.BlockSpec((D, tn), lambda j: (0, j)),
                  pl.BlockSpec((1, tn), lambda j: (0, j))],
        out_specs=pl.BlockSpec((R, tn), lambda j: (0, j)),
        out_shape=jax.ShapeDtypeStruct((R, N), F32),
        compiler_params=_cparams("parallel"),
        name="modvec",
    )(cvec, w, b.reshape(1, N))


def _inproj0_kernel(x_ref, g_ref, sc_ref, sh_ref, wt_ref, w2_ref, pt_ref, ua_ref, ub_ref):
    h = _rms_mod(x_ref[0], g_ref[...], sc_ref[0], sh_ref[0]).astype(BF16)
    pt_ref[0] = lax.dot_general(wt_ref[...], h, _NT, preferred_element_type=F32).astype(BF16)
    u = _bdot(h, w2_ref[...])
    ua_ref[0] = u[:, :LANE]
    ub_ref[0] = u[:, LANE:]


def _inproj0(x, g, sc, sh, wt_hy, w_s5):
    B, L, D = x.shape
    C = wt_hy.shape[0]
    tl = _pick(L, 512, 256, 128)
    return pl.pallas_call(
        _inproj0_kernel,
        grid=(B, L // tl),
        in_specs=[pl.BlockSpec((1, tl, D), lambda b, i: (b, i, 0)),
                  pl.BlockSpec((1, D), lambda b, i: (0, 0)),
                  pl.BlockSpec((1, 1, D), lambda b, i: (b, 0, 0)),
                  pl.BlockSpec((1, 1, D), lambda b, i: (b, 0, 0)),
                  pl.BlockSpec((C, D), lambda b, i: (0, 0)),
                  pl.BlockSpec((D, S5_D), lambda b, i: (0, 0))],
        out_specs=[pl.BlockSpec((1, C, tl), lambda b, i: (b, 0, i)),
                   pl.BlockSpec((1, tl, LANE), lambda b, i: (b, i, 0)),
                   pl.BlockSpec((1, tl, LANE), lambda b, i: (b, i, 0))],
        out_shape=[jax.ShapeDtypeStruct((B, C, L), BF16),
                   jax.ShapeDtypeStruct((B, L, LANE), F32),
                   jax.ShapeDtypeStruct((B, L, LANE), F32)],
        compiler_params=_cparams("parallel", "parallel"),
        name="inproj0",
    )(x, g, sc, sh, wt_hy, w_s5)


def _shortconv_kernel(p1_ref, p2_ref, p3_ref, w_ref, b_ref, x1_ref, x2_ref, v_ref):
    L = p1_ref.shape[2]
    nb, lb = x1_ref.shape[1], x1_ref.shape[3]
    lane = lax.broadcasted_iota(jnp.int32, (1, L), 1)
    first = lane == 0
    last = lane == L - 1
    for k, (p_ref, o_ref) in enumerate(((p1_ref, x1_ref), (p2_ref, x2_ref), (p3_ref, v_ref))):
        u = p_ref[0].astype(F32)
        w = w_ref[k]
        prev = jnp.where(first, 0.0, pltpu.roll(u, 1, 1))
        nxt = jnp.where(last, 0.0, pltpu.roll(u, L - 1, 1))
        y = (w[:, 0:1] * prev + w[:, 1:2] * u + w[:, 2:3] * nxt + b_ref[k]).astype(BF16)
        for j in range(nb):
            o_ref[0, j] = y[:, j * lb:(j + 1) * lb]


def _shortconv(pt, conv_w, conv_b, lb):
    B, C3, L = pt.shape
    nb = L // lb
    tc = 128
    nc = HY_D // tc
    w = conv_w.T.reshape(3, HY_D, 3)
    b = conv_b.reshape(3, HY_D, 1)
    specs = [pl.BlockSpec((1, tc, L), (lambda b_, i, k=k: (b_, k * nc + i, 0))) for k in range(3)]
    o_spec = pl.BlockSpec((1, nb, tc, lb), lambda b_, i: (b_, 0, i, 0))
    return pl.pallas_call(
        _shortconv_kernel,
        grid=(B, nc),
        in_specs=specs + [pl.BlockSpec((3, tc, 3), lambda b_, i: (0, i, 0)),
                          pl.BlockSpec((3, tc, 1), lambda b_, i: (0, i, 0))],
        out_specs=[o_spec, o_spec, o_spec],
        out_shape=[jax.ShapeDtypeStruct((B, nb, HY_D, lb), BF16)] * 3,
        compiler_params=_cparams("parallel", "parallel"),
        name="shortconv",
    )(pt, pt, pt, w, b)


def _filter_kernel(z_ref, w1_ref, b1_ref, f1_ref, w2_ref, b2_ref, f2_ref, w3_ref, dec_ref, o_ref):
    dot = functools.partial(jnp.dot, preferred_element_type=F32, precision=HIGHEST)
    z = z_ref[...]
    hdn = jnp.sin(f1_ref[...] * (dot(w1_ref[...], z) + b1_ref[...]))
    hdn = jnp.sin(f2_ref[...] * (dot(w2_ref[...], hdn) + b2_ref[...]))
    t = z[0:1, :]
    o_ref[...] = dot(w3_ref[...], hdn) * jnp.exp(-dec_ref[...] * t)


def _hyena_filter(L, fw1, fb1, ff1, fw2, fb2, ff2, fw3):
    pos = jnp.arange(L, dtype=F32)
    t = pos / max(L - 1, 1)
    w = 2.0 * math.pi * pos / L
    bands = jnp.linspace(1e-4, HY_BANDS - 1, HY_BANDS, dtype=F32)
    ang = bands[:, None] * w[None, :]
    z = jnp.concatenate([t[None, :], jnp.cos(ang), -jnp.sin(ang)], axis=0)
    z = jnp.pad(z, ((0, LANE - HY_EMB), (0, 0)))
    w1 = jnp.pad(fw1, ((0, LANE - HY_EMB), (0, 0))).T
    deltas = jnp.abs(jnp.linspace(math.log(HY_TARGET) / HY_DECAY_LONG,
                                  math.log(HY_TARGET) / HY_DECAY_SHORT, HY_D, dtype=F32))
    dec = jnp.tile(deltas, HY_ORDER * 2).reshape(-1, 1)
    nf = fw1.shape[1]
    No = fw3.shape[1]
    tl = _pick(L, 512, 256)
    full = lambda i: (0, 0)
    col = lambda v: v.reshape(nf, 1)
    return pl.pallas_call(
        _filter_kernel,
        grid=(L // tl,),
        in_specs=[pl.BlockSpec((LANE, tl), lambda i: (0, i)),
                  pl.BlockSpec((nf, LANE), full), pl.BlockSpec((nf, 1), full), pl.BlockSpec((nf, 1), full),
                  pl.BlockSpec((nf, nf), full), pl.BlockSpec((nf, 1), full), pl.BlockSpec((nf, 1), full),
                  pl.BlockSpec((No, nf), full), pl.BlockSpec((No, 1), full)],
        out_specs=pl.BlockSpec((No, tl), lambda i: (0, i)),
        out_shape=jax.ShapeDtypeStruct((No, L), F32),
        compiler_params=_cparams("parallel"),
        name="hyena_filter",
    )(z, w1, col(fb1), col(ff1), fw2.T, col(fb2), col(ff2), fw3.T, dec)


def _dft_mats(L, blk):
    N = 2 * L
    W = 64
    j = jnp.arange(2 * L, dtype=jnp.int32)
    kk = (j // (2 * blk)) * blk + j % blk
    is_im = (j // blk) % 2 == 1
    nyq = is_im & (kk == 0)
    hi = jnp.arange(L // W, dtype=jnp.int32) * W
    lo = jnp.arange(W, dtype=jnp.int32)
    alt = (1 - 2 * (lo % 2)).astype(F32)
    th = 2.0 * math.pi / N
    ah = ((hi[:, None] * kk[None, :]) % N).astype(F32) * th
    al = ((lo[:, None] * kk[None, :]) % N).astype(F32) * th
    ch, sh, cl, sl = jnp.cos(ah), jnp.sin(ah), jnp.cos(al), jnp.sin(al)
    c = ch[:, None, :] * cl[None] - sh[:, None, :] * sl[None]
    s = sh[:, None, :] * cl[None] + ch[:, None, :] * sl[None]
    wf = jnp.where(nyq[None, None, :], alt[None, :, None], jnp.where(is_im[None, None, :], -s, c))
    ct = ch.T[:, :, None] * cl.T[:, None, :] - sh.T[:, :, None] * sl.T[:, None, :]
    st = sh.T[:, :, None] * cl.T[:, None, :] + ch.T[:, :, None] * sl.T[:, None, :]
    scale = jnp.where(kk == 0, 1.0 / N, 2.0 / N).astype(F32)[:, None, None]
    wi = jnp.where(nyq[:, None, None], alt[None, None, :], jnp.where(is_im[:, None, None], -st, ct)) * scale
    return wf.reshape(L, 2 * L).astype(BF16), wi.reshape(2 * L, L).astype(BF16)


def _segdft_kernel(a_ref, w_ref, o_ref):
    o_ref[...] = _bdot(a_ref[...].astype(BF16), w_ref[...])


def _segdft(ht, wf, lb):
    R, L = ht.shape
    tm = 256
    return pl.pallas_call(
        _segdft_kernel,
        grid=(R // tm, L // lb),
        in_specs=[pl.BlockSpec((tm, lb), lambda i, m: (i, m)),
                  pl.BlockSpec((lb, 2 * lb), lambda i, m: (0, 0))],
        out_specs=pl.BlockSpec((tm, 2 * lb), lambda i, m: (i, m)),
        out_shape=jax.ShapeDtypeStruct((R, 2 * L), F32),
        compiler_params=_cparams("parallel", "parallel"),
        name="filter_segdft",
    )(ht, wf)


def _filter_blocks_kernel(hf_ref, hb_ref, af_ref, ab_ref, kr_ref, ki_ref, kn_ref, *, nb, lb):
    hf = hf_ref[...]
    hb = hb_ref[...]
    lag = lax.broadcasted_iota(jnp.int32, (1, hf.shape[1]), 1)
    hbz = jnp.where(lag == 0, 0.0, hb)
    nrm = lax.rsqrt(jnp.sum(hf * hf, axis=1, keepdims=True) + jnp.sum(hbz * hbz, axis=1, keepdims=True))
    k = lax.broadcasted_iota(jnp.int32, (1, lb), 1)
    sgn = jnp.where(k % 2 == 0, 1.0, -1.0)
    bin0 = k == 0

    def seg(ref, m):
        re = ref[:, m * 2 * lb:m * 2 * lb + lb]
        imp = ref[:, m * 2 * lb + lb:(m + 1) * 2 * lb]
        return re, jnp.where(bin0, 0.0, imp), imp[:, 0:1]

    def emit(idx, re, im, nyq):
        kr_ref[0, idx] = re * nrm
        ki_ref[0, idx] = im * nrm
        kn_ref[0, idx] = nyq * nrm

    fr, fi, fn = seg(af_ref, 0)
    br, bi, bn = seg(ab_ref, 0)
    b0 = hb[:, 0:1]
    emit(nb - 1, fr + br - b0, fi - bi, fn + bn - b0)
    for d in range(1, nb):
        for ref, taps, conj, idx in ((af_ref, hf, 1.0, nb - 1 + d), (ab_ref, hb, -1.0, nb - 1 - d)):
            r1, i1, n1 = seg(ref, d)
            r0, i0, n0 = seg(ref, d - 1)
            e0 = taps[:, (d - 1) * lb:(d - 1) * lb + 1]
            emit(idx, r1 + sgn * (r0 - e0), conj * (i1 + sgn * i0), n1 + n0 - e0)


def _filter_blocks(ht, seg, lb):
    R, L = ht.shape
    nb = L // lb
    nlag = 2 * nb - 1
    tm = 128
    nc = HY_D // tm
    fwd = lambda i: ((i // nc) * 2 * nc + i % nc, 0)
    bwd = lambda i: ((i // nc) * 2 * nc + nc + i % nc, 0)
    kspec = pl.BlockSpec((1, nlag, tm, lb), lambda i: (i // nc, 0, i % nc, 0))
    return pl.pallas_call(
        functools.partial(_filter_blocks_kernel, nb=nb, lb=lb),
        grid=(HY_ORDER * nc,),
        in_specs=[pl.BlockSpec((tm, L), fwd), pl.BlockSpec((tm, L), bwd),
                  pl.BlockSpec((tm, 2 * L), fwd), pl.BlockSpec((tm, 2 * L), bwd)],
        out_specs=[kspec, kspec, pl.BlockSpec((1, nlag, tm, 1), lambda i: (i // nc, 0, i % nc, 0))],
        out_shape=[jax.ShapeDtypeStruct((HY_ORDER, nlag, HY_D, lb), F32),
                   jax.ShapeDtypeStruct((HY_ORDER, nlag, HY_D, lb), F32),
                   jax.ShapeDtypeStruct((HY_ORDER, nlag, HY_D, 1), F32)],
        compiler_params=_cparams("parallel"),
        name="filter_blocks",
    )(ht, ht, seg, seg)


def _hyconv_kernel(v_ref, g_ref, bias_ref, kr_ref, ki_ref, kn_ref, wf_ref, wi_ref, o_ref,
                   vr_scr, vi_scr, vn_scr, y_scr):
    nb, ct, lb = v_ref.shape[1], v_ref.shape[2], v_ref.shape[3]
    bin0 = lax.broadcasted_iota(jnp.int32, (1, lb), 1) == 0
    v = v_ref[0]
    acc = _bdot(v.reshape(nb * ct, lb), wf_ref[...])
    imp = acc[:, lb:]
    vr_scr[...] = acc[:, :lb].reshape(nb, ct, lb)
    vi_scr[...] = jnp.where(bin0, 0.0, imp).reshape(nb, ct, lb)
    vn_scr[...] = imp[:, 0:1].reshape(nb, ct, 1)

    rt = min(ct, (8 * SUBLANE * LANE) // lb)
    nr = ct // rt

    def out_rows(n, carry):
        i = n // nr
        rows = pl.ds(pl.multiple_of((n % nr) * rt, rt), rt)
        yr = jnp.zeros((rt, lb), F32)
        yi = jnp.zeros((rt, lb), F32)
        yn = jnp.zeros((rt, 1), F32)
        for j in range(nb):
            d = i - j + (nb - 1)
            kr = kr_ref[0, d, rows, :]
            ki = ki_ref[0, d, rows, :]
            vr = vr_scr[j, rows, :]
            vi = vi_scr[j, rows, :]
            yr = yr + (vr * kr - vi * ki)
            yi = yi + (vr * ki + vi * kr)
            yn = yn + vn_scr[j, rows, :] * kn_ref[0, d, rows, :]
        y_scr[i, rows, :lb] = yr.astype(BF16)
        y_scr[i, rows, lb:] = jnp.where(bin0, yn, yi).astype(BF16)
        return carry

    lax.fori_loop(0, nb * nr, out_rows, 0)
    conv = _bdot(y_scr[...].reshape(nb * ct, 2 * lb), wi_ref[...]).reshape(nb, ct, lb)
    o_ref[0] = (g_ref[0].astype(F32) * (conv + v.astype(F32) * bias_ref[...])).astype(o_ref.dtype)


def _hyconv(v, gate, bias, kr, ki, kn, order, wf, wi):
    B, nb, C, lb = v.shape
    nlag = kr.shape[1]
    ct = 128
    blk = pl.BlockSpec((1, nb, ct, lb), lambda c, b: (b, 0, c, 0))
    kspec = pl.BlockSpec((1, nlag, ct, lb), lambda c, b: (order, 0, c, 0))
    return pl.pallas_call(
        _hyconv_kernel,
        grid=(C // ct, B),
        in_specs=[blk, blk,
                  pl.BlockSpec((ct, 1), lambda c, b: (c, 0)),
                  kspec, kspec,
                  pl.BlockSpec((1, nlag, ct, 1), lambda c, b: (order, 0, c, 0)),
                  pl.BlockSpec((lb, 2 * lb), lambda c, b: (0, 0)),
                  pl.BlockSpec((2 * lb, lb), lambda c, b: (0, 0))],
        out_specs=blk,
        out_shape=jax.ShapeDtypeStruct((B, nb, C, lb), BF16),
        scratch_shapes=[pltpu.VMEM((nb, ct, lb), F32), pltpu.VMEM((nb, ct, lb), F32),
                        pltpu.VMEM((nb, ct, 1), F32), pltpu.VMEM((nb, ct, 2 * lb), BF16)],
        compiler_params=_cparams("parallel", "parallel"),
        name="hyena_conv",
    )(v, gate, bias.reshape(C, 1), kr, ki, kn, wf, wi)


def _hyena(pt, conv_w, conv_b, hy_bias, filt):
    L = pt.shape[2]
    lb = HY_BLK if L >= 2 * HY_BLK else L // 2
    wf, wi = _dft_mats(lb, lb)
    ht = _hyena_filter(L, *filt)
    kr, ki, kn = _filter_blocks(ht, _segdft(ht, wf, lb), lb)
    x1, x2, v = _shortconv(pt, conv_w, conv_b, lb)
    z = _hyconv(v, x1, hy_bias[0], kr, ki, kn, 0, wf, wi)
    return _hyconv(z, x2, hy_bias[1], kr, ki, kn, 1, wf, wi)


def _s5_operators(lam_re, lam_im, log_step, b_re, b_im, c_re, c_im):
    lam = lax.complex(lam_re.astype(F32), lam_im.astype(F32))
    dt = jnp.exp(log_step.astype(F32))[..., None]
    lam_bar = jnp.exp(lam * dt)
    lam_t = jnp.exp(lam * dt * S5_CHUNK)
    b_bar = ((lam_bar - 1.0) / lam)[..., None] * lax.complex(b_re.astype(F32), b_im.astype(F32))
    c_mat = lax.complex(c_re.astype(F32), c_im.astype(F32))
    gh = S5_GROUPS // S5_HALVES
    eye = jnp.eye(gh, dtype=F32)

    def bd_in(t):
        t = t.reshape(2, S5_HALVES, gh, S5_STATE, S5_GROUP)
        return jnp.einsum('dfgpn,gh->dfgnhp', t, eye).reshape(2, S5_HALVES, LANE, S5_PH)

    def bd_out(t):
        t = t.reshape(2, S5_HALVES, gh, S5_GROUP, S5_STATE)
        return jnp.einsum('dfgnp,gh->dfgphn', t, eye).reshape(2, S5_HALVES, S5_PH, LANE)

    bbd = jnp.stack([bd_in(jnp.real(b_bar)), bd_in(jnp.imag(b_bar))], axis=1).astype(BF16)
    cbd = jnp.stack([bd_out(jnp.real(c_mat)), bd_out(-jnp.imag(c_mat))], axis=1).astype(BF16)
    flat = lambda z: jnp.stack([jnp.real(z), jnp.imag(z)], axis=1).reshape(2, 2, S5_P)
    return bbd, cbd, flat(lam_bar), flat(lam_t)


def _s5_kernel(ua_ref, ub_ref, s0_ref, bbd_ref, cbd_ref, lam_ref, lamt_ref, ya_ref, yb_ref, sfin_ref,
               sloc_scr, sinit_scr, bu_scr):
    T = S5_CHUNK
    PH = S5_PH
    M = ua_ref.shape[1] // T
    nt = M // SUBLANE
    rid = lax.broadcasted_iota(jnp.int32, (SUBLANE, PH), 0)
    combos = [(hf, d) for hf in range(S5_HALVES) for d in (0, 1)]
    u_refs = (ua_ref, ub_ref)
    y_refs = (ya_ref, yb_ref)
    re_sl = lambda hf: slice(hf * PH, (hf + 1) * PH)
    im_sl = lambda hf: slice(S5_P + hf * PH, S5_P + (hf + 1) * PH)
    order = lambda d: list(range(T)) if d == 0 else list(range(T - 1, -1, -1))

    def drive(hf, d, s):
        us = u_refs[hf][0, pl.ds(s, M, stride=T), :].astype(BF16)
        return _bdot(us, bbd_ref[d, 0, hf]), _bdot(us, bbd_ref[d, 1, hf])

    def advance(hf, d, sr, si, br, bi):
        lr = lam_ref[d, 0:1, re_sl(hf)]
        li = lam_ref[d, 1:2, re_sl(hf)]
        return lr * sr - li * si + br, lr * si + li * sr + bi

    for hf, d in combos:
        steps = order(d)
        for n, s in enumerate(steps):
            br, bi = drive(hf, d, s)
            bu_scr[s, :, :PH] = br
            bu_scr[s, :, PH:] = bi
            sr, si = (br, bi) if n == 0 else advance(hf, d, sr, si, br, bi)
        sloc_scr[:, :PH] = sr
        sloc_scr[:, PH:] = si

        ar = lamt_ref[d, 0:1, re_sl(hf)]
        ai = lamt_ref[d, 1:2, re_sl(hf)]

        def scan_tile(n, carry, d=d, ar=ar, ai=ai):
            cr, ci = carry
            base = pl.multiple_of((n if d == 0 else nt - 1 - n) * SUBLANE, SUBLANE)
            lr_t = sloc_scr[pl.ds(base, SUBLANE), :PH]
            li_t = sloc_scr[pl.ds(base, SUBLANE), PH:]
            out_r = jnp.zeros((SUBLANE, PH), F32)
            out_i = jnp.zeros((SUBLANE, PH), F32)
            for r in (range(SUBLANE) if d == 0 else range(SUBLANE - 1, -1, -1)):
                out_r = jnp.where(rid == r, cr, out_r)
                out_i = jnp.where(rid == r, ci, out_i)
                cr, ci = (ar * cr - ai * ci + lr_t[r:r + 1], ar * ci + ai * cr + li_t[r:r + 1])
            sinit_scr[pl.ds(base, SUBLANE), :PH] = out_r
            sinit_scr[pl.ds(base, SUBLANE), PH:] = out_i
            return cr, ci

        cr, ci = lax.fori_loop(0, nt, scan_tile, (s0_ref[0, d:d + 1, re_sl(hf)], s0_ref[0, d:d + 1, im_sl(hf)]))
        sfin_ref[0, d:d + 1, re_sl(hf)] = cr
        sfin_ref[0, d:d + 1, im_sl(hf)] = ci

        sr = sinit_scr[:, :PH]
        si = sinit_scr[:, PH:]
        for s in steps:
            sr, si = advance(hf, d, sr, si, bu_scr[s, :, :PH], bu_scr[s, :, PH:])
            ys = (_bdot(sr.astype(BF16), cbd_ref[d, 0, hf]) + _bdot(si.astype(BF16), cbd_ref[d, 1, hf]))
            if d == 0:
                y_refs[hf][0, pl.ds(s, M, stride=T), :] = ys
            else:
                y_refs[hf][0, pl.ds(s, M, stride=T), :] += ys


def _s5_pass(ua, ub, ops, s0):
    bbd, cbd, lam, lam_t = ops
    B, L, _ = ua.shape
    M = L // S5_CHUNK
    full = lambda nd: (lambda b: (0,) * nd)
    tok = pl.BlockSpec((1, L, LANE), lambda b: (b, 0, 0))
    st = pl.BlockSpec((1, 2, 2 * S5_P), lambda b: (b, 0, 0))
    return pl.pallas_call(
        _s5_kernel,
        grid=(B,),
        in_specs=[tok, tok, st,
                  pl.BlockSpec(bbd.shape, full(5)),
                  pl.BlockSpec(cbd.shape, full(5)),
                  pl.BlockSpec((2, 2, S5_P), full(3)),
                  pl.BlockSpec((2, 2, S5_P), full(3))],
        out_specs=[tok, tok, st],
        out_shape=[jax.ShapeDtypeStruct((B, L, LANE), F32),
                   jax.ShapeDtypeStruct((B, L, LANE), F32),
                   jax.ShapeDtypeStruct((B, 2, 2 * S5_P), F32)],
        scratch_shapes=[pltpu.VMEM((M, 2 * S5_PH), F32),
                        pltpu.VMEM((M, 2 * S5_PH), F32),
                        pltpu.VMEM((S5_CHUNK, M, 2 * S5_PH), F32)],
        compiler_params=_cparams("parallel"),
        name="s5",
    )(ua, ub, s0, bbd, cbd, lam, lam_t)


def _gelu_tanh(v):
    return 0.5 * v * (1.0 + jnp.tanh(math.sqrt(2.0 / math.pi) * (v + 0.044715 * v * v * v)))


def _outproj0_kernel(hy_ref, ya_ref, yb_ref, ua_ref, ub_ref, d_ref, gw_ref, gb_ref, wa_ref, wb_ref,
                     x_ref, m_ref, o_ref):
    ys = (jnp.concatenate([ya_ref[0], yb_ref[0]], axis=1)
          + d_ref[...] * jnp.concatenate([ua_ref[0], ub_ref[0]], axis=1))
    glu = _bdot(_gelu_tanh(ys).astype(BF16), gw_ref[...]) + gb_ref[...]
    s5 = glu[:, :S5_D] * _sigmoid(glu[:, S5_D:])
    hy = jnp.concatenate([hy_ref[0, a] for a in range(hy_ref.shape[1])], axis=1)
    y = lax.dot_general(hy, wa_ref[...], _TN, preferred_element_type=F32)
    y = y + _bdot(s5.astype(BF16), wb_ref[...])
    o_ref[0] = x_ref[0] + m_ref[0] * y


def _outproj0(hy, ys, u, d, glu_w, glu_b, w_out, x, m2):
    B, L, D = x.shape
    tl = _pick(L, 512, 256, 128)
    lb = hy.shape[3]
    kb = tl // lb
    full = lambda b, i: (0, 0)
    tok = lambda w: pl.BlockSpec((1, tl, w), lambda b, i: (b, i, 0))
    return pl.pallas_call(
        _outproj0_kernel,
        grid=(B, L // tl),
        in_specs=[pl.BlockSpec((1, kb, HY_D, lb), lambda b, i: (b, i, 0, 0)),
                  tok(LANE), tok(LANE), tok(LANE), tok(LANE),
                  pl.BlockSpec((1, S5_D), full),
                  pl.BlockSpec((S5_D, 2 * S5_D), full),
                  pl.BlockSpec((1, 2 * S5_D), full),
                  pl.BlockSpec((HY_D, D), full),
                  pl.BlockSpec((S5_D, D), full),
                  tok(D),
                  pl.BlockSpec((1, 1, D), lambda b, i: (b, 0, 0))],
        out_specs=tok(D),
        out_shape=jax.ShapeDtypeStruct((B, L, D), F32),
        compiler_params=_cparams("parallel", "parallel"),
        name="outproj0",
    )(hy, ys[0], ys[1], u[0], u[1], d.reshape(1, S5_D), glu_w.astype(BF16), glu_b.reshape(1, -1),
      w_out[:HY_D].astype(BF16), w_out[HY_D:].astype(BF16), x, m2)


def _moe_kernel(x_ref, g_ref, sc_ref, sh_ref, m_ref, wrh_ref, wrl_ref, br_ref, tril_ref,
                wg_ref, wu_ref, wd_ref, fg_ref, o_ref, hs_scr, gs_scr, pt_scr, ys_scr, seg_smem,
                *, ne, final_norm):
    j = pl.program_id(2)
    nj = pl.num_programs(2)
    tl = x_ref.shape[1]
    S = hs_scr.shape[0]

    @pl.when(j == 0)
    def _():
        hn = _rms_mod(x_ref[0], g_ref[...], sc_ref[0], sh_ref[0])
        hi = hn.astype(BF16)
        lo = (hn - hi.astype(F32)).astype(BF16)
        wrh = wrh_ref[...]
        logits = _bdot(hi, wrh) + _bdot(lo, wrh) + _bdot(hi, wrl_ref[...]) + br_ref[...]
        lane = lax.broadcasted_iota(jnp.int32, logits.shape, 1)
        neg = -jnp.inf
        lgm = jnp.where(lane < MOE_GROUPS, logits, neg)
        gmax = jnp.max(lgm, axis=1, keepdims=True)
        p_top = 1.0 / jnp.sum(jnp.exp(lgm - gmax), axis=1, keepdims=True)
        gidx = jnp.min(jnp.where(lgm == gmax, lane, LANE), axis=1, keepdims=True)
        elane = lane - MOE_GROUPS
        in_group = (elane >= gidx * MOE_EPG) & (elane < (gidx + 1) * MOE_EPG)
        lem = jnp.where(in_group, logits, neg)
        v1 = jnp.max(lem, axis=1, keepdims=True)
        i1 = jnp.min(jnp.where(lem == v1, lane, LANE), axis=1, keepdims=True)
        lem2 = jnp.where(lane == i1, neg, lem)
        v2 = jnp.max(lem2, axis=1, keepdims=True)
        i2 = jnp.min(jnp.where(lem2 == v2, lane, LANE), axis=1, keepdims=True)
        e2 = jnp.exp(v2 - v1)
        w1 = p_top / (1.0 + e2)
        gate = jnp.where(lane == i1, w1, jnp.where(lane == i2, w1 * e2, 0.0))
        onehot = (lane == gidx).astype(F32)
        rank = _bdot(tril_ref[...], onehot.astype(BF16))
        off = jnp.int32(0)
        offv = jnp.zeros((1, LANE), F32)
        for g in range(MOE_GROUPS):
            n_g = jnp.sum(onehot[:, g:g + 1]).astype(jnp.int32)
            seg_smem[g] = off
            seg_smem[MOE_GROUPS + g] = n_g
            offv = jnp.where(lane[0:1, :] == g, off.astype(F32), offv)
            off = off + ((n_g + MOE_ALIGN - 1) // MOE_ALIGN) * MOE_ALIGN
        dlane = onehot * (offv + rank)
        dest = jnp.sum(dlane, axis=1, keepdims=True).astype(jnp.int32)
        slot = lax.broadcasted_iota(jnp.int32, (tl, S), 1)
        pt_scr[...] = (slot == dest).astype(BF16)
        d_hi = jnp.floor(dlane * (1.0 / 256.0))
        ones = jnp.ones((SUBLANE, LANE), BF16)
        drow = (256.0 * lax.dot_general(ones, d_hi.astype(BF16), _NT, preferred_element_type=F32)
                + lax.dot_general(ones, (dlane - 256.0 * d_hi).astype(BF16), _NT, preferred_element_type=F32))
        srow = lax.broadcasted_iota(jnp.int32, (S, tl), 0)
        p = (srow == drow[0:1, :].astype(jnp.int32)).astype(BF16)
        hs_scr[...] = _bdot(p, hi).astype(BF16)
        ghi = gate.astype(BF16)
        glo = (gate - ghi.astype(F32)).astype(BF16)
        gs_scr[...] = _bdot(p, ghi) + _bdot(p, glo)
        ys_scr[...] = jnp.zeros_like(ys_scr)

    grp = j // (MOE_EPG // ne)
    start = pl.multiple_of(seg_smem[grp], MOE_ALIGN)
    units = (seg_smem[MOE_GROUPS + grp] + MOE_UNIT - 1) // MOE_UNIT
    lane1 = lax.broadcasted_iota(jnp.int32, (1, LANE), 1)

    def block(r0, rows):
        hs = hs_scr[pl.ds(r0, rows), :]
        gsb = gs_scr[pl.ds(r0, rows), :]
        acc = None
        for e in range(ne):
            a = _bdot(hs, wg_ref[e])
            u = _bdot(hs, wu_ref[e])
            gcol = jnp.sum(jnp.where(lane1 == MOE_GROUPS + j * ne + e, gsb, 0.0), axis=1, keepdims=True)
            y = _bdot((a * _sigmoid(a) * u * gcol).astype(BF16), wd_ref[e])
            acc = y if acc is None else acc + y
        ys_scr[pl.ds(r0, rows), :] += acc

    @pl.when(units <= 2)
    def _():
        block(start, 2 * MOE_UNIT)

    @pl.when(units == 3)
    def _():
        block(start, 3 * MOE_UNIT)

    @pl.when(units >= 4)
    def _():
        def body(i, carry):
            block(pl.multiple_of(start + i * (2 * MOE_UNIT), MOE_ALIGN), 2 * MOE_UNIT)
            return carry

        lax.fori_loop(0, (units + 1) // 2, body, 0)

    @pl.when(j == nj - 1)
    def _():
        moe = _bdot(pt_scr[...], ys_scr[...].astype(BF16))
        out = x_ref[0] + m_ref[0] * moe
        if final_norm:
            ms = jnp.mean(out * out, axis=-1, keepdims=True)
            out = out * lax.rsqrt(ms + EPS) * fg_ref[...]
        o_ref[0] = out


def _moe(x, g2, sc, sh, m5, wrh, wrl, br, wg, wu, wd, final_g=None):
    B, L, D = x.shape
    ne = 4
    tl = _pick(L, 1024, 512, 256, 128)
    S = tl + 3 * MOE_UNIT
    final_norm = final_g is not None
    fg = (final_g if final_norm else jnp.ones((D,), F32)).reshape(1, D)
    idx = jnp.arange(tl, dtype=jnp.int32)
    tril = (idx[None, :] < idx[:, None]).astype(BF16)
    full = lambda b, i, j: (0, 0)
    tok = pl.BlockSpec((1, tl, D), lambda b, i, j: (b, i, 0))
    vec = pl.BlockSpec((1, 1, D), lambda b, i, j: (b, 0, 0))
    wspec = lambda k, n: pl.BlockSpec((ne, k, n), lambda b, i, j: (j, 0, 0))
    return pl.pallas_call(
        functools.partial(_moe_kernel, ne=ne, final_norm=final_norm),
        grid=(B, L // tl, MOE_EXPERTS // ne),
        in_specs=[tok, pl.BlockSpec((1, D), full), vec, vec, vec,
                  pl.BlockSpec((D, LANE), full), pl.BlockSpec((D, LANE), full), pl.BlockSpec((1, LANE), full),
                  pl.BlockSpec((tl, tl), full, pipeline_mode=pl.Buffered(1)),
                  wspec(D, MOE_HIDDEN), wspec(D, MOE_HIDDEN), wspec(MOE_HIDDEN, D),
                  pl.BlockSpec((1, D), full)],
        out_specs=tok,
        out_shape=jax.ShapeDtypeStruct((B, L, D), F32),
        scratch_shapes=[pltpu.VMEM((S, D), BF16),
                        pltpu.VMEM((S, LANE), F32),
                        pltpu.VMEM((tl, S), BF16),
                        pltpu.VMEM((S, D), F32),
                        pltpu.SMEM((2 * MOE_GROUPS,), jnp.int32)],
        compiler_params=_cparams("parallel", "parallel", "arbitrary"),
        name="moe",
    )(x, g2, sc, sh, m5, wrh, wrl, br, tril, wg, wu, wd, fg)


def _moe_params(wg, bg, we, be, w_gate, w_up, w_down):
    D = wg.shape[0]
    pad = LANE - MOE_GROUPS - MOE_EXPERTS
    wr = jnp.pad(jnp.concatenate([wg, we], axis=1), ((0, 0), (0, pad)))
    br = jnp.pad(jnp.concatenate([bg, be]), (0, pad)).reshape(1, LANE)
    wrh = wr.astype(BF16)
    wrl = (wr - wrh.astype(F32)).astype(BF16)
    return (wrh, wrl, br,
            w_gate.reshape(MOE_EXPERTS, D, MOE_HIDDEN).astype(BF16),
            w_up.reshape(MOE_EXPERTS, D, MOE_HIDDEN).astype(BF16),
            w_down.reshape(MOE_EXPERTS, MOE_HIDDEN, D).astype(BF16))


def _inproj1_kernel(x_ref, g_ref, sc_ref, sh_ref, w_ref, o_ref, h_scr):
    @pl.when(pl.program_id(2) == 0)
    def _():
        h_scr[...] = _rms_mod(x_ref[0], g_ref[...], sc_ref[0], sh_ref[0]).astype(BF16)

    o_ref[0] = _bdot(h_scr[...], w_ref[...]).astype(o_ref.dtype)


def _inproj1(x, g, sc, sh, w):
    B, L, D = x.shape
    N = w.shape[1]
    tl = _pick(L, 1024, 512, 256, 128)
    tn = _pick(N, 2560, 1280, 1024, 512)
    return pl.pallas_call(
        _inproj1_kernel,
        grid=(B, L // tl, N // tn),
        in_specs=[pl.BlockSpec((1, tl, D), lambda b, i, j: (b, i, 0)),
                  pl.BlockSpec((1, D), lambda b, i, j: (0, 0)),
                  pl.BlockSpec((1, 1, D), lambda b, i, j: (b, 0, 0)),
                  pl.BlockSpec((1, 1, D), lambda b, i, j: (b, 0, 0)),
                  pl.BlockSpec((D, tn), lambda b, i, j: (0, j))],
        out_specs=pl.BlockSpec((1, tl, tn), lambda b, i, j: (b, i, j)),
        out_shape=jax.ShapeDtypeStruct((B, L, N), BF16),
        scratch_shapes=[pltpu.VMEM((tl, D), BF16)],
        compiler_params=_cparams("parallel", "parallel", "arbitrary"),
        name="inproj1",
    )(x, g, sc, sh, w)


def _gla_masks(C):
    t = np.arange(C)[:, None]
    s = np.arange(C)[None, :]
    fwd = []
    hs = 1
    while hs < C:
        fwd.append(((t // (2 * hs)) == (s // (2 * hs))) & ((t & hs) != 0) & ((s & hs) == 0))
        hs *= 2
    fwd = np.stack(fwd).astype(np.float32)
    return jnp.asarray(np.stack([fwd, fwd.transpose(0, 2, 1)])), jnp.asarray(np.eye(C, dtype=np.float32))


def _gla_direction(pq, v, z, lb, masks, eye, st_ref, rev):
    C = pq.shape[0]
    heads = [slice(h * LANE, (h + 1) * LANE) for h in range(C_HEADS)]
    q = pq * _sigmoid(pq)
    f = lb + (1.0 - lb) * _sigmoid(z)
    k = 1.0 - f
    g = jnp.log2(f)
    row = lax.broadcasted_iota(jnp.int32, (C, C_D), 0)
    qb = q.astype(BF16)
    kb = k.astype(BF16)
    att = [eye * lax.dot_general(qb[:, hd], kb[:, hd], _NT, preferred_element_type=F32) for hd in heads]
    pinc = g
    tot = g
    hs = 1
    lvl = 0
    while hs < C:
        aq = pinc
        ak = tot - pinc
        qe = (q * jnp.exp2(aq)).astype(BF16)
        ke = (k * jnp.exp2(ak)).astype(BF16)
        for h, hd in enumerate(heads):
            blk = lax.dot_general(qe[:, hd], ke[:, hd], _NT, preferred_element_type=F32)
            att[h] = jnp.where(masks[lvl], blk, att[h])
        if hs < SUBLANE:
            odd = (row & hs) != 0
            t3 = tot.reshape(C // SUBLANE, SUBLANE, C_D)
            tprev = pltpu.roll(t3, hs, 1).reshape(C, C_D)
            tnext = pltpu.roll(t3, SUBLANE - hs, 1).reshape(C, C_D)
            pinc = pinc + (jnp.where(odd, 0.0, tnext) if rev else jnp.where(odd, tprev, 0.0))
            tot = tot + jnp.where(odd, tprev, tnext)
        else:
            w = hs // SUBLANE
            tile = lambda a, i: a[i * SUBLANE:(i + 1) * SUBLANE]
            new_p, new_t = [], []
            for i in range(C // SUBLANE):
                sib = i - w if (i // w) % 2 == 1 else i + w
                grows = ((i // w) % 2 == 1) != rev
                new_p.append(tile(pinc, i) + tile(tot, sib) if grows else tile(pinc, i))
                new_t.append(tile(tot, i) + tile(tot, sib))
            pinc = jnp.concatenate(new_p, axis=0)
            tot = jnp.concatenate(new_t, axis=0)
        hs *= 2
        lvl += 1
    q_dec = pinc
    k_dec = tot - pinc
    vb = v.astype(BF16)
    qd = (q * jnp.exp2(q_dec)).astype(BF16)
    kd = (k * jnp.exp2(k_dec)).astype(BF16)
    keep = jnp.exp2(tot[0:1, :])
    outs = []
    for h, hd in enumerate(heads):
        st = st_ref[h]
        o = _bdot(att[h].astype(BF16), vb[:, hd])
        o = o + lax.dot_general(qd[:, hd], st.astype(BF16), _NT, preferred_element_type=F32)
        upd = lax.dot_general(vb[:, hd], kd[:, hd], _TN, preferred_element_type=F32)
        st_ref[h] = st * keep[:, hd] + upd
        outs.append(o)
    return jnp.concatenate(outs, axis=1)


def _gla_kernel(qf_ref, if_ref, zf_ref, qb_ref, ib_ref, zb_ref, lb_ref, mk_ref, eye_ref, s0_ref,
                of_ref, ob_ref, sfin_ref, st_scr):
    c = pl.program_id(1)

    @pl.when(c == 0)
    def _():
        st_scr[...] = s0_ref[0]

    eye = eye_ref[...]
    nlev = mk_ref.shape[1]
    for d, (q_ref, i_ref, z_ref, o_ref) in enumerate(((qf_ref, if_ref, zf_ref, of_ref),
                                                      (qb_ref, ib_ref, zb_ref, ob_ref))):
        masks = [mk_ref[d, lv] > 0.5 for lv in range(nlev)]
        o = _gla_direction(q_ref[0].astype(F32), i_ref[0].astype(F32), z_ref[0].astype(F32),
                           lb_ref[d:d + 1, :], masks, eye, st_scr.at[d], rev=(d == 1))
        o_ref[0] = o.astype(o_ref.dtype)

    @pl.when(c == pl.num_programs(1) - 1)
    def _():
        sfin_ref[0] = st_scr[...]


def _gla(p, lb, s0):
    B, L, _ = p.shape
    C = GLA_CHUNK
    nch = L // C
    masks, eye = _gla_masks(C)
    fcol = lambda k: pl.BlockSpec((1, C, C_D), lambda b, c: (b, c, k))
    bcol = lambda k: pl.BlockSpec((1, C, C_D), lambda b, c: (b, nch - 1 - c, k))
    st_spec = pl.BlockSpec((1, 2, C_HEADS, LANE, LANE), lambda b, c: (b, 0, 0, 0, 0))
    return pl.pallas_call(
        _gla_kernel,
        grid=(B, nch),
        in_specs=[fcol(0), fcol(2), fcol(3), bcol(0), bcol(2), bcol(4),
                  pl.BlockSpec((2, C_D), lambda b, c: (0, 0)),
                  pl.BlockSpec(masks.shape, lambda b, c: (0, 0, 0, 0)),
                  pl.BlockSpec((C, C), lambda b, c: (0, 0)),
                  st_spec],
        out_specs=[fcol(0), bcol(0), st_spec],
        out_shape=[jax.ShapeDtypeStruct((B, L, C_D), BF16),
                   jax.ShapeDtypeStruct((B, L, C_D), BF16),
                   jax.ShapeDtypeStruct((B, 2, C_HEADS, LANE, LANE), F32)],
        scratch_shapes=[pltpu.VMEM((2, C_HEADS, LANE, LANE), F32)],
        compiler_params=_cparams("parallel", "arbitrary"),
        name="gla",
    )(p, p, p, p, p, p, lb, masks, eye, s0)


def _outproj1_kernel(of_ref, ob_ref, g_ref, ng_ref, w_ref, x_ref, m_ref, o_ref):
    o = of_ref[0].astype(F32) + ob_ref[0].astype(F32)
    parts = []
    for h in range(C_HEADS):
        oh = o[:, h * LANE:(h + 1) * LANE]
        ms = jnp.mean(oh * oh, axis=-1, keepdims=True)
        parts.append(oh * lax.rsqrt(ms + EPS))
    on = jnp.concatenate(parts, axis=1) * ng_ref[...] * _sigmoid(g_ref[0].astype(F32))
    y = _bdot(on.astype(BF16), w_ref[...])
    o_ref[0] = x_ref[0] + m_ref[0] * y


def _outproj1(o_f, o_b, p, norm_g, w_out, x, m2):
    B, L, D = x.shape
    tl = _pick(L, 512, 256, 128)
    full = lambda b, i: (0, 0)
    tok = pl.BlockSpec((1, tl, C_D), lambda b, i: (b, i, 0))
    return pl.pallas_call(
        _outproj1_kernel,
        grid=(B, L // tl),
        in_specs=[tok, tok,
                  pl.BlockSpec((1, tl, C_D), lambda b, i: (b, i, 1)),
                  pl.BlockSpec((1, C_D), full),
                  pl.BlockSpec((C_D, D), full),
                  pl.BlockSpec((1, tl, D), lambda b, i: (b, i, 0)),
                  pl.BlockSpec((1, 1, D), lambda b, i: (b, 0, 0))],
        out_specs=pl.BlockSpec((1, tl, D), lambda b, i: (b, i, 0)),
        out_shape=jax.ShapeDtypeStruct((B, L, D), F32),
        compiler_params=_cparams("parallel", "parallel"),
        name="outproj1",
    )(o_f, o_b, p, norm_g.reshape(1, C_D), w_out.astype(BF16), x, m2)


def _mods(cmat, w, b, nb):
    R = cmat.shape[0]
    pad = (-R) % SUBLANE
    m = _modvec(jnp.pad(cmat, ((0, pad), (0, 0))), w, b)[:R]
    m = jnp.broadcast_to(m, (nb, m.shape[1])) if R == 1 else m
    return [m[:, None, k * D_MODEL:(k + 1) * D_MODEL] for k in range(6)]


def kernel(x, c, ctx, c_ctx, mod_w, mod_b, norm1_g, norm2_g, final_g,
           ab_w_in, ab_w_out, hy_conv_w, hy_conv_b, hy_fw1, hy_fb1, hy_ff1,
           hy_fw2, hy_fb2, hy_ff2, hy_fw3, hy_bias,
           s5_lam_re, s5_lam_im, s5_log_step, s5_b_re, s5_b_im, s5_c_re, s5_c_im,
           s5_d, s5_glu_w, s5_glu_b,
           c_w_in, c_w_out, c_lower_bounds, c_norm_g,
           moe_wg, moe_bg, moe_we, moe_be, moe_w_gate, moe_w_up, moe_w_down):
    B, L, D = x.shape
    row = lambda t: t.reshape(1, -1)

    m = _mods(c, mod_w[0], mod_b[0], B)
    mc = _mods(c_ctx[None, :], mod_w[0], mod_b[0], B)
    w_in = ab_w_in[0]
    wt_hy = w_in[:, :3 * HY_D].T.astype(BF16)
    w_s5 = w_in[:, 3 * HY_D:].astype(BF16)
    g1 = row(norm1_g[0])
    pt_c, *u_c = _inproj0(ctx, g1, 1.0 + mc[1], mc[0], wt_hy, w_s5)
    pt_l, *u_l = _inproj0(x, g1, 1.0 + m[1], m[0], wt_hy, w_s5)

    ops = _s5_operators(s5_lam_re[0], s5_lam_im[0], s5_log_step[0], s5_b_re[0], s5_b_im[0],
                        s5_c_re[0], s5_c_im[0])
    *ys_c, s_fin = _s5_pass(*u_c, ops, jnp.zeros((B, 2, 2 * S5_P), F32))
    *ys_l, _ = _s5_pass(*u_l, ops, s_fin)

    filt = (hy_fw1[0], hy_fb1[0], hy_ff1[0], hy_fw2[0], hy_fb2[0], hy_ff2[0], hy_fw3[0])
    hy_c = _hyena(pt_c, hy_conv_w[0], hy_conv_b[0], hy_bias[0], filt)
    hy_l = _hyena(pt_l, hy_conv_w[0], hy_conv_b[0], hy_bias[0], filt)

    x = _outproj0(hy_l, ys_l, u_l, s5_d[0], s5_glu_w[0], s5_glu_b[0], ab_w_out[0], x, m[2])
    ctx = _outproj0(hy_c, ys_c, u_c, s5_d[0], s5_glu_w[0], s5_glu_b[0], ab_w_out[0], ctx, mc[2])

    mp = _moe_params(moe_wg[0], moe_bg[0], moe_we[0], moe_be[0],
                     moe_w_gate[0], moe_w_up[0], moe_w_down[0])
    g2 = row(norm2_g[0])
    x = _moe(x, g2, 1.0 + m[4], m[3], m[5], *mp)
    ctx = _moe(ctx.reshape(1, -1, D), g2, 1.0 + mc[4][:1], mc[3][:1], mc[5][:1], *mp).reshape(ctx.shape)

    m = _mods(c, mod_w[1], mod_b[1], B)
    mc = _mods(c_ctx[None, :], mod_w[1], mod_b[1], B)
    sm = jax.nn.softmax(c_lower_bounds.astype(F32), axis=1)
    lower = (jnp.cumsum(sm, axis=1) - sm[:, :1])[:, 1]
    g1 = row(norm1_g[1])
    w1 = c_w_in[0].astype(BF16)
    p_c = _inproj1(ctx, g1, 1.0 + mc[1], mc[0], w1)
    p_l = _inproj1(x, g1, 1.0 + m[1], m[0], w1)
    zeros = jnp.zeros((B, 2, C_HEADS, C_HEAD_DIM, C_HEAD_DIM), F32)
    _, _, s_ctx = _gla(p_c, lower, zeros)
    o_f, o_b, _ = _gla(p_l, lower, s_ctx)
    x = _outproj1(o_f, o_b, p_l, c_norm_g[0], c_w_out[0], x, m[2])

    mp = _moe_params(moe_wg[1], moe_bg[1], moe_we[1], moe_be[1],
                     moe_w_gate[1], moe_w_up[1], moe_w_down[1])
    return _moe(x, row(norm2_g[1]), 1.0 + m[4], m[3], m[5], *mp, final_g=final_g)
```

```python
import functools
import math

import numpy as np
import jax
import jax.numpy as jnp
from jax import lax
from jax.experimental import pallas as pl
from jax.experimental.pallas import tpu as pltpu

F32 = jnp.float32
BF16 = jnp.bfloat16
EPS = 1e-6
HIGHEST = lax.Precision.HIGHEST

D_MODEL = 1024
HY_D = 768
HY_ORDER = 2
HY_EMB = 33
HY_BANDS = (HY_EMB - 1) // 2
HY_DECAY_SHORT = 0.3
HY_DECAY_LONG = 1.5
HY_TARGET = 1e-2
S5_D = 256
S5_GROUP = 16
S5_GROUPS = S5_D // S5_GROUP
S5_STATE = 64
S5_P = S5_GROUPS * S5_STATE
S5_HALVES = S5_D // 128
S5_PH = S5_P // S5_HALVES
C_HEADS = 8
C_HEAD_DIM = 128
C_D = C_HEADS * C_HEAD_DIM
MOE_GROUPS = 4
MOE_EPG = 8
MOE_EXPERTS = MOE_GROUPS * MOE_EPG
MOE_HIDDEN = 256
MOE_UNIT = 128
MOE_ALIGN = 16

LANE = 128
SUBLANE = 8
HY_BLK = 512
S5_CHUNK = 16
GLA_CHUNK = 128
VMEM_LIMIT = 56 * 1024 * 1024


def _cparams(*sem):
    return pltpu.CompilerParams(dimension_semantics=sem, vmem_limit_bytes=VMEM_LIMIT)


def _pick(n, *cands):
    for c in cands:
        if n % c == 0:
            return c
    return n


def _rms_mod(xv, g, sc, sh):
    ms = jnp.mean(xv * xv, axis=-1, keepdims=True)
    return xv * lax.rsqrt(ms + EPS) * g * sc + sh


def _sigmoid(v):
    return 1.0 / (1.0 + jnp.exp(-v))


def _bdot(a, b):
    return jnp.dot(a, b, preferred_element_type=F32)


_NT = (((1,), (1,)), ((), ()))
_TN = (((0,), (0,)), ((), ()))


def _mm_kernel(a_ref, b_ref, o_ref):
    o_ref[...] = _bdot(a_ref[...].astype(BF16), b_ref[...]).astype(o_ref.dtype)


def _mm(a, b, out_dtype=F32, name="mm"):
    M, K = a.shape
    N = b.shape[1]
    tm = _pick(M, 768, 512, 256, 128)
    tn = _pick(N, 512, 256, 128)
    return pl.pallas_call(
        _mm_kernel,
        grid=(M // tm, N // tn),
        in_specs=[pl.BlockSpec((tm, K), lambda i, j: (i, 0)),
                  pl.BlockSpec((K, tn), lambda i, j: (0, j))],
        out_specs=pl.BlockSpec((tm, tn), lambda i, j: (i, j)),
        out_shape=jax.ShapeDtypeStruct((M, N), out_dtype),
        compiler_params=_cparams("parallel", "parallel"),
        name=name,
    )(a, b.astype(BF16))


def _modvec_kernel(c_ref, w_ref, b_ref, o_ref):
    cv = c_ref[...]
    sc = cv * _sigmoid(cv)
    o_ref[...] = jnp.dot(sc, w_ref[...], preferred_element_type=F32, precision=HIGHEST) + b_ref[...]


def _modvec(cvec, w, b):
    R, D = cvec.shape
    N = w.shape[1]
    tn = _pick(N, 512, 256, 128)
    return pl.pallas_call(
        _modvec_kernel,
        grid=(N // tn,),
        in_specs=[pl.BlockSpec((R, D), lambda j: (0, 0)),
                  pl.BlockSpec((D, tn), lambda j: (0, j)),
                  pl.BlockSpec((1, tn), lambda j: (0, j))],
        out_specs=pl.BlockSpec((R, tn), lambda j: (0, j)),
        out_shape=jax.ShapeDtypeStruct((R, N), F32),
        compiler_params=_cparams("parallel"),
        name="modvec",
    )(cvec, w, b.reshape(1, N))


def _inproj0_kernel(x_ref, g_ref, sc_ref, sh_ref, wt_ref, w2_ref, pt_ref, ua_ref, ub_ref):
    h = _rms_mod(x_ref[0], g_ref[...], sc_ref[0], sh_ref[0]).astype(BF16)
    pt_ref[0] = lax.dot_general(wt_ref[...], h, _NT, preferred_element_type=F32).astype(BF16)
    u = _bdot(h, w2_ref[...])
    ua_ref[0] = u[:, :LANE]
    ub_ref[0] = u[:, LANE:]


def _inproj0(x, g, sc, sh, wt_hy, w_s5):
    B, L, D = x.shape
    C = wt_hy.shape[0]
    tl = _pick(L, 512, 256, 128)
    return pl.pallas_call(
        _inproj0_kernel,
        grid=(B, L // tl),
        in_specs=[pl.BlockSpec((1, tl, D), lambda b, i: (b, i, 0)),
                  pl.BlockSpec((1, D), lambda b, i: (0, 0)),
                  pl.BlockSpec((1, 1, D), lambda b, i: (b, 0, 0)),
                  pl.BlockSpec((1, 1, D), lambda b, i: (b, 0, 0)),
                  pl.BlockSpec((C, D), lambda b, i: (0, 0)),
                  pl.BlockSpec((D, S5_D), lambda b, i: (0, 0))],
        out_specs=[pl.BlockSpec((1, C, tl), lambda b, i: (b, 0, i)),
                   pl.BlockSpec((1, tl, LANE), lambda b, i: (b, i, 0)),
                   pl.BlockSpec((1, tl, LANE), lambda b, i: (b, i, 0))],
        out_shape=[jax.ShapeDtypeStruct((B, C, L), BF16),
                   jax.ShapeDtypeStruct((B, L, LANE), F32),
                   jax.ShapeDtypeStruct((B, L, LANE), F32)],
        compiler_params=_cparams("parallel", "parallel"),
        name="inproj0",
    )(x, g, sc, sh, wt_hy, w_s5)


def _shortconv_kernel(p1_ref, p2_ref, p3_ref, w_ref, b_ref, x1_ref, x2_ref, v_ref):
    L = p1_ref.shape[2]
    nb, lb = x1_ref.shape[1], x1_ref.shape[3]
    lane = lax.broadcasted_iota(jnp.int32, (1, L), 1)
    first = lane == 0
    last = lane == L - 1
    for k, (p_ref, o_ref) in enumerate(((p1_ref, x1_ref), (p2_ref, x2_ref), (p3_ref, v_ref))):
        u = p_ref[0].astype(F32)
        w = w_ref[k]
        prev = jnp.where(first, 0.0, pltpu.roll(u, 1, 1))
        nxt = jnp.where(last, 0.0, pltpu.roll(u, L - 1, 1))
        y = (w[:, 0:1] * prev + w[:, 1:2] * u + w[:, 2:3] * nxt + b_ref[k]).astype(BF16)
        for j in range(nb):
            o_ref[0, j] = y[:, j * lb:(j + 1) * lb]


def _shortconv(pt, conv_w, conv_b, lb):
    B, C3, L = pt.shape
    nb = L // lb
    tc = 128
    nc = HY_D // tc
    w = conv_w.T.reshape(3, HY_D, 3)
    b = conv_b.reshape(3, HY_D, 1)
    specs = [pl.BlockSpec((1, tc, L), (lambda b_, i, k=k: (b_, k * nc + i, 0))) for k in range(3)]
    o_spec = pl.BlockSpec((1, nb, tc, lb), lambda b_, i: (b_, 0, i, 0))
    return pl.pallas_call(
        _shortconv_kernel,
        grid=(B, nc),
        in_specs=specs + [pl.BlockSpec((3, tc, 3), lambda b_, i: (0, i, 0)),
                          pl.BlockSpec((3, tc, 1), lambda b_, i: (0, i, 0))],
        out_specs=[o_spec, o_spec, o_spec],
        out_shape=[jax.ShapeDtypeStruct((B, nb, HY_D, lb), BF16)] * 3,
        compiler_params=_cparams("parallel", "parallel"),
        name="shortconv",
    )(pt, pt, pt, w, b)


def _filter_kernel(z_ref, w1_ref, b1_ref, f1_ref, w2_ref, b2_ref, f2_ref, w3_ref, dec_ref, o_ref):
    dot = functools.partial(jnp.dot, preferred_element_type=F32, precision=HIGHEST)
    z = z_ref[...]
    hdn = jnp.sin(f1_ref[...] * (dot(w1_ref[...], z) + b1_ref[...]))
    hdn = jnp.sin(f2_ref[...] * (dot(w2_ref[...], hdn) + b2_ref[...]))
    t = z[0:1, :]
    o_ref[...] = dot(w3_ref[...], hdn) * jnp.exp(-dec_ref[...] * t)


def _hyena_filter(L, fw1, fb1, ff1, fw2, fb2, ff2, fw3):
    pos = jnp.arange(L, dtype=F32)
    t = pos / max(L - 1, 1)
    w = 2.0 * math.pi * pos / L
    bands = jnp.linspace(1e-4, HY_BANDS - 1, HY_BANDS, dtype=F32)
    ang = bands[:, None] * w[None, :]
    z = jnp.concatenate([t[None, :], jnp.cos(ang), -jnp.sin(ang)], axis=0)
    z = jnp.pad(z, ((0, LANE - HY_EMB), (0, 0)))
    w1 = jnp.pad(fw1, ((0, LANE - HY_EMB), (0, 0))).T
    deltas = jnp.abs(jnp.linspace(math.log(HY_TARGET) / HY_DECAY_LONG,
                                  math.log(HY_TARGET) / HY_DECAY_SHORT, HY_D, dtype=F32))
    dec = jnp.tile(deltas, HY_ORDER * 2).reshape(-1, 1)
    nf = fw1.shape[1]
    No = fw3.shape[1]
    tl = _pick(L, 512, 256)
    full = lambda i: (0, 0)
    col = lambda v: v.reshape(nf, 1)
    return pl.pallas_call(
        _filter_kernel,
        grid=(L // tl,),
        in_specs=[pl.BlockSpec((LANE, tl), lambda i: (0, i)),
                  pl.BlockSpec((nf, LANE), full), pl.BlockSpec((nf, 1), full), pl.BlockSpec((nf, 1), full),
                  pl.BlockSpec((nf, nf), full), pl.BlockSpec((nf, 1), full), pl.BlockSpec((nf, 1), full),
                  pl.BlockSpec((No, nf), full), pl.BlockSpec((No, 1), full)],
        out_specs=pl.BlockSpec((No, tl), lambda i: (0, i)),
        out_shape=jax.ShapeDtypeStruct((No, L), F32),
        compiler_params=_cparams("parallel"),
        name="hyena_filter",
    )(z, w1, col(fb1), col(ff1), fw2.T, col(fb2), col(ff2), fw3.T, dec)


def _dft_mats(L, blk):
    N = 2 * L
    W = 64
    j = jnp.arange(2 * L, dtype=jnp.int32)
    kk = (j // (2 * blk)) * blk + j % blk
    is_im = (j // blk) % 2 == 1
    nyq = is_im & (kk == 0)
    hi = jnp.arange(L // W, dtype=jnp.int32) * W
    lo = jnp.arange(W, dtype=jnp.int32)
    alt = (1 - 2 * (lo % 2)).astype(F32)
    th = 2.0 * math.pi / N
    ah = ((hi[:, None] * kk[None, :]) % N).astype(F32) * th
    al = ((lo[:, None] * kk[None, :]) % N).astype(F32) * th
    ch, sh, cl, sl = jnp.cos(ah), jnp.sin(ah), jnp.cos(al), jnp.sin(al)
    c = ch[:, None, :] * cl[None] - sh[:, None, :] * sl[None]
    s = sh[:, None, :] * cl[None] + ch[:, None, :] * sl[None]
    wf = jnp.where(nyq[None, None, :], alt[None, :, None], jnp.where(is_im[None, None, :], -s, c))
    ct = ch.T[:, :, None] * cl.T[:, None, :] - sh.T[:, :, None] * sl.T[:, None, :]
    st = sh.T[:, :, None] * cl.T[:, None, :] + ch.T[:, :, None] * sl.T[:, None, :]
    scale = jnp.where(kk == 0, 1.0 / N, 2.0 / N).astype(F32)[:, None, None]
    wi = jnp.where(nyq[:, None, None], alt[None, None, :], jnp.where(is_im[:, None, None], -st, ct)) * scale
    return wf.reshape(L, 2 * L).astype(BF16), wi.reshape(2 * L, L).astype(BF16)


def _segdft_kernel(a_ref, w_ref, o_ref):
    o_ref[...] = _bdot(a_ref[...].astype(BF16), w_ref[...])


def _segdft(ht, wf, lb):
    R, L = ht.shape
    tm = 256
    return pl.pallas_call(
        _segdft_kernel,
        grid=(R // tm, L // lb),
        in_specs=[pl.BlockSpec((tm, lb), lambda i, m: (i, m)),
                  pl.BlockSpec((lb, 2 * lb), lambda i, m: (0, 0))],
        out_specs=pl.BlockSpec((tm, 2 * lb), lambda i, m: (i, m)),
        out_shape=jax.ShapeDtypeStruct((R, 2 * L), F32),
        compiler_params=_cparams("parallel", "parallel"),
        name="filter_segdft",
    )(ht, wf)


def _filter_blocks_kernel(hf_ref, hb_ref, af_ref, ab_ref, kr_ref, ki_ref, kn_ref, *, nb, lb):
    hf = hf_ref[...]
    hb = hb_ref[...]
    lag = lax.broadcasted_iota(jnp.int32, (1, hf.shape[1]), 1)
    hbz = jnp.where(lag == 0, 0.0, hb)
    nrm = lax.rsqrt(jnp.sum(hf * hf, axis=1, keepdims=True) + jnp.sum(hbz * hbz, axis=1, keepdims=True))
    k = lax.broadcasted_iota(jnp.int32, (1, lb), 1)
    sgn = jnp.where(k % 2 == 0, 1.0, -1.0)
    bin0 = k == 0

    def seg(ref, m):
        re = ref[:, m * 2 * lb:m * 2 * lb + lb]
        imp = ref[:, m * 2 * lb + lb:(m + 1) * 2 * lb]
        return re, jnp.where(bin0, 0.0, imp), imp[:, 0:1]

    def emit(idx, re, im, nyq):
        kr_ref[0, idx] = re * nrm
        ki_ref[0, idx] = im * nrm
        kn_ref[0, idx] = nyq * nrm

    fr, fi, fn = seg(af_ref, 0)
    br, bi, bn = seg(ab_ref, 0)
    b0 = hb[:, 0:1]
    emit(nb - 1, fr + br - b0, fi - bi, fn + bn - b0)
    for d in range(1, nb):
        for ref, taps, conj, idx in ((af_ref, hf, 1.0, nb - 1 + d), (ab_ref, hb, -1.0, nb - 1 - d)):
            r1, i1, n1 = seg(ref, d)
            r0, i0, n0 = seg(ref, d - 1)
            e0 = taps[:, (d - 1) * lb:(d - 1) * lb + 1]
            emit(idx, r1 + sgn * (r0 - e0), conj * (i1 + sgn * i0), n1 + n0 - e0)


def _filter_blocks(ht, seg, lb):
    R, L = ht.shape
    nb = L // lb
    nlag = 2 * nb - 1
    tm = 128
    nc = HY_D // tm
    fwd = lambda i: ((i // nc) * 2 * nc + i % nc, 0)
    bwd = lambda i: ((i // nc) * 2 * nc + nc + i % nc, 0)
    kspec = pl.BlockSpec((1, nlag, tm, lb), lambda i: (i // nc, 0, i % nc, 0))
    return pl.pallas_call(
        functools.partial(_filter_blocks_kernel, nb=nb, lb=lb),
        grid=(HY_ORDER * nc,),
        in_specs=[pl.BlockSpec((tm, L), fwd), pl.BlockSpec((tm, L), bwd),
                  pl.BlockSpec((tm, 2 * L), fwd), pl.BlockSpec((tm, 2 * L), bwd)],
        out_specs=[kspec, kspec, pl.BlockSpec((1, nlag, tm, 1), lambda i: (i // nc, 0, i % nc, 0))],
        out_shape=[jax.ShapeDtypeStruct((HY_ORDER, nlag, HY_D, lb), F32),
                   jax.ShapeDtypeStruct((HY_ORDER, nlag, HY_D, lb), F32),
                   jax.ShapeDtypeStruct((HY_ORDER, nlag, HY_D, 1), F32)],
        compiler_params=_cparams("parallel"),
        name="filter_blocks",
    )(ht, ht, seg, seg)


def _hyconv_kernel(v_ref, g_ref, bias_ref, kr_ref, ki_ref, kn_ref, wf_ref, wi_ref, o_ref,
                   vr_scr, vi_scr, vn_scr, y_scr):
    nb, ct, lb = v_ref.shape[1], v_ref.shape[2], v_ref.shape[3]
    bin0 = lax.broadcasted_iota(jnp.int32, (1, lb), 1) == 0
    v = v_ref[0]
    acc = _bdot(v.reshape(nb * ct, lb), wf_ref[...])
    imp = acc[:, lb:]
    vr_scr[...] = acc[:, :lb].reshape(nb, ct, lb)
    vi_scr[...] = jnp.where(bin0, 0.0, imp).reshape(nb, ct, lb)
    vn_scr[...] = imp[:, 0:1].reshape(nb, ct, 1)

    rt = min(ct, (8 * SUBLANE * LANE) // lb)
    nr = ct // rt

    def out_rows(n, carry):
        i = n // nr
        rows = pl.ds(pl.multiple_of((n % nr) * rt, rt), rt)
        yr = jnp.zeros((rt, lb), F32)
        yi = jnp.zeros((rt, lb), F32)
        yn = jnp.zeros((rt, 1), F32)
        for j in range(nb):
            d = i - j + (nb - 1)
            kr = kr_ref[0, d, rows, :]
            ki = ki_ref[0, d, rows, :]
            vr = vr_scr[j, rows, :]
            vi = vi_scr[j, rows, :]
            yr = yr + (vr * kr - vi * ki)
            yi = yi + (vr * ki + vi * kr)
            yn = yn + vn_scr[j, rows, :] * kn_ref[0, d, rows, :]
        y_scr[i, rows, :lb] = yr.astype(BF16)
        y_scr[i, rows, lb:] = jnp.where(bin0, yn, yi).astype(BF16)
        return carry

    lax.fori_loop(0, nb * nr, out_rows, 0)
    conv = _bdot(y_scr[...].reshape(nb * ct, 2 * lb), wi_ref[...]).reshape(nb, ct, lb)
    o_ref[0] = (g_ref[0].astype(F32) * (conv + v.astype(F32) * bias_ref[...])).astype(o_ref.dtype)


def _hyconv(v, gate, bias, kr, ki, kn, order, wf, wi):
    B, nb, C, lb = v.shape
    nlag = kr.shape[1]
    ct = 128
    blk = pl.BlockSpec((1, nb, ct, lb), lambda c, b: (b, 0, c, 0))
    kspec = pl.BlockSpec((1, nlag, ct, lb), lambda c, b: (order, 0, c, 0))
    return pl.pallas_call(
        _hyconv_kernel,
        grid=(C // ct, B),
        in_specs=[blk, blk,
                  pl.BlockSpec((ct, 1), lambda c, b: (c, 0)),
                  kspec, kspec,
                  pl.BlockSpec((1, nlag, ct, 1), lambda c, b: (order, 0, c, 0)),
                  pl.BlockSpec((lb, 2 * lb), lambda c, b: (0, 0)),
                  pl.BlockSpec((2 * lb, lb), lambda c, b: (0, 0))],
        out_specs=blk,
        out_shape=jax.ShapeDtypeStruct((B, nb, C, lb), BF16),
        scratch_shapes=[pltpu.VMEM((nb, ct, lb), F32), pltpu.VMEM((nb, ct, lb), F32),
                        pltpu.VMEM((nb, ct, 1), F32), pltpu.VMEM((nb, ct, 2 * lb), BF16)],
        compiler_params=_cparams("parallel", "parallel"),
        name="hyena_conv",
    )(v, gate, bias.reshape(C, 1), kr, ki, kn, wf, wi)


def _hyena(pt, conv_w, conv_b, hy_bias, filt):
    L = pt.shape[2]
    lb = HY_BLK if L >= 2 * HY_BLK else L // 2
    wf, wi = _dft_mats(lb, lb)
    ht = _hyena_filter(L, *filt)
    kr, ki, kn = _filter_blocks(ht, _segdft(ht, wf, lb), lb)
    x1, x2, v = _shortconv(pt, conv_w, conv_b, lb)
    z = _hyconv(v, x1, hy_bias[0], kr, ki, kn, 0, wf, wi)
    return _hyconv(z, x2, hy_bias[1], kr, ki, kn, 1, wf, wi)


def _s5_operators(lam_re, lam_im, log_step, b_re, b_im, c_re, c_im):
    lam = lax.complex(lam_re.astype(F32), lam_im.astype(F32))
    dt = jnp.exp(log_step.astype(F32))[..., None]
    lam_bar = jnp.exp(lam * dt)
    lam_t = jnp.exp(lam * dt * S5_CHUNK)
    b_bar = ((lam_bar - 1.0) / lam)[..., None] * lax.complex(b_re.astype(F32), b_im.astype(F32))
    c_mat = lax.complex(c_re.astype(F32), c_im.astype(F32))
    gh = S5_GROUPS // S5_HALVES
    eye = jnp.eye(gh, dtype=F32)

    def bd_in(t):
        t = t.reshape(2, S5_HALVES, gh, S5_STATE, S5_GROUP)
        return jnp.einsum('dfgpn,gh->dfgnhp', t, eye).reshape(2, S5_HALVES, LANE, S5_PH)

    def bd_out(t):
        t = t.reshape(2, S5_HALVES, gh, S5_GROUP, S5_STATE)
        return jnp.einsum('dfgnp,gh->dfgphn', t, eye).reshape(2, S5_HALVES, S5_PH, LANE)

    bbd = jnp.stack([bd_in(jnp.real(b_bar)), bd_in(jnp.imag(b_bar))], axis=1).astype(BF16)
    cbd = jnp.stack([bd_out(jnp.real(c_mat)), bd_out(-jnp.imag(c_mat))], axis=1).astype(BF16)
    flat = lambda z: jnp.stack([jnp.real(z), jnp.imag(z)], axis=1).reshape(2, 2, S5_P)
    return bbd, cbd, flat(lam_bar), flat(lam_t)


def _s5_kernel(ua_ref, ub_ref, s0_ref, bbd_ref, cbd_ref, lam_ref, lamt_ref, ya_ref, yb_ref, sfin_ref,
               sloc_scr, sinit_scr, bu_scr):
    T = S5_CHUNK
    PH = S5_PH
    M = ua_ref.shape[1] // T
    nt = M // SUBLANE
    rid = lax.broadcasted_iota(jnp.int32, (SUBLANE, PH), 0)
    combos = [(hf, d) for hf in range(S5_HALVES) for d in (0, 1)]
    u_refs = (ua_ref, ub_ref)
    y_refs = (ya_ref, yb_ref)
    re_sl = lambda hf: slice(hf * PH, (hf + 1) * PH)
    im_sl = lambda hf: slice(S5_P + hf * PH, S5_P + (hf + 1) * PH)
    order = lambda d: list(range(T)) if d == 0 else list(range(T - 1, -1, -1))

    def drive(hf, d, s):
        us = u_refs[hf][0, pl.ds(s, M, stride=T), :].astype(BF16)
        return _bdot(us, bbd_ref[d, 0, hf]), _bdot(us, bbd_ref[d, 1, hf])

    def advance(hf, d, sr, si, br, bi):
        lr = lam_ref[d, 0:1, re_sl(hf)]
        li = lam_ref[d, 1:2, re_sl(hf)]
        return lr * sr - li * si + br, lr * si + li * sr + bi

    for hf, d in combos:
        steps = order(d)
        for n, s in enumerate(steps):
            br, bi = drive(hf, d, s)
            bu_scr[s, :, :PH] = br
            bu_scr[s, :, PH:] = bi
            sr, si = (br, bi) if n == 0 else advance(hf, d, sr, si, br, bi)
        sloc_scr[:, :PH] = sr
        sloc_scr[:, PH:] = si

        ar = lamt_ref[d, 0:1, re_sl(hf)]
        ai = lamt_ref[d, 1:2, re_sl(hf)]

        def scan_tile(n, carry, d=d, ar=ar, ai=ai):
            cr, ci = carry
            base = pl.multiple_of((n if d == 0 else nt - 1 - n) * SUBLANE, SUBLANE)
            lr_t = sloc_scr[pl.ds(base, SUBLANE), :PH]
            li_t = sloc_scr[pl.ds(base, SUBLANE), PH:]
            out_r = jnp.zeros((SUBLANE, PH), F32)
            out_i = jnp.zeros((SUBLANE, PH), F32)
            for r in (range(SUBLANE) if d == 0 else range(SUBLANE - 1, -1, -1)):
                out_r = jnp.where(rid == r, cr, out_r)
                out_i = jnp.where(rid == r, ci, out_i)
                cr, ci = (ar * cr - ai * ci + lr_t[r:r + 1], ar * ci + ai * cr + li_t[r:r + 1])
            sinit_scr[pl.ds(base, SUBLANE), :PH] = out_r
            sinit_scr[pl.ds(base, SUBLANE), PH:] = out_i
            return cr, ci

        cr, ci = lax.fori_loop(0, nt, scan_tile, (s0_ref[0, d:d + 1, re_sl(hf)], s0_ref[0, d:d + 1, im_sl(hf)]))
        sfin_ref[0, d:d + 1, re_sl(hf)] = cr
        sfin_ref[0, d:d + 1, im_sl(hf)] = ci

        sr = sinit_scr[:, :PH]
        si = sinit_scr[:, PH:]
        for s in steps:
            sr, si = advance(hf, d, sr, si, bu_scr[s, :, :PH], bu_scr[s, :, PH:])
            ys = (_bdot(sr.astype(BF16), cbd_ref[d, 0, hf]) + _bdot(si.astype(BF16), cbd_ref[d, 1, hf]))
            if d == 0:
                y_refs[hf][0, pl.ds(s, M, stride=T), :] = ys
            else:
                y_refs[hf][0, pl.ds(s, M, stride=T), :] += ys


def _s5_pass(ua, ub, ops, s0):
    bbd, cbd, lam, lam_t = ops
    B, L, _ = ua.shape
    M = L // S5_CHUNK
    full = lambda nd: (lambda b: (0,) * nd)
    tok = pl.BlockSpec((1, L, LANE), lambda b: (b, 0, 0))
    st = pl.BlockSpec((1, 2, 2 * S5_P), lambda b: (b, 0, 0))
    return pl.pallas_call(
        _s5_kernel,
        grid=(B,),
        in_specs=[tok, tok, st,
                  pl.BlockSpec(bbd.shape, full(5)),
                  pl.BlockSpec(cbd.shape, full(5)),
                  pl.BlockSpec((2, 2, S5_P), full(3)),
                  pl.BlockSpec((2, 2, S5_P), full(3))],
        out_specs=[tok, tok, st],
        out_shape=[jax.ShapeDtypeStruct((B, L, LANE), F32),
                   jax.ShapeDtypeStruct((B, L, LANE), F32),
                   jax.ShapeDtypeStruct((B, 2, 2 * S5_P), F32)],
        scratch_shapes=[pltpu.VMEM((M, 2 * S5_PH), F32),
                        pltpu.VMEM((M, 2 * S5_PH), F32),
                        pltpu.VMEM((S5_CHUNK, M, 2 * S5_PH), F32)],
        compiler_params=_cparams("parallel"),
        name="s5",
    )(ua, ub, s0, bbd, cbd, lam, lam_t)


def _gelu_tanh(v):
    return 0.5 * v * (1.0 + jnp.tanh(math.sqrt(2.0 / math.pi) * (v + 0.044715 * v * v * v)))


def _outproj0_kernel(hy_ref, ya_ref, yb_ref, ua_ref, ub_ref, d_ref, gw_ref, gb_ref, wa_ref, wb_ref,
                     x_ref, m_ref, o_ref):
    ys = (jnp.concatenate([ya_ref[0], yb_ref[0]], axis=1)
          + d_ref[...] * jnp.concatenate([ua_ref[0], ub_ref[0]], axis=1))
    glu = _bdot(_gelu_tanh(ys).astype(BF16), gw_ref[...]) + gb_ref[...]
    s5 = glu[:, :S5_D] * _sigmoid(glu[:, S5_D:])
    hy = jnp.concatenate([hy_ref[0, a] for a in range(hy_ref.shape[1])], axis=1)
    y = lax.dot_general(hy, wa_ref[...], _TN, preferred_element_type=F32)
    y = y + _bdot(s5.astype(BF16), wb_ref[...])
    o_ref[0] = x_ref[0] + m_ref[0] * y


def _outproj0(hy, ys, u, d, glu_w, glu_b, w_out, x, m2):
    B, L, D = x.shape
    tl = _pick(L, 512, 256, 128)
    lb = hy.shape[3]
    kb = tl // lb
    full = lambda b, i: (0, 0)
    tok = lambda w: pl.BlockSpec((1, tl, w), lambda b, i: (b, i, 0))
    return pl.pallas_call(
        _outproj0_kernel,
        grid=(B, L // tl),
        in_specs=[pl.BlockSpec((1, kb, HY_D, lb), lambda b, i: (b, i, 0, 0)),
                  tok(LANE), tok(LANE), tok(LANE), tok(LANE),
                  pl.BlockSpec((1, S5_D), full),
                  pl.BlockSpec((S5_D, 2 * S5_D), full),
                  pl.BlockSpec((1, 2 * S5_D), full),
                  pl.BlockSpec((HY_D, D), full),
                  pl.BlockSpec((S5_D, D), full),
                  tok(D),
                  pl.BlockSpec((1, 1, D), lambda b, i: (b, 0, 0))],
        out_specs=tok(D),
        out_shape=jax.ShapeDtypeStruct((B, L, D), F32),
        compiler_params=_cparams("parallel", "parallel"),
        name="outproj0",
    )(hy, ys[0], ys[1], u[0], u[1], d.reshape(1, S5_D), glu_w.astype(BF16), glu_b.reshape(1, -1),
      w_out[:HY_D].astype(BF16), w_out[HY_D:].astype(BF16), x, m2)


def _moe_kernel(x_ref, g_ref, sc_ref, sh_ref, m_ref, wrh_ref, wrl_ref, br_ref, tril_ref,
                wg_ref, wu_ref, wd_ref, fg_ref, o_ref, hs_scr, gs_scr, pt_scr, ys_scr, seg_smem,
                *, ne, final_norm):
    j = pl.program_id(2)
    nj = pl.num_programs(2)
    tl = x_ref.shape[1]
    S = hs_scr.shape[0]

    @pl.when(j == 0)
    def _():
        hn = _rms_mod(x_ref[0], g_ref[...], sc_ref[0], sh_ref[0])
        hi = hn.astype(BF16)
        lo = (hn - hi.astype(F32)).astype(BF16)
        wrh = wrh_ref[...]
        logits = _bdot(hi, wrh) + _bdot(lo, wrh) + _bdot(hi, wrl_ref[...]) + br_ref[...]
        lane = lax.broadcasted_iota(jnp.int32, logits.shape, 1)
        neg = -jnp.inf
        lgm = jnp.where(lane < MOE_GROUPS, logits, neg)
        gmax = jnp.max(lgm, axis=1, keepdims=True)
        p_top = 1.0 / jnp.sum(jnp.exp(lgm - gmax), axis=1, keepdims=True)
        gidx = jnp.min(jnp.where(lgm == gmax, lane, LANE), axis=1, keepdims=True)
        elane = lane - MOE_GROUPS
        in_group = (elane >= gidx * MOE_EPG) & (elane < (gidx + 1) * MOE_EPG)
        lem = jnp.where(in_group, logits, neg)
        v1 = jnp.max(lem, axis=1, keepdims=True)
        i1 = jnp.min(jnp.where(lem == v1, lane, LANE), axis=1, keepdims=True)
        lem2 = jnp.where(lane == i1, neg, lem)
        v2 = jnp.max(lem2, axis=1, keepdims=True)
        i2 = jnp.min(jnp.where(lem2 == v2, lane, LANE), axis=1, keepdims=True)
        e2 = jnp.exp(v2 - v1)
        w1 = p_top / (1.0 + e2)
        gate = jnp.where(lane == i1, w1, jnp.where(lane == i2, w1 * e2, 0.0))
        onehot = (lane == gidx).astype(F32)
        rank = _bdot(tril_ref[...], onehot.astype(BF16))
        off = jnp.int32(0)
        offv = jnp.zeros((1, LANE), F32)
        for g in range(MOE_GROUPS):
            n_g = jnp.sum(onehot[:, g:g + 1]).astype(jnp.int32)
            seg_smem[g] = off
            seg_smem[MOE_GROUPS + g] = n_g
            offv = jnp.where(lane[0:1, :] == g, off.astype(F32), offv)
            off = off + ((n_g + MOE_ALIGN - 1) // MOE_ALIGN) * MOE_ALIGN
        dlane = onehot * (offv + rank)
        dest = jnp.sum(dlane, axis=1, keepdims=True).astype(jnp.int32)
        slot = lax.broadcasted_iota(jnp.int32, (tl, S), 1)
        pt_scr[...] = (slot == dest).astype(BF16)
        d_hi = jnp.floor(dlane * (1.0 / 256.0))
        ones = jnp.ones((SUBLANE, LANE), BF16)
        drow = (256.0 * lax.dot_general(ones, d_hi.astype(BF16), _NT, preferred_element_type=F32)
                + lax.dot_general(ones, (dlane - 256.0 * d_hi).astype(BF16), _NT, preferred_element_type=F32))
        srow = lax.broadcasted_iota(jnp.int32, (S, tl), 0)
        p = (srow == drow[0:1, :].astype(jnp.int32)).astype(BF16)
        hs_scr[...] = _bdot(p, hi).astype(BF16)
        ghi = gate.astype(BF16)
        glo = (gate - ghi.astype(F32)).astype(BF16)
        gs_scr[...] = _bdot(p, ghi) + _bdot(p, glo)
        ys_scr[...] = jnp.zeros_like(ys_scr)

    grp = j // (MOE_EPG // ne)
    start = pl.multiple_of(seg_smem[grp], MOE_ALIGN)
    units = (seg_smem[MOE_GROUPS + grp] + MOE_UNIT - 1) // MOE_UNIT
    lane1 = lax.broadcasted_iota(jnp.int32, (1, LANE), 1)

    def block(r0, rows):
        hs = hs_scr[pl.ds(r0, rows), :]
        gsb = gs_scr[pl.ds(r0, rows), :]
        acc = None
        for e in range(ne):
            a = _bdot(hs, wg_ref[e])
            u = _bdot(hs, wu_ref[e])
            gcol = jnp.sum(jnp.where(lane1 == MOE_GROUPS + j * ne + e, gsb, 0.0), axis=1, keepdims=True)
            y = _bdot((a * _sigmoid(a) * u * gcol).astype(BF16), wd_ref[e])
            acc = y if acc is None else acc + y
        ys_scr[pl.ds(r0, rows), :] += acc

    @pl.when(units <= 2)
    def _():
        block(start, 2 * MOE_UNIT)

    @pl.when(units == 3)
    def _():
        block(start, 3 * MOE_UNIT)

    @pl.when(units >= 4)
    def _():
        def body(i, carry):
            block(pl.multiple_of(start + i * (2 * MOE_UNIT), MOE_ALIGN), 2 * MOE_UNIT)
            return carry

        lax.fori_loop(0, (units + 1) // 2, body, 0)

    @pl.when(j == nj - 1)
    def _():
        moe = _bdot(pt_scr[...], ys_scr[...].astype(BF16))
        out = x_ref[0] + m_ref[0] * moe
        if final_norm:
            ms = jnp.mean(out * out, axis=-1, keepdims=True)
            out = out * lax.rsqrt(ms + EPS) * fg_ref[...]
        o_ref[0] = out


def _moe(x, g2, sc, sh, m5, wrh, wrl, br, wg, wu, wd, final_g=None):
    B, L, D = x.shape
    ne = 4
    tl = _pick(L, 1024, 512, 256, 128)
    S = tl + 3 * MOE_UNIT
    final_norm = final_g is not None
    fg = (final_g if final_norm else jnp.ones((D,), F32)).reshape(1, D)
    idx = jnp.arange(tl, dtype=jnp.int32)
    tril = (idx[None, :] < idx[:, None]).astype(BF16)
    full = lambda b, i, j: (0, 0)
    tok = pl.BlockSpec((1, tl, D), lambda b, i, j: (b, i, 0))
    vec = pl.BlockSpec((1, 1, D), lambda b, i, j: (b, 0, 0))
    wspec = lambda k, n: pl.BlockSpec((ne, k, n), lambda b, i, j: (j, 0, 0))
    return pl.pallas_call(
        functools.partial(_moe_kernel, ne=ne, final_norm=final_norm),
        grid=(B, L // tl, MOE_EXPERTS // ne),
        in_specs=[tok, pl.BlockSpec((1, D), full), vec, vec, vec,
                  pl.BlockSpec((D, LANE), full), pl.BlockSpec((D, LANE), full), pl.BlockSpec((1, LANE), full),
                  pl.BlockSpec((tl, tl), full, pipeline_mode=pl.Buffered(1)),
                  wspec(D, MOE_HIDDEN), wspec(D, MOE_HIDDEN), wspec(MOE_HIDDEN, D),
                  pl.BlockSpec((1, D), full)],
        out_specs=tok,
        out_shape=jax.ShapeDtypeStruct((B, L, D), F32),
        scratch_shapes=[pltpu.VMEM((S, D), BF16),
                        pltpu.VMEM((S, LANE), F32),
                        pltpu.VMEM((tl, S), BF16),
                        pltpu.VMEM((S, D), F32),
                        pltpu.SMEM((2 * MOE_GROUPS,), jnp.int32)],
        compiler_params=_cparams("parallel", "parallel", "arbitrary"),
        name="moe",
    )(x, g2, sc, sh, m5, wrh, wrl, br, tril, wg, wu, wd, fg)


def _moe_params(wg, bg, we, be, w_gate, w_up, w_down):
    D = wg.shape[0]
    pad = LANE - MOE_GROUPS - MOE_EXPERTS
    wr = jnp.pad(jnp.concatenate([wg, we], axis=1), ((0, 0), (0, pad)))
    br = jnp.pad(jnp.concatenate([bg, be]), (0, pad)).reshape(1, LANE)
    wrh = wr.astype(BF16)
    wrl = (wr - wrh.astype(F32)).astype(BF16)
    return (wrh, wrl, br,
            w_gate.reshape(MOE_EXPERTS, D, MOE_HIDDEN).astype(BF16),
            w_up.reshape(MOE_EXPERTS, D, MOE_HIDDEN).astype(BF16),
            w_down.reshape(MOE_EXPERTS, MOE_HIDDEN, D).astype(BF16))


def _inproj1_kernel(x_ref, g_ref, sc_ref, sh_ref, w_ref, o_ref, h_scr):
    @pl.when(pl.program_id(2) == 0)
    def _():
        h_scr[...] = _rms_mod(x_ref[0], g_ref[...], sc_ref[0], sh_ref[0]).astype(BF16)

    o_ref[0] = _bdot(h_scr[...], w_ref[...]).astype(o_ref.dtype)


def _inproj1(x, g, sc, sh, w):
    B, L, D = x.shape
    N = w.shape[1]
    tl = _pick(L, 1024, 512, 256, 128)
    tn = _pick(N, 2560, 1280, 1024, 512)
    return pl.pallas_call(
        _inproj1_kernel,
        grid=(B, L // tl, N // tn),
        in_specs=[pl.BlockSpec((1, tl, D), lambda b, i, j: (b, i, 0)),
                  pl.BlockSpec((1, D), lambda b, i, j: (0, 0)),
                  pl.BlockSpec((1, 1, D), lambda b, i, j: (b, 0, 0)),
                  pl.BlockSpec((1, 1, D), lambda b, i, j: (b, 0, 0)),
                  pl.BlockSpec((D, tn), lambda b, i, j: (0, j))],
        out_specs=pl.BlockSpec((1, tl, tn), lambda b, i, j: (b, i, j)),
        out_shape=jax.ShapeDtypeStruct((B, L, N), BF16),
        scratch_shapes=[pltpu.VMEM((tl, D), BF16)],
        compiler_params=_cparams("parallel", "parallel", "arbitrary"),
        name="inproj1",
    )(x, g, sc, sh, w)


def _gla_masks(C):
    t = np.arange(C)[:, None]
    s = np.arange(C)[None, :]
    fwd = []
    hs = 1
    while hs < C:
        fwd.append(((t // (2 * hs)) == (s // (2 * hs))) & ((t & hs) != 0) & ((s & hs) == 0))
        hs *= 2
    fwd = np.stack(fwd).astype(np.float32)
    return jnp.asarray(np.stack([fwd, fwd.transpose(0, 2, 1)])), jnp.asarray(np.eye(C, dtype=np.float32))


def _gla_direction(pq, v, z, lb, masks, eye, st_ref, rev):
    C = pq.shape[0]
    heads = [slice(h * LANE, (h + 1) * LANE) for h in range(C_HEADS)]
    q = pq * _sigmoid(pq)
    f = lb + (1.0 - lb) * _sigmoid(z)
    k = 1.0 - f
    g = jnp.log2(f)
    row = lax.broadcasted_iota(jnp.int32, (C, C_D), 0)
    qb = q.astype(BF16)
    kb = k.astype(BF16)
    att = [eye * lax.dot_general(qb[:, hd], kb[:, hd], _NT, preferred_element_type=F32) for hd in heads]
    pinc = g
    tot = g
    hs = 1
    lvl = 0
    while hs < C:
        aq = pinc
        ak = tot - pinc
        qe = (q * jnp.exp2(aq)).astype(BF16)
        ke = (k * jnp.exp2(ak)).astype(BF16)
        for h, hd in enumerate(heads):
            blk = lax.dot_general(qe[:, hd], ke[:, hd], _NT, preferred_element_type=F32)
            att[h] = jnp.where(masks[lvl], blk, att[h])
        if hs < SUBLANE:
            odd = (row & hs) != 0
            t3 = tot.reshape(C // SUBLANE, SUBLANE, C_D)
            tprev = pltpu.roll(t3, hs, 1).reshape(C, C_D)
            tnext = pltpu.roll(t3, SUBLANE - hs, 1).reshape(C, C_D)
            pinc = pinc + (jnp.where(odd, 0.0, tnext) if rev else jnp.where(odd, tprev, 0.0))
            tot = tot + jnp.where(odd, tprev, tnext)
        else:
            w = hs // SUBLANE
            tile = lambda a, i: a[i * SUBLANE:(i + 1) * SUBLANE]
            new_p, new_t = [], []
            for i in range(C // SUBLANE):
                sib = i - w if (i // w) % 2 == 1 else i + w
                grows = ((i // w) % 2 == 1) != rev
                new_p.append(tile(pinc, i) + tile(tot, sib) if grows else tile(pinc, i))
                new_t.append(tile(tot, i) + tile(tot, sib))
            pinc = jnp.concatenate(new_p, axis=0)
            tot = jnp.concatenate(new_t, axis=0)
        hs *= 2
        lvl += 1
    q_dec = pinc
    k_dec = tot - pinc
    vb = v.astype(BF16)
    qd = (q * jnp.exp2(q_dec)).astype(BF16)
    kd = (k * jnp.exp2(k_dec)).astype(BF16)
    keep = jnp.exp2(tot[0:1, :])
    outs = []
    for h, hd in enumerate(heads):
        st = st_ref[h]
        o = _bdot(att[h].astype(BF16), vb[:, hd])
        o = o + lax.dot_general(qd[:, hd], st.astype(BF16), _NT, preferred_element_type=F32)
        upd = lax.dot_general(vb[:, hd], kd[:, hd], _TN, preferred_element_type=F32)
        st_ref[h] = st * keep[:, hd] + upd
        outs.append(o)
    return jnp.concatenate(outs, axis=1)


def _gla_kernel(qf_ref, if_ref, zf_ref, qb_ref, ib_ref, zb_ref, lb_ref, mk_ref, eye_ref, s0_ref,
                of_ref, ob_ref, sfin_ref, st_scr):
    c = pl.program_id(1)

    @pl.when(c == 0)
    def _():
        st_scr[...] = s0_ref[0]

    eye = eye_ref[...]
    nlev = mk_ref.shape[1]
    for d, (q_ref, i_ref, z_ref, o_ref) in enumerate(((qf_ref, if_ref, zf_ref, of_ref),
                                                      (qb_ref, ib_ref, zb_ref, ob_ref))):
        masks = [mk_ref[d, lv] > 0.5 for lv in range(nlev)]
        o = _gla_direction(q_ref[0].astype(F32), i_ref[0].astype(F32), z_ref[0].astype(F32),
                           lb_ref[d:d + 1, :], masks, eye, st_scr.at[d], rev=(d == 1))
        o_ref[0] = o.astype(o_ref.dtype)

    @pl.when(c == pl.num_programs(1) - 1)
    def _():
        sfin_ref[0] = st_scr[...]


def _gla(p, lb, s0):
    B, L, _ = p.shape
    C = GLA_CHUNK
    nch = L // C
    masks, eye = _gla_masks(C)
    fcol = lambda k: pl.BlockSpec((1, C, C_D), lambda b, c: (b, c, k))
    bcol = lambda k: pl.BlockSpec((1, C, C_D), lambda b, c: (b, nch - 1 - c, k))
    st_spec = pl.BlockSpec((1, 2, C_HEADS, LANE, LANE), lambda b, c: (b, 0, 0, 0, 0))
    return pl.pallas_call(
        _gla_kernel,
        grid=(B, nch),
        in_specs=[fcol(0), fcol(2), fcol(3), bcol(0), bcol(2), bcol(4),
                  pl.BlockSpec((2, C_D), lambda b, c: (0, 0)),
                  pl.BlockSpec(masks.shape, lambda b, c: (0, 0, 0, 0)),
                  pl.BlockSpec((C, C), lambda b, c: (0, 0)),
                  st_spec],
        out_specs=[fcol(0), bcol(0), st_spec],
        out_shape=[jax.ShapeDtypeStruct((B, L, C_D), BF16),
                   jax.ShapeDtypeStruct((B, L, C_D), BF16),
                   jax.ShapeDtypeStruct((B, 2, C_HEADS, LANE, LANE), F32)],
        scratch_shapes=[pltpu.VMEM((2, C_HEADS, LANE, LANE), F32)],
        compiler_params=_cparams("parallel", "arbitrary"),
        name="gla",
    )(p, p, p, p, p, p, lb, masks, eye, s0)


def _outproj1_kernel(of_ref, ob_ref, g_ref, ng_ref, w_ref, x_ref, m_ref, o_ref):
    o = of_ref[0].astype(F32) + ob_ref[0].astype(F32)
    parts = []
    for h in range(C_HEADS):
        oh = o[:, h * LANE:(h + 1) * LANE]
        ms = jnp.mean(oh * oh, axis=-1, keepdims=True)
        parts.append(oh * lax.rsqrt(ms + EPS))
    on = jnp.concatenate(parts, axis=1) * ng_ref[...] * _sigmoid(g_ref[0].astype(F32))
    y = _bdot(on.astype(BF16), w_ref[...])
    o_ref[0] = x_ref[0] + m_ref[0] * y


def _outproj1(o_f, o_b, p, norm_g, w_out, x, m2):
    B, L, D = x.shape
    tl = _pick(L, 512, 256, 128)
    full = lambda b, i: (0, 0)
    tok = pl.BlockSpec((1, tl, C_D), lambda b, i: (b, i, 0))
    return pl.pallas_call(
        _outproj1_kernel,
        grid=(B, L // tl),
        in_specs=[tok, tok,
                  pl.BlockSpec((1, tl, C_D), lambda b, i: (b, i, 1)),
                  pl.BlockSpec((1, C_D), full),
                  pl.BlockSpec((C_D, D), full),
                  pl.BlockSpec((1, tl, D), lambda b, i: (b, i, 0)),
                  pl.BlockSpec((1, 1, D), lambda b, i: (b, 0, 0))],
        out_specs=pl.BlockSpec((1, tl, D), lambda b, i: (b, i, 0)),
        out_shape=jax.ShapeDtypeStruct((B, L, D), F32),
        compiler_params=_cparams("parallel", "parallel"),
        name="outproj1",
    )(o_f, o_b, p, norm_g.reshape(1, C_D), w_out.astype(BF16), x, m2)


def _mods(cmat, w, b, nb):
    R = cmat.shape[0]
    pad = (-R) % SUBLANE
    m = _modvec(jnp.pad(cmat, ((0, pad), (0, 0))), w, b)[:R]
    m = jnp.broadcast_to(m, (nb, m.shape[1])) if R == 1 else m
    return [m[:, None, k * D_MODEL:(k + 1) * D_MODEL] for k in range(6)]


def kernel(x, c, ctx, c_ctx, mod_w, mod_b, norm1_g, norm2_g, final_g,
           ab_w_in, ab_w_out, hy_conv_w, hy_conv_b, hy_fw1, hy_fb1, hy_ff1,
           hy_fw2, hy_fb2, hy_ff2, hy_fw3, hy_bias,
           s5_lam_re, s5_lam_im, s5_log_step, s5_b_re, s5_b_im, s5_c_re, s5_c_im,
           s5_d, s5_glu_w, s5_glu_b,
           c_w_in, c_w_out, c_lower_bounds, c_norm_g,
           moe_wg, moe_bg, moe_we, moe_be, moe_w_gate, moe_w_up, moe_w_down):
    B, L, D = x.shape
    row = lambda t: t.reshape(1, -1)

    m = _mods(c, mod_w[0], mod_b[0], B)
    mc = _mods(c_ctx[None, :], mod_w[0], mod_b[0], B)
    w_in = ab_w_in[0]
    wt_hy = w_in[:, :3 * HY_D].T.astype(BF16)
    w_s5 = w_in[:, 3 * HY_D:].astype(BF16)
    g1 = row(norm1_g[0])
    pt_c, *u_c = _inproj0(ctx, g1, 1.0 + mc[1], mc[0], wt_hy, w_s5)
    pt_l, *u_l = _inproj0(x, g1, 1.0 + m[1], m[0], wt_hy, w_s5)

    ops = _s5_operators(s5_lam_re[0], s5_lam_im[0], s5_log_step[0], s5_b_re[0], s5_b_im[0],
                        s5_c_re[0], s5_c_im[0])
    *ys_c, s_fin = _s5_pass(*u_c, ops, jnp.zeros((B, 2, 2 * S5_P), F32))
    *ys_l, _ = _s5_pass(*u_l, ops, s_fin)

    filt = (hy_fw1[0], hy_fb1[0], hy_ff1[0], hy_fw2[0], hy_fb2[0], hy_ff2[0], hy_fw3[0])
    hy_c = _hyena(pt_c, hy_conv_w[0], hy_conv_b[0], hy_bias[0], filt)
    hy_l = _hyena(pt_l, hy_conv_w[0], hy_conv_b[0], hy_bias[0], filt)

    x = _outproj0(hy_l, ys_l, u_l, s5_d[0], s5_glu_w[0], s5_glu_b[0], ab_w_out[0], x, m[2])
    ctx = _outproj0(hy_c, ys_c, u_c, s5_d[0], s5_glu_w[0], s5_glu_b[0], ab_w_out[0], ctx, mc[2])

    mp = _moe_params(moe_wg[0], moe_bg[0], moe_we[0], moe_be[0],
                     moe_w_gate[0], moe_w_up[0], moe_w_down[0])
    g2 = row(norm2_g[0])
    x = _moe(x, g2, 1.0 + m[4], m[3], m[5], *mp)
    ctx = _moe(ctx.reshape(1, -1, D), g2, 1.0 + mc[4][:1], mc[3][:1], mc[5][:1], *mp).reshape(ctx.shape)

    m = _mods(c, mod_w[1], mod_b[1], B)
    mc = _mods(c_ctx[None, :], mod_w[1], mod_b[1], B)
    sm = jax.nn.softmax(c_lower_bounds.astype(F32), axis=1)
    lower = (jnp.cumsum(sm, axis=1) - sm[:, :1])[:, 1]
    g1 = row(norm1_g[1])
    w1 = c_w_in[0].astype(BF16)
    p_c = _inproj1(ctx, g1, 1.0 + mc[1], mc[0], w1)
    p_l = _inproj1(x, g1, 1.0 + m[1], m[0], w1)
    zeros = jnp.zeros((B, 2, C_HEADS, C_HEAD_DIM, C_HEAD_DIM), F32)
    _, _, s_ctx = _gla(p_c, lower, zeros)
    o_f, o_b, _ = _gla(p_l, lower, s_ctx)
    x = _outproj1(o_f, o_b, p_l, c_norm_g[0], c_w_out[0], x, m[2])

    mp = _moe_params(moe_wg[1], moe_bg[1], moe_we[1], moe_be[1],
                     moe_w_gate[1], moe_w_up[1], moe_w_down[1])
    return _moe(x, row(norm2_g[1]), 1.0 + m[4], m[3], m[5], *mp, final_g=final_g)
```

```python
import functools
import math

import numpy as np
import jax
import jax.numpy as jnp
from jax import lax
from jax.experimental import pallas as pl
from jax.experimental.pallas import tpu as pltpu

F32 = jnp.float32
BF16 = jnp.bfloat16
EPS = 1e-6
HIGHEST = lax.Precision.HIGHEST

D_MODEL = 1024
HY_D = 768
HY_ORDER = 2
HY_EMB = 33
HY_BANDS = (HY_EMB - 1) // 2
HY_DECAY_SHORT = 0.3
HY_DECAY_LONG = 1.5
HY_TARGET = 1e-2
S5_D = 256
S5_GROUP = 16
S5_GROUPS = S5_D // S5_GROUP
S5_STATE = 64
S5_P = S5_GROUPS * S5_STATE
S5_HALVES = S5_D // 128
S5_PH = S5_P // S5_HALVES
C_HEADS = 8
C_HEAD_DIM = 128
C_D = C_HEADS * C_HEAD_DIM
MOE_GROUPS = 4
MOE_EPG = 8
MOE_EXPERTS = MOE_GROUPS * MOE_EPG
MOE_HIDDEN = 256
MOE_UNIT = 128
MOE_ALIGN = 16

LANE = 128
SUBLANE = 8
HY_BLK = 512
S5_CHUNK = 16
GLA_CHUNK = 128
VMEM_LIMIT = 60 * 1024 * 1024


def _cparams(*sem):
    return pltpu.CompilerParams(dimension_semantics=sem, vmem_limit_bytes=VMEM_LIMIT)


def _pick(n, *cands):
    for c in cands:
        if n % c == 0:
            return c
    return n


def _rms_mod(xv, g, sc, sh):
    ms = jnp.mean(xv * xv, axis=-1, keepdims=True)
    return xv * lax.rsqrt(ms + EPS) * g * sc + sh


def _sigmoid(v):
    return 1.0 / (1.0 + jnp.exp(-v))


def _bdot(a, b):
    return jnp.dot(a, b, preferred_element_type=F32)


_NT = (((1,), (1,)), ((), ()))
_TN = (((0,), (0,)), ((), ()))


def _mm_kernel(a_ref, b_ref, o_ref):
    o_ref[...] = _bdot(a_ref[...].astype(BF16), b_ref[...]).astype(o_ref.dtype)


def _mm(a, b, out_dtype=F32, name="mm"):
    M, K = a.shape
    N = b.shape[1]
    tm = _pick(M, 768, 512, 256, 128)
    tn = _pick(N, 512, 256, 128)
    return pl.pallas_call(
        _mm_kernel,
        grid=(M // tm, N // tn),
        in_specs=[pl.BlockSpec((tm, K), lambda i, j: (i, 0)),
                  pl.BlockSpec((K, tn), lambda i, j: (0, j))],
        out_specs=pl.BlockSpec((tm, tn), lambda i, j: (i, j)),
        out_shape=jax.ShapeDtypeStruct((M, N), out_dtype),
        compiler_params=_cparams("parallel", "parallel"),
        name=name,
    )(a, b.astype(BF16))


def _modvec_kernel(c_ref, w_ref, b_ref, o_ref):
    cv = c_ref[...]
    sc = cv * _sigmoid(cv)
    o_ref[...] = jnp.dot(sc, w_ref[...], preferred_element_type=F32, precision=HIGHEST) + b_ref[...]


def _modvec(cvec, w, b):
    R, D = cvec.shape
    N = w.shape[1]
    tn = _pick(N, 512, 256, 128)
    return pl.pallas_call(
        _modvec_kernel,
        grid=(N // tn,),
        in_specs=[pl.BlockSpec((R, D), lambda j: (0, 0)),
                  pl.BlockSpec((D, tn), lambda j: (0, j)),
                  pl.BlockSpec((1, tn), lambda j: (0, j))],
        out_specs=pl.BlockSpec((R, tn), lambda j: (0, j)),
        out_shape=jax.ShapeDtypeStruct((R, N), F32),
        compiler_params=_cparams("parallel"),
        name="modvec",
    )(cvec, w, b.reshape(1, N))


def _inproj0_kernel(x_ref, g_ref, sc_ref, sh_ref, wt_ref, w2_ref, pt_ref, ua_ref, ub_ref):
    h = _rms_mod(x_ref[0], g_ref[...], sc_ref[0], sh_ref[0]).astype(BF16)
    pt_ref[0] = lax.dot_general(wt_ref[...], h, _NT, preferred_element_type=F32).astype(BF16)
    u = _bdot(h, w2_ref[...])
    ua_ref[0] = u[:, :LANE]
    ub_ref[0] = u[:, LANE:]


def _inproj0(x, g, sc, sh, wt_hy, w_s5):
    B, L, D = x.shape
    C = wt_hy.shape[0]
    tl = _pick(L, 512, 256, 128)
    return pl.pallas_call(
        _inproj0_kernel,
        grid=(B, L // tl),
        in_specs=[pl.BlockSpec((1, tl, D), lambda b, i: (b, i, 0)),
                  pl.BlockSpec((1, D), lambda b, i: (0, 0)),
                  pl.BlockSpec((1, 1, D), lambda b, i: (b, 0, 0)),
                  pl.BlockSpec((1, 1, D), lambda b, i: (b, 0, 0)),
                  pl.BlockSpec((C, D), lambda b, i: (0, 0)),
                  pl.BlockSpec((D, S5_D), lambda b, i: (0, 0))],
        out_specs=[pl.BlockSpec((1, C, tl), lambda b, i: (b, 0, i)),
                   pl.BlockSpec((1, tl, LANE), lambda b, i: (b, i, 0)),
                   pl.BlockSpec((1, tl, LANE), lambda b, i: (b, i, 0))],
        out_shape=[jax.ShapeDtypeStruct((B, C, L), BF16),
                   jax.ShapeDtypeStruct((B, L, LANE), F32),
                   jax.ShapeDtypeStruct((B, L, LANE), F32)],
        compiler_params=_cparams("parallel", "parallel"),
        name="inproj0",
    )(x, g, sc, sh, wt_hy, w_s5)


def _shortconv_kernel(p1_ref, p2_ref, p3_ref, w_ref, b_ref, x1_ref, x2_ref, v_ref):
    L = p1_ref.shape[2]
    nb, lb = x1_ref.shape[1], x1_ref.shape[3]
    lane = lax.broadcasted_iota(jnp.int32, (1, L), 1)
    first = lane == 0
    last = lane == L - 1
    for k, (p_ref, o_ref) in enumerate(((p1_ref, x1_ref), (p2_ref, x2_ref), (p3_ref, v_ref))):
        u = p_ref[0].astype(F32)
        w = w_ref[k]
        prev = jnp.where(first, 0.0, pltpu.roll(u, 1, 1))
        nxt = jnp.where(last, 0.0, pltpu.roll(u, L - 1, 1))
        y = (w[:, 0:1] * prev + w[:, 1:2] * u + w[:, 2:3] * nxt + b_ref[k]).astype(BF16)
        for j in range(nb):
            o_ref[0, j] = y[:, j * lb:(j + 1) * lb]


def _shortconv(pt, conv_w, conv_b, lb):
    B, C3, L = pt.shape
    nb = L // lb
    tc = 128
    nc = HY_D // tc
    w = conv_w.T.reshape(3, HY_D, 3)
    b = conv_b.reshape(3, HY_D, 1)
    specs = [pl.BlockSpec((1, tc, L), (lambda b_, i, k=k: (b_, k * nc + i, 0))) for k in range(3)]
    o_spec = pl.BlockSpec((1, nb, tc, lb), lambda b_, i: (b_, 0, i, 0))
    return pl.pallas_call(
        _shortconv_kernel,
        grid=(B, nc),
        in_specs=specs + [pl.BlockSpec((3, tc, 3), lambda b_, i: (0, i, 0)),
                          pl.BlockSpec((3, tc, 1), lambda b_, i: (0, i, 0))],
        out_specs=[o_spec, o_spec, o_spec],
        out_shape=[jax.ShapeDtypeStruct((B, nb, HY_D, lb), BF16)] * 3,
        compiler_params=_cparams("parallel", "parallel"),
        name="shortconv",
    )(pt, pt, pt, w, b)


def _filter_kernel(z_ref, w1_ref, b1_ref, f1_ref, w2_ref, b2_ref, f2_ref, w3_ref, dec_ref, o_ref):
    dot = functools.partial(jnp.dot, preferred_element_type=F32, precision=HIGHEST)
    z = z_ref[...]
    hdn = jnp.sin(f1_ref[...] * (dot(w1_ref[...], z) + b1_ref[...]))
    hdn = jnp.sin(f2_ref[...] * (dot(w2_ref[...], hdn) + b2_ref[...]))
    t = z[0:1, :]
    o_ref[...] = dot(w3_ref[...], hdn) * jnp.exp(-dec_ref[...] * t)


def _hyena_filter(L, fw1, fb1, ff1, fw2, fb2, ff2, fw3):
    pos = jnp.arange(L, dtype=F32)
    t = pos / max(L - 1, 1)
    w = 2.0 * math.pi * pos / L
    bands = jnp.linspace(1e-4, HY_BANDS - 1, HY_BANDS, dtype=F32)
    ang = bands[:, None] * w[None, :]
    z = jnp.concatenate([t[None, :], jnp.cos(ang), -jnp.sin(ang)], axis=0)
    z = jnp.pad(z, ((0, LANE - HY_EMB), (0, 0)))
    w1 = jnp.pad(fw1, ((0, LANE - HY_EMB), (0, 0))).T
    deltas = jnp.abs(jnp.linspace(math.log(HY_TARGET) / HY_DECAY_LONG,
                                  math.log(HY_TARGET) / HY_DECAY_SHORT, HY_D, dtype=F32))
    dec = jnp.tile(deltas, HY_ORDER * 2).reshape(-1, 1)
    nf = fw1.shape[1]
    No = fw3.shape[1]
    tl = _pick(L, 512, 256)
    full = lambda i: (0, 0)
    col = lambda v: v.reshape(nf, 1)
    return pl.pallas_call(
        _filter_kernel,
        grid=(L // tl,),
        in_specs=[pl.BlockSpec((LANE, tl), lambda i: (0, i)),
                  pl.BlockSpec((nf, LANE), full), pl.BlockSpec((nf, 1), full), pl.BlockSpec((nf, 1), full),
                  pl.BlockSpec((nf, nf), full), pl.BlockSpec((nf, 1), full), pl.BlockSpec((nf, 1), full),
                  pl.BlockSpec((No, nf), full), pl.BlockSpec((No, 1), full)],
        out_specs=pl.BlockSpec((No, tl), lambda i: (0, i)),
        out_shape=jax.ShapeDtypeStruct((No, L), F32),
        compiler_params=_cparams("parallel"),
        name="hyena_filter",
    )(z, w1, col(fb1), col(ff1), fw2.T, col(fb2), col(ff2), fw3.T, dec)


def _dft_mats(L, blk):
    N = 2 * L
    W = 64
    j = jnp.arange(2 * L, dtype=jnp.int32)
    kk = (j // (2 * blk)) * blk + j % blk
    is_im = (j // blk) % 2 == 1
    nyq = is_im & (kk == 0)
    hi = jnp.arange(L // W, dtype=jnp.int32) * W
    lo = jnp.arange(W, dtype=jnp.int32)
    alt = (1 - 2 * (lo % 2)).astype(F32)
    th = 2.0 * math.pi / N
    ah = ((hi[:, None] * kk[None, :]) % N).astype(F32) * th
    al = ((lo[:, None] * kk[None, :]) % N).astype(F32) * th
    ch, sh, cl, sl = jnp.cos(ah), jnp.sin(ah), jnp.cos(al), jnp.sin(al)
    c = ch[:, None, :] * cl[None] - sh[:, None, :] * sl[None]
    s = sh[:, None, :] * cl[None] + ch[:, None, :] * sl[None]
    wf = jnp.where(nyq[None, None, :], alt[None, :, None], jnp.where(is_im[None, None, :], -s, c))
    ct = ch.T[:, :, None] * cl.T[:, None, :] - sh.T[:, :, None] * sl.T[:, None, :]
    st = sh.T[:, :, None] * cl.T[:, None, :] + ch.T[:, :, None] * sl.T[:, None, :]
    scale = jnp.where(kk == 0, 1.0 / N, 2.0 / N).astype(F32)[:, None, None]
    wi = jnp.where(nyq[:, None, None], alt[None, None, :], jnp.where(is_im[:, None, None], -st, ct)) * scale
    return wf.reshape(L, 2 * L).astype(BF16), wi.reshape(2 * L, L).astype(BF16)


def _segdft_kernel(a_ref, w_ref, o_ref):
    o_ref[...] = _bdot(a_ref[...].astype(BF16), w_ref[...])


def _segdft(ht, wf, lb):
    R, L = ht.shape
    tm = 256
    return pl.pallas_call(
        _segdft_kernel,
        grid=(R // tm, L // lb),
        in_specs=[pl.BlockSpec((tm, lb), lambda i, m: (i, m)),
                  pl.BlockSpec((lb, 2 * lb), lambda i, m: (0, 0))],
        out_specs=pl.BlockSpec((tm, 2 * lb), lambda i, m: (i, m)),
        out_shape=jax.ShapeDtypeStruct((R, 2 * L), F32),
        compiler_params=_cparams("parallel", "parallel"),
        name="filter_segdft",
    )(ht, wf)


def _filter_blocks_kernel(hf_ref, hb_ref, af_ref, ab_ref, kr_ref, ki_ref, kn_ref, *, nb, lb):
    hf = hf_ref[...]
    hb = hb_ref[...]
    lag = lax.broadcasted_iota(jnp.int32, (1, hf.shape[1]), 1)
    hbz = jnp.where(lag == 0, 0.0, hb)
    nrm = lax.rsqrt(jnp.sum(hf * hf, axis=1, keepdims=True) + jnp.sum(hbz * hbz, axis=1, keepdims=True))
    k = lax.broadcasted_iota(jnp.int32, (1, lb), 1)
    sgn = jnp.where(k % 2 == 0, 1.0, -1.0)
    bin0 = k == 0

    def seg(ref, m):
        re = ref[:, m * 2 * lb:m * 2 * lb + lb]
        imp = ref[:, m * 2 * lb + lb:(m + 1) * 2 * lb]
        return re, jnp.where(bin0, 0.0, imp), imp[:, 0:1]

    def emit(idx, re, im, nyq):
        kr_ref[0, idx] = re * nrm
        ki_ref[0, idx] = im * nrm
        kn_ref[0, idx] = nyq * nrm

    fr, fi, fn = seg(af_ref, 0)
    br, bi, bn = seg(ab_ref, 0)
    b0 = hb[:, 0:1]
    emit(nb - 1, fr + br - b0, fi - bi, fn + bn - b0)
    for d in range(1, nb):
        for ref, taps, conj, idx in ((af_ref, hf, 1.0, nb - 1 + d), (ab_ref, hb, -1.0, nb - 1 - d)):
            r1, i1, n1 = seg(ref, d)
            r0, i0, n0 = seg(ref, d - 1)
            e0 = taps[:, (d - 1) * lb:(d - 1) * lb + 1]
            emit(idx, r1 + sgn * (r0 - e0), conj * (i1 + sgn * i0), n1 + n0 - e0)


def _filter_blocks(ht, seg, lb):
    R, L = ht.shape
    nb = L // lb
    nlag = 2 * nb - 1
    tm = 128
    nc = HY_D // tm
    fwd = lambda i: ((i // nc) * 2 * nc + i % nc, 0)
    bwd = lambda i: ((i // nc) * 2 * nc + nc + i % nc, 0)
    kspec = pl.BlockSpec((1, nlag, tm, lb), lambda i: (i // nc, 0, i % nc, 0))
    return pl.pallas_call(
        functools.partial(_filter_blocks_kernel, nb=nb, lb=lb),
        grid=(HY_ORDER * nc,),
        in_specs=[pl.BlockSpec((tm, L), fwd), pl.BlockSpec((tm, L), bwd),
                  pl.BlockSpec((tm, 2 * L), fwd), pl.BlockSpec((tm, 2 * L), bwd)],
        out_specs=[kspec, kspec, pl.BlockSpec((1, nlag, tm, 1), lambda i: (i // nc, 0, i % nc, 0))],
        out_shape=[jax.ShapeDtypeStruct((HY_ORDER, nlag, HY_D, lb), F32),
                   jax.ShapeDtypeStruct((HY_ORDER, nlag, HY_D, lb), F32),
                   jax.ShapeDtypeStruct((HY_ORDER, nlag, HY_D, 1), F32)],
        compiler_params=_cparams("parallel"),
        name="filter_blocks",
    )(ht, ht, seg, seg)


def _hyconv_kernel(v_ref, g_ref, bias_ref, kr_ref, ki_ref, kn_ref, wf_ref, wi_ref, o_ref,
                   vr_scr, vi_scr, vn_scr, y_scr):
    nb, ct, lb = v_ref.shape[1], v_ref.shape[2], v_ref.shape[3]
    bin0 = lax.broadcasted_iota(jnp.int32, (1, lb), 1) == 0
    v = v_ref[0]
    acc = _bdot(v.reshape(nb * ct, lb), wf_ref[...])
    imp = acc[:, lb:]
    vr_scr[...] = acc[:, :lb].reshape(nb, ct, lb)
    vi_scr[...] = jnp.where(bin0, 0.0, imp).reshape(nb, ct, lb)
    vn_scr[...] = imp[:, 0:1].reshape(nb, ct, 1)

    rt = min(ct, (8 * SUBLANE * LANE) // lb)
    nr = ct // rt

    def out_rows(n, carry):
        i = n // nr
        rows = pl.ds(pl.multiple_of((n % nr) * rt, rt), rt)
        yr = jnp.zeros((rt, lb), F32)
        yi = jnp.zeros((rt, lb), F32)
        yn = jnp.zeros((rt, 1), F32)
        for j in range(nb):
            d = i - j + (nb - 1)
            kr = kr_ref[0, d, rows, :]
            ki = ki_ref[0, d, rows, :]
            vr = vr_scr[j, rows, :]
            vi = vi_scr[j, rows, :]
            yr = yr + (vr * kr - vi * ki)
            yi = yi + (vr * ki + vi * kr)
            yn = yn + vn_scr[j, rows, :] * kn_ref[0, d, rows, :]
        y_scr[i, rows, :lb] = yr.astype(BF16)
        y_scr[i, rows, lb:] = jnp.where(bin0, yn, yi).astype(BF16)
        return carry

    lax.fori_loop(0, nb * nr, out_rows, 0)
    conv = _bdot(y_scr[...].reshape(nb * ct, 2 * lb), wi_ref[...]).reshape(nb, ct, lb)
    o_ref[0] = (g_ref[0].astype(F32) * (conv + v.astype(F32) * bias_ref[...])).astype(o_ref.dtype)


def _hyconv(v, gate, bias, kr, ki, kn, order, wf, wi):
    B, nb, C, lb = v.shape
    nlag = kr.shape[1]
    ct = 128
    blk = pl.BlockSpec((1, nb, ct, lb), lambda c, b: (b, 0, c, 0))
    kspec = pl.BlockSpec((1, nlag, ct, lb), lambda c, b: (order, 0, c, 0))
    return pl.pallas_call(
        _hyconv_kernel,
        grid=(C // ct, B),
        in_specs=[blk, blk,
                  pl.BlockSpec((ct, 1), lambda c, b: (c, 0)),
                  kspec, kspec,
                  pl.BlockSpec((1, nlag, ct, 1), lambda c, b: (order, 0, c, 0)),
                  pl.BlockSpec((lb, 2 * lb), lambda c, b: (0, 0)),
                  pl.BlockSpec((2 * lb, lb), lambda c, b: (0, 0))],
        out_specs=blk,
        out_shape=jax.ShapeDtypeStruct((B, nb, C, lb), BF16),
        scratch_shapes=[pltpu.VMEM((nb, ct, lb), F32), pltpu.VMEM((nb, ct, lb), F32),
                        pltpu.VMEM((nb, ct, 1), F32), pltpu.VMEM((nb, ct, 2 * lb), BF16)],
        compiler_params=_cparams("parallel", "parallel"),
        name="hyena_conv",
    )(v, gate, bias.reshape(C, 1), kr, ki, kn, wf, wi)


def _hyena(pt, conv_w, conv_b, hy_bias, filt):
    L = pt.shape[2]
    lb = HY_BLK if L >= 2 * HY_BLK else L // 2
    wf, wi = _dft_mats(lb, lb)
    ht = _hyena_filter(L, *filt)
    kr, ki, kn = _filter_blocks(ht, _segdft(ht, wf, lb), lb)
    x1, x2, v = _shortconv(pt, conv_w, conv_b, lb)
    z = _hyconv(v, x1, hy_bias[0], kr, ki, kn, 0, wf, wi)
    return _hyconv(z, x2, hy_bias[1], kr, ki, kn, 1, wf, wi)


def _s5_operators(lam_re, lam_im, log_step, b_re, b_im, c_re, c_im):
    lam = lax.complex(lam_re.astype(F32), lam_im.astype(F32))
    dt = jnp.exp(log_step.astype(F32))[..., None]
    lam_bar = jnp.exp(lam * dt)
    lam_t = jnp.exp(lam * dt * S5_CHUNK)
    b_bar = ((lam_bar - 1.0) / lam)[..., None] * lax.complex(b_re.astype(F32), b_im.astype(F32))
    c_mat = lax.complex(c_re.astype(F32), c_im.astype(F32))
    gh = S5_GROUPS // S5_HALVES
    eye = jnp.eye(gh, dtype=F32)

    def bd_in(t):
        t = t.reshape(2, S5_HALVES, gh, S5_STATE, S5_GROUP)
        return jnp.einsum('dfgpn,gh->dfgnhp', t, eye).reshape(2, S5_HALVES, LANE, S5_PH)

    def bd_out(t):
        t = t.reshape(2, S5_HALVES, gh, S5_GROUP, S5_STATE)
        return jnp.einsum('dfgnp,gh->dfgphn', t, eye).reshape(2, S5_HALVES, S5_PH, LANE)

    bbd = jnp.stack([bd_in(jnp.real(b_bar)), bd_in(jnp.imag(b_bar))], axis=1).astype(BF16)
    cbd = jnp.stack([bd_out(jnp.real(c_mat)), bd_out(-jnp.imag(c_mat))], axis=1).astype(BF16)
    flat = lambda z: jnp.stack([jnp.real(z), jnp.imag(z)], axis=1).reshape(2, 2, S5_P)
    return bbd, cbd, flat(lam_bar), flat(lam_t)


def _s5_kernel(ua_ref, ub_ref, s0_ref, bbd_ref, cbd_ref, lam_ref, lamt_ref, ya_ref, yb_ref, sfin_ref,
               sloc_scr, sinit_scr, bu_scr):
    T = S5_CHUNK
    PH = S5_PH
    M = ua_ref.shape[1] // T
    nt = M // SUBLANE
    rid = lax.broadcasted_iota(jnp.int32, (SUBLANE, PH), 0)
    combos = [(hf, d) for hf in range(S5_HALVES) for d in (0, 1)]
    u_refs = (ua_ref, ub_ref)
    y_refs = (ya_ref, yb_ref)
    re_sl = lambda hf: slice(hf * PH, (hf + 1) * PH)
    im_sl = lambda hf: slice(S5_P + hf * PH, S5_P + (hf + 1) * PH)
    order = lambda d: list(range(T)) if d == 0 else list(range(T - 1, -1, -1))

    def drive(hf, d, s):
        us = u_refs[hf][0, pl.ds(s, M, stride=T), :].astype(BF16)
        return _bdot(us, bbd_ref[d, 0, hf]), _bdot(us, bbd_ref[d, 1, hf])

    def advance(hf, d, sr, si, br, bi):
        lr = lam_ref[d, 0:1, re_sl(hf)]
        li = lam_ref[d, 1:2, re_sl(hf)]
        return lr * sr - li * si + br, lr * si + li * sr + bi

    for hf, d in combos:
        steps = order(d)
        for n, s in enumerate(steps):
            br, bi = drive(hf, d, s)
            bu_scr[s, :, :PH] = br
            bu_scr[s, :, PH:] = bi
            sr, si = (br, bi) if n == 0 else advance(hf, d, sr, si, br, bi)
        sloc_scr[:, :PH] = sr
        sloc_scr[:, PH:] = si

        ar = lamt_ref[d, 0:1, re_sl(hf)]
        ai = lamt_ref[d, 1:2, re_sl(hf)]

        def scan_tile(n, carry, d=d, ar=ar, ai=ai):
            cr, ci = carry
            base = pl.multiple_of((n if d == 0 else nt - 1 - n) * SUBLANE, SUBLANE)
            lr_t = sloc_scr[pl.ds(base, SUBLANE), :PH]
            li_t = sloc_scr[pl.ds(base, SUBLANE), PH:]
            out_r = jnp.zeros((SUBLANE, PH), F32)
            out_i = jnp.zeros((SUBLANE, PH), F32)
            for r in (range(SUBLANE) if d == 0 else range(SUBLANE - 1, -1, -1)):
                out_r = jnp.where(rid == r, cr, out_r)
                out_i = jnp.where(rid == r, ci, out_i)
                cr, ci = (ar * cr - ai * ci + lr_t[r:r + 1], ar * ci + ai * cr + li_t[r:r + 1])
            sinit_scr[pl.ds(base, SUBLANE), :PH] = out_r
            sinit_scr[pl.ds(base, SUBLANE), PH:] = out_i
            return cr, ci

        cr, ci = lax.fori_loop(0, nt, scan_tile, (s0_ref[0, d:d + 1, re_sl(hf)], s0_ref[0, d:d + 1, im_sl(hf)]))
        sfin_ref[0, d:d + 1, re_sl(hf)] = cr
        sfin_ref[0, d:d + 1, im_sl(hf)] = ci

        sr = sinit_scr[:, :PH]
        si = sinit_scr[:, PH:]
        for s in steps:
            sr, si = advance(hf, d, sr, si, bu_scr[s, :, :PH], bu_scr[s, :, PH:])
            ys = (_bdot(sr.astype(BF16), cbd_ref[d, 0, hf]) + _bdot(si.astype(BF16), cbd_ref[d, 1, hf]))
            if d == 0:
                y_refs[hf][0, pl.ds(s, M, stride=T), :] = ys
            else:
                y_refs[hf][0, pl.ds(s, M, stride=T), :] += ys


def _s5_pass(ua, ub, ops, s0):
    bbd, cbd, lam, lam_t = ops
    B, L, _ = ua.shape
    M = L // S5_CHUNK
    full = lambda nd: (lambda b: (0,) * nd)
    tok = pl.BlockSpec((1, L, LANE), lambda b: (b, 0, 0))
    st = pl.BlockSpec((1, 2, 2 * S5_P), lambda b: (b, 0, 0))
    return pl.pallas_call(
        _s5_kernel,
        grid=(B,),
        in_specs=[tok, tok, st,
                  pl.BlockSpec(bbd.shape, full(5)),
                  pl.BlockSpec(cbd.shape, full(5)),
                  pl.BlockSpec((2, 2, S5_P), full(3)),
                  pl.BlockSpec((2, 2, S5_P), full(3))],
        out_specs=[tok, tok, st],
        out_shape=[jax.ShapeDtypeStruct((B, L, LANE), F32),
                   jax.ShapeDtypeStruct((B, L, LANE), F32),
                   jax.ShapeDtypeStruct((B, 2, 2 * S5_P), F32)],
        scratch_shapes=[pltpu.VMEM((M, 2 * S5_PH), F32),
                        pltpu.VMEM((M, 2 * S5_PH), F32),
                        pltpu.VMEM((S5_CHUNK, M, 2 * S5_PH), F32)],
        compiler_params=_cparams("parallel"),
        name="s5",
    )(ua, ub, s0, bbd, cbd, lam, lam_t)


def _gelu_tanh(v):
    return 0.5 * v * (1.0 + jnp.tanh(math.sqrt(2.0 / math.pi) * (v + 0.044715 * v * v * v)))


def _outproj0_kernel(hy_ref, ya_ref, yb_ref, ua_ref, ub_ref, d_ref, gw_ref, gb_ref, wa_ref, wb_ref,
                     x_ref, m_ref, o_ref):
    ys = (jnp.concatenate([ya_ref[0], yb_ref[0]], axis=1)
          + d_ref[...] * jnp.concatenate([ua_ref[0], ub_ref[0]], axis=1))
    glu = _bdot(_gelu_tanh(ys).astype(BF16), gw_ref[...]) + gb_ref[...]
    s5 = glu[:, :S5_D] * _sigmoid(glu[:, S5_D:])
    hy = jnp.concatenate([hy_ref[0, a] for a in range(hy_ref.shape[1])], axis=1)
    y = lax.dot_general(hy, wa_ref[...], _TN, preferred_element_type=F32)
    y = y + _bdot(s5.astype(BF16), wb_ref[...])
    o_ref[0] = x_ref[0] + m_ref[0] * y


def _outproj0(hy, ys, u, d, glu_w, glu_b, w_out, x, m2):
    B, L, D = x.shape
    tl = _pick(L, 512, 256, 128)
    lb = hy.shape[3]
    kb = tl // lb
    full = lambda b, i: (0, 0)
    tok = lambda w: pl.BlockSpec((1, tl, w), lambda b, i: (b, i, 0))
    return pl.pallas_call(
        _outproj0_kernel,
        grid=(B, L // tl),
        in_specs=[pl.BlockSpec((1, kb, HY_D, lb), lambda b, i: (b, i, 0, 0)),
                  tok(LANE), tok(LANE), tok(LANE), tok(LANE),
                  pl.BlockSpec((1, S5_D), full),
                  pl.BlockSpec((S5_D, 2 * S5_D), full),
                  pl.BlockSpec((1, 2 * S5_D), full),
                  pl.BlockSpec((HY_D, D), full),
                  pl.BlockSpec((S5_D, D), full),
                  tok(D),
                  pl.BlockSpec((1, 1, D), lambda b, i: (b, 0, 0))],
        out_specs=tok(D),
        out_shape=jax.ShapeDtypeStruct((B, L, D), F32),
        compiler_params=_cparams("parallel", "parallel"),
        name="outproj0",
    )(hy, ys[0], ys[1], u[0], u[1], d.reshape(1, S5_D), glu_w.astype(BF16), glu_b.reshape(1, -1),
      w_out[:HY_D].astype(BF16), w_out[HY_D:].astype(BF16), x, m2)


def _moe_kernel(x_ref, g_ref, sc_ref, sh_ref, m_ref, wrh_ref, wrl_ref, br_ref, tril_ref,
                wg_ref, wu_ref, wd_ref, fg_ref, o_ref, hs_scr, gs_scr, pt_scr, ys_scr, seg_smem,
                *, ne, final_norm):
    j = pl.program_id(2)
    nj = pl.num_programs(2)
    tl = x_ref.shape[1]
    S = hs_scr.shape[0]

    @pl.when(j == 0)
    def _():
        hn = _rms_mod(x_ref[0], g_ref[...], sc_ref[0], sh_ref[0])
        hi = hn.astype(BF16)
        lo = (hn - hi.astype(F32)).astype(BF16)
        wrh = wrh_ref[...]
        logits = _bdot(hi, wrh) + _bdot(lo, wrh) + _bdot(hi, wrl_ref[...]) + br_ref[...]
        lane = lax.broadcasted_iota(jnp.int32, logits.shape, 1)
        neg = -jnp.inf
        lgm = jnp.where(lane < MOE_GROUPS, logits, neg)
        gmax = jnp.max(lgm, axis=1, keepdims=True)
        p_top = 1.0 / jnp.sum(jnp.exp(lgm - gmax), axis=1, keepdims=True)
        gidx = jnp.min(jnp.where(lgm == gmax, lane, LANE), axis=1, keepdims=True)
        elane = lane - MOE_GROUPS
        in_group = (elane >= gidx * MOE_EPG) & (elane < (gidx + 1) * MOE_EPG)
        lem = jnp.where(in_group, logits, neg)
        v1 = jnp.max(lem, axis=1, keepdims=True)
        i1 = jnp.min(jnp.where(lem == v1, lane, LANE), axis=1, keepdims=True)
        lem2 = jnp.where(lane == i1, neg, lem)
        v2 = jnp.max(lem2, axis=1, keepdims=True)
        i2 = jnp.min(jnp.where(lem2 == v2, lane, LANE), axis=1, keepdims=True)
        e2 = jnp.exp(v2 - v1)
        w1 = p_top / (1.0 + e2)
        gate = jnp.where(lane == i1, w1, jnp.where(lane == i2, w1 * e2, 0.0))
        onehot = (lane == gidx).astype(F32)
        rank = _bdot(tril_ref[...], onehot.astype(BF16))
        off = jnp.int32(0)
        offv = jnp.zeros((1, LANE), F32)
        for g in range(MOE_GROUPS):
            n_g = jnp.sum(onehot[:, g:g + 1]).astype(jnp.int32)
            seg_smem[g] = off
            seg_smem[MOE_GROUPS + g] = n_g
            offv = jnp.where(lane[0:1, :] == g, off.astype(F32), offv)
            off = off + ((n_g + MOE_ALIGN - 1) // MOE_ALIGN) * MOE_ALIGN
        dlane = onehot * (offv + rank)
        dest = jnp.sum(dlane, axis=1, keepdims=True).astype(jnp.int32)
        slot = lax.broadcasted_iota(jnp.int32, (tl, S), 1)
        pt_scr[...] = (slot == dest).astype(BF16)
        d_hi = jnp.floor(dlane * (1.0 / 256.0))
        ones = jnp.ones((SUBLANE, LANE), BF16)
        drow = (256.0 * lax.dot_general(ones, d_hi.astype(BF16), _NT, preferred_element_type=F32)
                + lax.dot_general(ones, (dlane - 256.0 * d_hi).astype(BF16), _NT, preferred_element_type=F32))
        srow = lax.broadcasted_iota(jnp.int32, (S, tl), 0)
        p = (srow == drow[0:1, :].astype(jnp.int32)).astype(BF16)
        hs_scr[...] = _bdot(p, hi).astype(BF16)
        ghi = gate.astype(BF16)
        glo = (gate - ghi.astype(F32)).astype(BF16)
        gs_scr[...] = _bdot(p, ghi) + _bdot(p, glo)
        ys_scr[...] = jnp.zeros_like(ys_scr)

    grp = j // (MOE_EPG // ne)
    start = pl.multiple_of(seg_smem[grp], MOE_ALIGN)
    units = (seg_smem[MOE_GROUPS + grp] + MOE_UNIT - 1) // MOE_UNIT
    lane1 = lax.broadcasted_iota(jnp.int32, (1, LANE), 1)

    def block(r0, rows):
        hs = hs_scr[pl.ds(r0, rows), :]
        gsb = gs_scr[pl.ds(r0, rows), :]
        acc = None
        for e in range(ne):
            a = _bdot(hs, wg_ref[e])
            u = _bdot(hs, wu_ref[e])
            gcol = jnp.sum(jnp.where(lane1 == MOE_GROUPS + j * ne + e, gsb, 0.0), axis=1, keepdims=True)
            y = _bdot((a * _sigmoid(a) * u * gcol).astype(BF16), wd_ref[e])
            acc = y if acc is None else acc + y
        ys_scr[pl.ds(r0, rows), :] = acc.astype(BF16)

    @pl.when(units <= 2)
    def _():
        block(start, 2 * MOE_UNIT)

    @pl.when(units == 3)
    def _():
        block(start, 3 * MOE_UNIT)

    @pl.when(units >= 4)
    def _():
        def body(i, carry):
            block(pl.multiple_of(start + i * (2 * MOE_UNIT), MOE_ALIGN), 2 * MOE_UNIT)
            return carry

        lax.fori_loop(0, (units + 1) // 2, body, 0)

    @pl.when(j == nj - 1)
    def _():
        moe = _bdot(pt_scr[...], ys_scr[...])
        out = x_ref[0] + m_ref[0] * moe
        if final_norm:
            ms = jnp.mean(out * out, axis=-1, keepdims=True)
            out = out * lax.rsqrt(ms + EPS) * fg_ref[...]
        o_ref[0] = out


def _moe(x, g2, sc, sh, m5, wrh, wrl, br, wg, wu, wd, final_g=None):
    B, L, D = x.shape
    ne = 8
    tl = _pick(L, 1024, 512, 256, 128)
    S = tl + 3 * MOE_UNIT
    final_norm = final_g is not None
    fg = (final_g if final_norm else jnp.ones((D,), F32)).reshape(1, D)
    idx = jnp.arange(tl, dtype=jnp.int32)
    tril = (idx[None, :] < idx[:, None]).astype(BF16)
    full = lambda b, i, j: (0, 0)
    tok = pl.BlockSpec((1, tl, D), lambda b, i, j: (b, i, 0))
    tok_in = pl.BlockSpec((1, tl, D), lambda b, i, j: (b, i, 0), pipeline_mode=pl.Buffered(1))
    vec = pl.BlockSpec((1, 1, D), lambda b, i, j: (b, 0, 0))
    wspec = lambda k, n: pl.BlockSpec((ne, k, n), lambda b, i, j: (j, 0, 0))
    return pl.pallas_call(
        functools.partial(_moe_kernel, ne=ne, final_norm=final_norm),
        grid=(B, L // tl, MOE_EXPERTS // ne),
        in_specs=[tok_in, pl.BlockSpec((1, D), full), vec, vec, vec,
                  pl.BlockSpec((D, LANE), full), pl.BlockSpec((D, LANE), full), pl.BlockSpec((1, LANE), full),
                  pl.BlockSpec((tl, tl), full, pipeline_mode=pl.Buffered(1)),
                  wspec(D, MOE_HIDDEN), wspec(D, MOE_HIDDEN), wspec(MOE_HIDDEN, D),
                  pl.BlockSpec((1, D), full)],
        out_specs=tok,
        out_shape=jax.ShapeDtypeStruct((B, L, D), F32),
        scratch_shapes=[pltpu.VMEM((S, D), BF16),
                        pltpu.VMEM((S, LANE), F32),
                        pltpu.VMEM((tl, S), BF16),
                        pltpu.VMEM((S, D), BF16),
                        pltpu.SMEM((2 * MOE_GROUPS,), jnp.int32)],
        compiler_params=_cparams("parallel", "parallel", "arbitrary"),
        name="moe",
    )(x, g2, sc, sh, m5, wrh, wrl, br, tril, wg, wu, wd, fg)


def _moe_params(wg, bg, we, be, w_gate, w_up, w_down):
    D = wg.shape[0]
    pad = LANE - MOE_GROUPS - MOE_EXPERTS
    wr = jnp.pad(jnp.concatenate([wg, we], axis=1), ((0, 0), (0, pad)))
    br = jnp.pad(jnp.concatenate([bg, be]), (0, pad)).reshape(1, LANE)
    wrh = wr.astype(BF16)
    wrl = (wr - wrh.astype(F32)).astype(BF16)
    return (wrh, wrl, br,
            w_gate.reshape(MOE_EXPERTS, D, MOE_HIDDEN).astype(BF16),
            w_up.reshape(MOE_EXPERTS, D, MOE_HIDDEN).astype(BF16),
            w_down.reshape(MOE_EXPERTS, MOE_HIDDEN, D).astype(BF16))


def _inproj1_kernel(x_ref, g_ref, sc_ref, sh_ref, w_ref, o_ref, h_scr):
    @pl.when(pl.program_id(2) == 0)
    def _():
        h_scr[...] = _rms_mod(x_ref[0], g_ref[...], sc_ref[0], sh_ref[0]).astype(BF16)

    o_ref[0] = _bdot(h_scr[...], w_ref[...]).astype(o_ref.dtype)


def _inproj1(x, g, sc, sh, w):
    B, L, D = x.shape
    N = w.shape[1]
    tl = _pick(L, 1024, 512, 256, 128)
    tn = _pick(N, 2560, 1280, 1024, 512)
    return pl.pallas_call(
        _inproj1_kernel,
        grid=(B, L // tl, N // tn),
        in_specs=[pl.BlockSpec((1, tl, D), lambda b, i, j: (b, i, 0)),
                  pl.BlockSpec((1, D), lambda b, i, j: (0, 0)),
                  pl.BlockSpec((1, 1, D), lambda b, i, j: (b, 0, 0)),
                  pl.BlockSpec((1, 1, D), lambda b, i, j: (b, 0, 0)),
                  pl.BlockSpec((D, tn), lambda b, i, j: (0, j))],
        out_specs=pl.BlockSpec((1, tl, tn), lambda b, i, j: (b, i, j)),
        out_shape=jax.ShapeDtypeStruct((B, L, N), BF16),
        scratch_shapes=[pltpu.VMEM((tl, D), BF16)],
        compiler_params=_cparams("parallel", "parallel", "arbitrary"),
        name="inproj1",
    )(x, g, sc, sh, w)


def _gla_masks(C):
    t = np.arange(C)[:, None]
    s = np.arange(C)[None, :]
    fwd = []
    hs = 1
    while hs < C:
        fwd.append(((t // (2 * hs)) == (s // (2 * hs))) & ((t & hs) != 0) & ((s & hs) == 0))
        hs *= 2
    fwd = np.stack(fwd).astype(np.float32)
    return jnp.asarray(np.stack([fwd, fwd.transpose(0, 2, 1)])), jnp.asarray(np.eye(C, dtype=np.float32))


def _gla_direction(pq, v, z, lb, masks, eye, st_ref, rev):
    C = pq.shape[0]
    heads = [slice(h * LANE, (h + 1) * LANE) for h in range(C_HEADS)]
    q = pq * _sigmoid(pq)
    f = lb + (1.0 - lb) * _sigmoid(z)
    k = 1.0 - f
    g = jnp.log2(f)
    row = lax.broadcasted_iota(jnp.int32, (C, C_D), 0)
    qb = q.astype(BF16)
    kb = k.astype(BF16)
    att = [eye * lax.dot_general(qb[:, hd], kb[:, hd], _NT, preferred_element_type=F32) for hd in heads]
    pinc = g
    tot = g
    hs = 1
    lvl = 0
    while hs < C:
        aq = pinc
        ak = tot - pinc
        qe = (q * jnp.exp2(aq)).astype(BF16)
        ke = (k * jnp.exp2(ak)).astype(BF16)
        for h, hd in enumerate(heads):
            blk = lax.dot_general(qe[:, hd], ke[:, hd], _NT, preferred_element_type=F32)
            att[h] = jnp.where(masks[lvl], blk, att[h])
        if hs < SUBLANE:
            odd = (row & hs) != 0
            t3 = tot.reshape(C // SUBLANE, SUBLANE, C_D)
            tprev = pltpu.roll(t3, hs, 1).reshape(C, C_D)
            tnext = pltpu.roll(t3, SUBLANE - hs, 1).reshape(C, C_D)
            pinc = pinc + (jnp.where(odd, 0.0, tnext) if rev else jnp.where(odd, tprev, 0.0))
            tot = tot + jnp.where(odd, tprev, tnext)
        else:
            w = hs // SUBLANE
            tile = lambda a, i: a[i * SUBLANE:(i + 1) * SUBLANE]
            new_p, new_t = [], []
            for i in range(C // SUBLANE):
                sib = i - w if (i // w) % 2 == 1 else i + w
                grows = ((i // w) % 2 == 1) != rev
                new_p.append(tile(pinc, i) + tile(tot, sib) if grows else tile(pinc, i))
                new_t.append(tile(tot, i) + tile(tot, sib))
            pinc = jnp.concatenate(new_p, axis=0)
            tot = jnp.concatenate(new_t, axis=0)
        hs *= 2
        lvl += 1
    q_dec = pinc
    k_dec = tot - pinc
    vb = v.astype(BF16)
    qd = (q * jnp.exp2(q_dec)).astype(BF16)
    kd = (k * jnp.exp2(k_dec)).astype(BF16)
    keep = jnp.exp2(tot[0:1, :])
    outs = []
    for h, hd in enumerate(heads):
        st = st_ref[h]
        o = _bdot(att[h].astype(BF16), vb[:, hd])
        o = o + lax.dot_general(qd[:, hd], st.astype(BF16), _NT, preferred_element_type=F32)
        upd = lax.dot_general(vb[:, hd], kd[:, hd], _TN, preferred_element_type=F32)
        st_ref[h] = st * keep[:, hd] + upd
        outs.append(o)
    return jnp.concatenate(outs, axis=1)


def _gla_kernel(qf_ref, if_ref, zf_ref, qb_ref, ib_ref, zb_ref, lb_ref, mk_ref, eye_ref, s0_ref,
                of_ref, ob_ref, sfin_ref, st_scr):
    c = pl.program_id(1)

    @pl.when(c == 0)
    def _():
        st_scr[...] = s0_ref[0]

    eye = eye_ref[...]
    nlev = mk_ref.shape[1]
    for d, (q_ref, i_ref, z_ref, o_ref) in enumerate(((qf_ref, if_ref, zf_ref, of_ref),
                                                      (qb_ref, ib_ref, zb_ref, ob_ref))):
        masks = [mk_ref[d, lv] > 0.5 for lv in range(nlev)]
        o = _gla_direction(q_ref[0].astype(F32), i_ref[0].astype(F32), z_ref[0].astype(F32),
                           lb_ref[d:d + 1, :], masks, eye, st_scr.at[d], rev=(d == 1))
        o_ref[0] = o.astype(o_ref.dtype)

    @pl.when(c == pl.num_programs(1) - 1)
    def _():
        sfin_ref[0] = st_scr[...]


def _gla(p, lb, s0):
    B, L, _ = p.shape
    C = GLA_CHUNK
    nch = L // C
    masks, eye = _gla_masks(C)
    fcol = lambda k: pl.BlockSpec((1, C, C_D), lambda b, c: (b, c, k))
    bcol = lambda k: pl.BlockSpec((1, C, C_D), lambda b, c: (b, nch - 1 - c, k))
    st_spec = pl.BlockSpec((1, 2, C_HEADS, LANE, LANE), lambda b, c: (b, 0, 0, 0, 0))
    return pl.pallas_call(
        _gla_kernel,
        grid=(B, nch),
        in_specs=[fcol(0), fcol(2), fcol(3), bcol(0), bcol(2), bcol(4),
                  pl.BlockSpec((2, C_D), lambda b, c: (0, 0)),
                  pl.BlockSpec(masks.shape, lambda b, c: (0, 0, 0, 0)),
                  pl.BlockSpec((C, C), lambda b, c: (0, 0)),
                  st_spec],
        out_specs=[fcol(0), bcol(0), st_spec],
        out_shape=[jax.ShapeDtypeStruct((B, L, C_D), BF16),
                   jax.ShapeDtypeStruct((B, L, C_D), BF16),
                   jax.ShapeDtypeStruct((B, 2, C_HEADS, LANE, LANE), F32)],
        scratch_shapes=[pltpu.VMEM((2, C_HEADS, LANE, LANE), F32)],
        compiler_params=_cparams("parallel", "arbitrary"),
        name="gla",
    )(p, p, p, p, p, p, lb, masks, eye, s0)


def _outproj1_kernel(of_ref, ob_ref, g_ref, ng_ref, w_ref, x_ref, m_ref, o_ref):
    o = of_ref[0].astype(F32) + ob_ref[0].astype(F32)
    parts = []
    for h in range(C_HEADS):
        oh = o[:, h * LANE:(h + 1) * LANE]
        ms = jnp.mean(oh * oh, axis=-1, keepdims=True)
        parts.append(oh * lax.rsqrt(ms + EPS))
    on = jnp.concatenate(parts, axis=1) * ng_ref[...] * _sigmoid(g_ref[0].astype(F32))
    y = _bdot(on.astype(BF16), w_ref[...])
    o_ref[0] = x_ref[0] + m_ref[0] * y


def _outproj1(o_f, o_b, p, norm_g, w_out, x, m2):
    B, L, D = x.shape
    tl = _pick(L, 512, 256, 128)
    full = lambda b, i: (0, 0)
    tok = pl.BlockSpec((1, tl, C_D), lambda b, i: (b, i, 0))
    return pl.pallas_call(
        _outproj1_kernel,
        grid=(B, L // tl),
        in_specs=[tok, tok,
                  pl.BlockSpec((1, tl, C_D), lambda b, i: (b, i, 1)),
                  pl.BlockSpec((1, C_D), full),
                  pl.BlockSpec((C_D, D), full),
                  pl.BlockSpec((1, tl, D), lambda b, i: (b, i, 0)),
                  pl.BlockSpec((1, 1, D), lambda b, i: (b, 0, 0))],
        out_specs=pl.BlockSpec((1, tl, D), lambda b, i: (b, i, 0)),
        out_shape=jax.ShapeDtypeStruct((B, L, D), F32),
        compiler_params=_cparams("parallel", "parallel"),
        name="outproj1",
    )(o_f, o_b, p, norm_g.reshape(1, C_D), w_out.astype(BF16), x, m2)


def _mods(cmat, w, b, nb):
    R = cmat.shape[0]
    pad = (-R) % SUBLANE
    m = _modvec(jnp.pad(cmat, ((0, pad), (0, 0))), w, b)[:R]
    m = jnp.broadcast_to(m, (nb, m.shape[1])) if R == 1 else m
    return [m[:, None, k * D_MODEL:(k + 1) * D_MODEL] for k in range(6)]


def kernel(x, c, ctx, c_ctx, mod_w, mod_b, norm1_g, norm2_g, final_g,
           ab_w_in, ab_w_out, hy_conv_w, hy_conv_b, hy_fw1, hy_fb1, hy_ff1,
           hy_fw2, hy_fb2, hy_ff2, hy_fw3, hy_bias,
           s5_lam_re, s5_lam_im, s5_log_step, s5_b_re, s5_b_im, s5_c_re, s5_c_im,
           s5_d, s5_glu_w, s5_glu_b,
           c_w_in, c_w_out, c_lower_bounds, c_norm_g,
           moe_wg, moe_bg, moe_we, moe_be, moe_w_gate, moe_w_up, moe_w_down):
    B, L, D = x.shape
    row = lambda t: t.reshape(1, -1)

    m = _mods(c, mod_w[0], mod_b[0], B)
    mc = _mods(c_ctx[None, :], mod_w[0], mod_b[0], B)
    w_in = ab_w_in[0]
    wt_hy = w_in[:, :3 * HY_D].T.astype(BF16)
    w_s5 = w_in[:, 3 * HY_D:].astype(BF16)
    g1 = row(norm1_g[0])
    pt_c, *u_c = _inproj0(ctx, g1, 1.0 + mc[1], mc[0], wt_hy, w_s5)
    pt_l, *u_l = _inproj0(x, g1, 1.0 + m[1], m[0], wt_hy, w_s5)

    ops = _s5_operators(s5_lam_re[0], s5_lam_im[0], s5_log_step[0], s5_b_re[0], s5_b_im[0],
                        s5_c_re[0], s5_c_im[0])
    *ys_c, s_fin = _s5_pass(*u_c, ops, jnp.zeros((B, 2, 2 * S5_P), F32))
    *ys_l, _ = _s5_pass(*u_l, ops, s_fin)

    filt = (hy_fw1[0], hy_fb1[0], hy_ff1[0], hy_fw2[0], hy_fb2[0], hy_ff2[0], hy_fw3[0])
    hy_c = _hyena(pt_c, hy_conv_w[0], hy_conv_b[0], hy_bias[0], filt)
    hy_l = _hyena(pt_l, hy_conv_w[0], hy_conv_b[0], hy_bias[0], filt)

    x = _outproj0(hy_l, ys_l, u_l, s5_d[0], s5_glu_w[0], s5_glu_b[0], ab_w_out[0], x, m[2])
    ctx = _outproj0(hy_c, ys_c, u_c, s5_d[0], s5_glu_w[0], s5_glu_b[0], ab_w_out[0], ctx, mc[2])

    mp = _moe_params(moe_wg[0], moe_bg[0], moe_we[0], moe_be[0],
                     moe_w_gate[0], moe_w_up[0], moe_w_down[0])
    g2 = row(norm2_g[0])
    x = _moe(x, g2, 1.0 + m[4], m[3], m[5], *mp)
    ctx = _moe(ctx.reshape(1, -1, D), g2, 1.0 + mc[4][:1], mc[3][:1], mc[5][:1], *mp).reshape(ctx.shape)

    m = _mods(c, mod_w[1], mod_b[1], B)
    mc = _mods(c_ctx[None, :], mod_w[1], mod_b[1], B)
    sm = jax.nn.softmax(c_lower_bounds.astype(F32), axis=1)
    lower = (jnp.cumsum(sm, axis=1) - sm[:, :1])[:, 1]
    g1 = row(norm1_g[1])
    w1 = c_w_in[0].astype(BF16)
    p_c = _inproj1(ctx, g1, 1.0 + mc[1], mc[0], w1)
    p_l = _inproj1(x, g1, 1.0 + m[1], m[0], w1)
    zeros = jnp.zeros((B, 2, C_HEADS, C_HEAD_DIM, C_HEAD_DIM), F32)
    _, _, s_ctx = _gla(p_c, lower, zeros)
    o_f, o_b, _ = _gla(p_l, lower, s_ctx)
    x = _outproj1(o_f, o_b, p_l, c_norm_g[0], c_w_out[0], x, m[2])

    mp = _moe_params(moe_wg[1], moe_bg[1], moe_we[1], moe_be[1],
                     moe_w_gate[1], moe_w_up[1], moe_w_down[1])
    return _moe(x, row(norm2_g[1]), 1.0 + m[4], m[3], m[5], *mp, final_g=final_g)
```
